```python
import math
import jax, jax.numpy as jnp
from jax import lax
import numpy as np

D_MODEL = 1024
BATCH = 4
SEQ = 8192
DEPTH = 2

GRID_W = 64
CTX_LEN = 256
CHUNK = 128
Q_BLOCK = 128
EPS = 1e-6
ROPE_BASE = 10000.0
ML_HEADS = 4
ML_DQK = D_MODEL // (2 * ML_HEADS)
ML_DV = D_MODEL // (2 * ML_HEADS)
RET_HEADS = 4
RET_DQK = D_MODEL // (2 * RET_HEADS)
RET_DV = D_MODEL // (2 * RET_HEADS)
DA_HEADS = 8
DA_DHEAD = D_MODEL // (2 * DA_HEADS)
DA_DV = 2 * DA_DHEAD
N_GROUPS = 4
EXPERTS_PER_GROUP = 8
N_EXPERTS = N_GROUPS * EXPERTS_PER_GROUP
TOP_K = 2
D_EXPERT = D_MODEL // 2
MOE_BLOCK = 256
N_EVEN = (DEPTH + 1) // 2
N_ODD = DEPTH // 2
EVEN_SPLITS = (ML_HEADS * ML_DQK, ML_HEADS * ML_DQK, ML_HEADS * ML_DV, ML_HEADS * ML_DV,
               2 * ML_HEADS, 2 * ML_HEADS,
               RET_HEADS * RET_DQK, RET_HEADS * RET_DQK, RET_HEADS * RET_DV, RET_HEADS * RET_DV)
EVEN_IN = sum(EVEN_SPLITS)
EVEN_SPLIT_IDX = tuple(int(s) for s in np.cumsum(EVEN_SPLITS)[:-1])
EVEN_MIX_W = ML_HEADS * ML_DV + RET_HEADS * RET_DV
ODD_IN = 4 * DA_HEADS * DA_DHEAD + DA_HEADS * DA_DV
ODD_MIX_W = DA_HEADS * DA_DV

kernel_name = 'hybrid_mlstm_retention_diffattn_hmoe_dit'

F32 = jnp.float32


def rms_norm(x, g):
    xf = x.astype(F32)
    y = xf * lax.rsqrt(jnp.mean(xf * xf, -1, keepdims=True) + EPS)
    return (y * g.astype(F32)).astype(x.dtype)


def adaln(h, g, shift, scale):
    return rms_norm(h, g) * (1 + scale) + shift


def head_layer_norm(x, g):
    xf = x.astype(F32)
    mu = jnp.mean(xf, -1, keepdims=True)
    var = jnp.mean(jnp.square(xf - mu), -1, keepdims=True)
    return ((xf - mu) * lax.rsqrt(var + EPS) * g[:, None, :].astype(F32)).astype(x.dtype)


def head_rms_norm(x, g):
    xf = x.astype(F32)
    y = xf * lax.rsqrt(jnp.mean(xf * xf, -1, keepdims=True) + EPS)
    return (y * g[:, None, :].astype(F32)).astype(x.dtype)


def modulation(cond, w, b):
    m = jax.nn.silu(cond) @ w + b
    return m.reshape(*cond.shape[:-1], 6, D_MODEL)


def rope_angles(pos, n_freq):
    inv = ROPE_BASE ** (-jnp.arange(n_freq, dtype=F32) / n_freq)
    return pos.astype(F32)[:, None] * inv[None, :]


def apply_rotary(x, ang):
    x1, x2 = jnp.split(x.astype(F32), 2, axis=-1)
    cos, sin = jnp.cos(ang), jnp.sin(ang)
    return jnp.concatenate([x1 * cos - x2 * sin, x1 * sin + x2 * cos], -1).astype(x.dtype)


def rope_2d(x, rows, cols):
    half = x.shape[-1] // 2
    nf = half // 2
    return jnp.concatenate([apply_rotary(x[..., :half], rope_angles(rows, nf)),
                            apply_rotary(x[..., half:], rope_angles(cols, nf))], -1)


def to_chunks(a):
    B, H, T = a.shape[:3]
    return jnp.moveaxis(a.reshape(B, H, T // CHUNK, CHUNK, *a.shape[3:]), 2, 0)


def from_chunks(a):
    a = jnp.moveaxis(a, 0, 2)
    return a.reshape(a.shape[0], a.shape[1], -1, a.shape[-1])


def mlstm_scan(q, k, v, ig, lf, state):
    xs = tuple(to_chunks(a.astype(F32)) for a in (q, k, v, ig, lf))
    causal = jnp.tril(jnp.ones((CHUNK, CHUNK), bool))

    def step(carry, inp):
        C, n, m = carry
        qc, kc, vc, ic, fc = inp
        b = jnp.cumsum(fc, -1)
        dlog = jnp.where(causal, b[..., :, None] - b[..., None, :] + ic[..., None, :], -jnp.inf)
        inter = b + m[..., None]
        m_t = jnp.maximum(inter, jnp.max(dlog, -1))
        s = jnp.einsum('bhtk,bhsk->bhts', qc, kc) * jnp.exp(dlog - m_t[..., None])
        w_inter = jnp.exp(inter - m_t)
        numer = jnp.einsum('bhts,bhsv->bhtv', s, vc) + w_inter[..., None] * jnp.einsum('bhtk,bhvk->bhtv', qc, C)
        denom = jnp.sum(s, -1) + w_inter * jnp.einsum('bhtk,bhk->bht', qc, n)
        h = numer / jnp.maximum(jnp.abs(denom), jnp.exp(-m_t))[..., None]
        b_last = b[..., -1]
        w_log = b_last[..., None] - b + ic
        m_new = jnp.maximum(b_last + m, jnp.max(w_log, -1))
        decay = jnp.exp(b_last + m - m_new)
        w = jnp.exp(w_log - m_new[..., None])
        C = decay[..., None, None] * C + jnp.einsum('bhsv,bhsk->bhvk', vc * w[..., None], kc)
        n = decay[..., None] * n + jnp.einsum('bhs,bhsk->bhk', w, kc)
        return (C, n, m_new), h

    state, h = lax.scan(step, state, xs)
    return from_chunks(h).astype(q.dtype), state


def retention_scan(q, k, v, log_gamma, R):
    xs = tuple(to_chunks(a.astype(F32)) for a in (q, k, v))
    lg = log_gamma.astype(F32)
    idx = jnp.arange(CHUNK, dtype=F32)
    causal = jnp.tril(jnp.ones((CHUNK, CHUNK), bool))
    dist = jnp.maximum(idx[:, None] - idx[None, :], 0.0)
    intra = jnp.where(causal, jnp.exp(lg[:, None, None] * dist), 0.0)
    q_decay = jnp.exp(lg[:, None] * (idx + 1.0))
    k_decay = jnp.exp(lg[:, None] * (CHUNK - 1.0 - idx))
    chunk_decay = jnp.exp(lg * CHUNK)

    def step(R, inp):
        qc, kc, vc = inp
        s = jnp.einsum('bhtk,bhsk->bhts', qc, kc) * intra
        o = jnp.einsum('bhts,bhsv->bhtv', s, vc) + q_decay[..., None] * jnp.einsum('bhtk,bhkv->bhtv', qc, R)
        R = chunk_decay[:, None, None] * R + jnp.einsum('bhsk,bhsv->bhkv', kc * k_decay[..., None], vc)
        return R, o

    R, o = lax.scan(step, R, xs)
    return from_chunks(o).astype(q.dtype), R


def mlstm_retention_mixer(u_ctx, u_lat, w_in, gate_b, ml_g, ret_ld, ret_g, w_out):
    B, L, _ = u_ctx.shape
    S = u_lat.shape[1]

    def project(u, pos):
        T = u.shape[1]
        mq, mk, mv, mo, mi, mf, rq, rk, rv, rg = jnp.split(u @ w_in, EVEN_SPLIT_IDX, axis=-1)
        heads = lambda a, H: a.reshape(B, T, H, -1).transpose(0, 2, 1, 3)
        ang = rope_angles(pos, RET_DQK // 2)
        ig = (mi.reshape(B, T, 2, ML_HEADS) + gate_b[:, 0]).astype(F32).transpose(2, 0, 3, 1)
        lf = jax.nn.log_sigmoid((mf.reshape(B, T, 2, ML_HEADS) + gate_b[:, 1]).astype(F32)).transpose(2, 0, 3, 1)
        return dict(mq=heads(mq, ML_HEADS), mk=heads(mk, ML_HEADS) * (ML_DQK ** -0.5), mv=heads(mv, ML_HEADS),
                    mo=jax.nn.sigmoid(heads(mo, ML_HEADS)), mi=ig, mf=lf,
                    rq=apply_rotary(heads(rq, RET_HEADS), ang),
                    rk=apply_rotary(heads(rk, RET_HEADS), ang) * (RET_DQK ** -0.5),
                    rv=heads(rv, RET_HEADS), rg=rg)

    pc = project(u_ctx, jnp.arange(L))
    pl = project(u_lat, L + jnp.arange(S))
    ml_c = ml_l = ret_c = ret_l = 0.0
    for d in range(2):
        fl = (lambda a: a) if d == 0 else (lambda a: jnp.flip(a, 2))
        st0 = (jnp.zeros((B, ML_HEADS, ML_DV, ML_DQK), F32), jnp.zeros((B, ML_HEADS, ML_DQK), F32),
               jnp.zeros((B, ML_HEADS), F32))
        hc, st = mlstm_scan(fl(pc['mq']), fl(pc['mk']), fl(pc['mv']), fl(pc['mi'][d]), fl(pc['mf'][d]), st0)
        hl, _ = mlstm_scan(fl(pl['mq']), fl(pl['mk']), fl(pl['mv']), fl(pl['mi'][d]), fl(pl['mf'][d]), st)
        ml_c = ml_c + fl(hc)
        ml_l = ml_l + fl(hl)
        R0 = jnp.zeros((B, RET_HEADS, RET_DQK, RET_DV), F32)
        oc, R = retention_scan(fl(pc['rq']), fl(pc['rk']), fl(pc['rv']), ret_ld[d], R0)
        ol, _ = retention_scan(fl(pl['rq']), fl(pl['rk']), fl(pl['rv']), ret_ld[d], R)
        ret_c = ret_c + fl(oc)
        ret_l = ret_l + fl(ol)

    def merge(p, ml, ret):
        T = ml.shape[2]
        ml = head_layer_norm(p['mo'] * ml, ml_g).transpose(0, 2, 1, 3).reshape(B, T, -1)
        ret = jax.nn.silu(p['rg']) * head_layer_norm(ret, ret_g).transpose(0, 2, 1, 3).reshape(B, T, -1)
        return jnp.concatenate([ml, ret], -1) @ w_out

    return merge(pc, ml_c, ret_c), merge(pl, ml_l, ret_l)


def diff_attend(q, k, v, lam):
    B, H, _, T, d = q.shape
    nb = T // Q_BLOCK
    qb = jnp.moveaxis(q.reshape(B, H, 2, nb, Q_BLOCK, d), 3, 0)
    scale = d ** -0.5

    def block(qi):
        s = jnp.einsum('bhmqd,bhmkd->bhmqk', qi, k).astype(F32) * scale
        p = jax.nn.softmax(s, -1)
        a = p[:, :, 0] - lam * p[:, :, 1]
        return jnp.einsum('bhqk,bhkv->bhqv', a.astype(v.dtype), v)

    o = lax.map(block, qb)
    return jnp.moveaxis(o, 0, 2).reshape(B, H, T, -1)


def diff_attention_mixer(u_ctx, u_lat, rows, cols, w_in, lam_p, norm_g, w_out, lam_init, with_ctx):
    B = u_ctx.shape[0]

    def project(u):
        T = u.shape[1]
        q, k, v = jnp.split(u @ w_in, [2 * DA_HEADS * DA_DHEAD, 4 * DA_HEADS * DA_DHEAD], axis=-1)
        q = q.reshape(B, T, DA_HEADS, 2, DA_DHEAD).transpose(0, 2, 3, 1, 4)
        k = k.reshape(B, T, DA_HEADS, 2, DA_DHEAD).transpose(0, 2, 3, 1, 4)
        v = v.reshape(B, T, DA_HEADS, DA_DV).transpose(0, 2, 1, 3)
        return q, k, v

    qc, kc, vc = project(u_ctx)
    ql, kl, vl = project(u_lat)
    ql = rope_2d(ql, rows, cols)
    kl = rope_2d(kl, rows, cols)
    lp = lam_p.astype(F32)
    lam = jnp.exp(jnp.sum(lp[0] * lp[1])) - jnp.exp(jnp.sum(lp[2] * lp[3])) + lam_init

    def finish(o):
        T = o.shape[2]
        o = head_rms_norm(o, norm_g) * (1.0 - lam_init)
        return o.transpose(0, 2, 1, 3).reshape(B, T, -1) @ w_out

    y_lat = finish(diff_attend(ql, jnp.concatenate([kl, kc], 3), jnp.concatenate([vl, vc], 2), lam))
    y_ctx = finish(diff_attend(qc, kc, vc, lam)) if with_ctx else None
    return y_ctx, y_lat


def routed_experts(x, e, g, w1, w3, w2):
    N, D = x.shape
    A = e.shape[0]
    nblk = -(-A // MOE_BLOCK) + N_EXPERTS
    P = nblk * MOE_BLOCK
    tok = jnp.repeat(jnp.arange(N, dtype=jnp.int32), TOP_K)
    order = jnp.argsort(e)
    e_s = e[order]
    counts = jnp.bincount(e, length=N_EXPERTS)
    padded = (counts + MOE_BLOCK - 1) // MOE_BLOCK * MOE_BLOCK
    start = jnp.cumsum(counts) - counts
    pend = jnp.cumsum(padded)
    dest = (pend - padded)[e_s] + jnp.arange(A) - start[e_s]
    buf_tok = jnp.full((P,), N, jnp.int32).at[dest].set(tok[order])
    buf_g = jnp.zeros((P,), x.dtype).at[dest].set(g[order].astype(x.dtype))
    blk_expert = jnp.clip(jnp.searchsorted(pend, jnp.arange(nblk) * MOE_BLOCK, side='right'), 0, N_EXPERTS - 1)
    xp = jnp.concatenate([x, jnp.zeros((1, D), x.dtype)], 0)
    xb = xp[buf_tok].reshape(nblk, MOE_BLOCK, D)

    def expert_block(args):
        xi, ei = args
        return (jax.nn.silu(xi @ w1[ei]) * (xi @ w3[ei])) @ w2[ei]

    yb = lax.map(expert_block, (xb, blk_expert)).reshape(P, D)
    out = jnp.zeros((N + 1, D), x.dtype).at[buf_tok].add(yb * buf_g[:, None])
    return out[:N]


def hmoe(x, wg, bg, we, be, w1, w3, w2):
    N = x.shape[0]
    pg = jax.nn.softmax((x @ wg).astype(F32) + bg.astype(F32), -1)
    pg_top, grp = lax.top_k(pg, 1)
    le = ((x @ we).astype(F32) + be.astype(F32)).reshape(N, N_GROUPS, EXPERTS_PER_GROUP)
    le = jnp.take_along_axis(le, grp[:, :, None], 1)[:, 0]
    pe_top, e_in = lax.top_k(jax.nn.softmax(le, -1), TOP_K)
    gate = pg_top * pe_top / jnp.sum(pe_top, -1, keepdims=True)
    expert = grp * EXPERTS_PER_GROUP + e_in
    return routed_experts(x, expert.reshape(-1).astype(jnp.int32), gate.reshape(-1), w1, w3, w2)


def setup_inputs(seed: int = 0) -> dict:
    key = jax.random.key(seed)
    ks = iter(jax.random.split(key, 32))
    nrm = lambda shape, s: jax.random.normal(next(ks), shape, F32) * s
    D = D_MODEL
    x = nrm((BATCH, SEQ, D), 1.0)
    c = nrm((BATCH, D), 1.0)
    ctx = nrm((BATCH, CTX_LEN, D), 1.0)
    c_ctx = nrm((D,), 1.0)
    w_mod = nrm((DEPTH, D, 6 * D), 0.5 * D ** -0.5)
    b_mod = nrm((DEPTH, 6 * D), 0.02)
    norm1_g = 1.0 + nrm((DEPTH, D), 0.05)
    norm2_g = 1.0 + nrm((DEPTH, D), 0.05)
    w_in_even = nrm((N_EVEN, D, EVEN_IN), D ** -0.5)
    gate_base = jnp.stack([jnp.zeros((ML_HEADS,), F32), jnp.linspace(3.0, 6.0, ML_HEADS)])
    ml_gate_b = gate_base[None, None] + nrm((N_EVEN, 2, 2, ML_HEADS), 0.1)
    ml_norm_g = 1.0 + nrm((N_EVEN, ML_HEADS, ML_DV), 0.05)
    base_ld = jnp.log1p(-(2.0 ** (-jnp.linspace(5.0, 12.0, RET_HEADS))))
    ret_log_decay = base_ld[None, None, :] * jnp.exp(nrm((N_EVEN, 2, RET_HEADS), 0.1))
    ret_norm_g = 1.0 + nrm((N_EVEN, RET_HEADS, RET_DV), 0.05)
    w_out_even = nrm((N_EVEN, EVEN_MIX_W, D), EVEN_MIX_W ** -0.5)
    w_in_odd = nrm((N_ODD, D, ODD_IN), D ** -0.5)
    da_lambda = nrm((N_ODD, 4, DA_DHEAD), 0.1)
    da_norm_g = 1.0 + nrm((N_ODD, DA_HEADS, DA_DV), 0.05)
    w_out_odd = nrm((N_ODD, ODD_MIX_W, D), ODD_MIX_W ** -0.5)
    router_g_w = nrm((DEPTH, D, N_GROUPS), D ** -0.5)
    router_g_b = nrm((DEPTH, N_GROUPS), 0.01)
    router_e_w = nrm((DEPTH, D, N_EXPERTS), D ** -0.5)
    router_e_b = nrm((DEPTH, N_EXPERTS), 0.01)
    w1 = nrm((DEPTH, N_EXPERTS, D, D_EXPERT), D ** -0.5)
    w3 = nrm((DEPTH, N_EXPERTS, D, D_EXPERT), D ** -0.5)
    w2 = nrm((DEPTH, N_EXPERTS, D_EXPERT, D), D_EXPERT ** -0.5)
    final_norm_g = 1.0 + nrm((D,), 0.05)
    return {'x': x, 'c': c, 'ctx': ctx, 'c_ctx': c_ctx, 'w_mod': w_mod, 'b_mod': b_mod,
            'norm1_g': norm1_g, 'norm2_g': norm2_g, 'w_in_even': w_in_even, 'ml_gate_b': ml_gate_b,
            'ml_norm_g': ml_norm_g, 'ret_log_decay': ret_log_decay, 'ret_norm_g': ret_norm_g,
            'w_out_even': w_out_even, 'w_in_odd': w_in_odd, 'da_lambda': da_lambda, 'da_norm_g': da_norm_g,
            'w_out_odd': w_out_odd, 'router_g_w': router_g_w, 'router_g_b': router_g_b,
            'router_e_w': router_e_w, 'router_e_b': router_e_b, 'w1': w1, 'w3': w3, 'w2': w2,
            'final_norm_g': final_norm_g}


def reference(x, c, ctx, c_ctx, w_mod, b_mod, norm1_g, norm2_g, w_in_even, ml_gate_b, ml_norm_g,
              ret_log_decay, ret_norm_g, w_out_even, w_in_odd, da_lambda, da_norm_g, w_out_odd,
              router_g_w, router_g_b, router_e_w, router_e_b, w1, w3, w2, final_norm_g):
    B, S, D = x.shape
    L = ctx.shape[1]
    t = jnp.arange(S)
    rows, cols = t // GRID_W, t % GRID_W
    h_lat, h_ctx = x, ctx
    for l in range(DEPTH):
        last = l == DEPTH - 1
        j = l // 2
        m_lat = modulation(c, w_mod[l], b_mod[l])[:, :, None, :]
        m_ctx = modulation(c_ctx, w_mod[l], b_mod[l])
        u_lat = adaln(h_lat, norm1_g[l], m_lat[:, 0], m_lat[:, 1])
        u_ctx = adaln(h_ctx, norm1_g[l], m_ctx[0], m_ctx[1])
        if l % 2 == 0:
            y_ctx, y_lat = mlstm_retention_mixer(u_ctx, u_lat, w_in_even[j], ml_gate_b[j], ml_norm_g[j],
                                                 ret_log_decay[j], ret_norm_g[j], w_out_even[j])
        else:
            lam_init = 0.8 - 0.6 * math.exp(-0.3 * l)
            y_ctx, y_lat = diff_attention_mixer(u_ctx, u_lat, rows, cols, w_in_odd[j], da_lambda[j],
                                                da_norm_g[j], w_out_odd[j], lam_init, not last)
        h_lat = h_lat + m_lat[:, 2] * y_lat
        v_lat = adaln(h_lat, norm2_g[l], m_lat[:, 3], m_lat[:, 4])
        moe_w = (router_g_w[l], router_g_b[l], router_e_w[l], router_e_b[l], w1[l], w3[l], w2[l])
        if last:
            h_lat = h_lat + m_lat[:, 5] * hmoe(v_lat.reshape(-1, D), *moe_w).reshape(B, S, D)
        else:
            h_ctx = h_ctx + m_ctx[2] * y_ctx
            v_ctx = adaln(h_ctx, norm2_g[l], m_ctx[3], m_ctx[4])
            y = hmoe(jnp.concatenate([v_ctx.reshape(-1, D), v_lat.reshape(-1, D)], 0), *moe_w)
            n_ctx = B * L
            h_ctx = h_ctx + m_ctx[5] * y[:n_ctx].reshape(B, L, D)
            h_lat = h_lat + m_lat[:, 5] * y[n_ctx:].reshape(B, S, D)
    return rms_norm(h_lat, final_norm_g)
```

```python
import functools
import math

import numpy as np
import jax
import jax.numpy as jnp
from jax import lax
from jax.experimental import pallas as pl
from jax.experimental.pallas import tpu as pltpu

F32 = jnp.float32
BF16 = jnp.bfloat16

D_MODEL = 1024
GRID_W = 64
CHUNK = 128
EPS = 1e-6
ROPE_BASE = 10000.0
ML_HEADS = 4
RET_HEADS = 4
HEAD_DIM = 128
DA_HEADS = 8
DA_DHEAD = 64
N_GROUPS = 4
EXPERTS_PER_GROUP = 8
N_EXPERTS = N_GROUPS * EXPERTS_PER_GROUP
TOP_K = 2
D_EXPERT = D_MODEL // 2
MOE_BLOCK = 256
ROUTER_PAD = 128
GATE_PAD = 128
EVEN_MAIN = 4096
EVEN_SCAN_COLS = 3072
NEG_BIG = -1e30

V7X_VMEM_LIMIT = 56 * 1024 * 1024


def _cparams(sem):
    return pltpu.CompilerParams(dimension_semantics=sem, vmem_limit_bytes=V7X_VMEM_LIMIT)


def _row_tile(n_lat_per_batch, n_ctx_total):
    for t in (512, 256, 128):
        if n_lat_per_batch % t == 0 and n_ctx_total % t == 0:
            return t
    raise ValueError("row counts must be multiples of 128")


def _split3(a):
    hi = a.astype(BF16)
    r1 = a - hi.astype(F32)
    mid = r1.astype(BF16)
    lo = (r1 - mid.astype(F32)).astype(BF16)
    return hi, mid, lo


def _dot_hi(a, b):
    ah, al, _ = _split3(a)
    bh, bl, _ = _split3(b)
    d = functools.partial(jnp.dot, preferred_element_type=F32)
    return d(ah, bh) + (d(ah, bl) + d(al, bh))


def _sigmoid(x):
    return 1.0 / (1.0 + jnp.exp(-x))


def _rms_rows(x):
    return x * lax.rsqrt(jnp.mean(x * x, axis=-1, keepdims=True) + EPS)


def _mod_kernel(c_ref, w_ref, b_ref, o_ref):
    c = c_ref[...]
    o_ref[...] = _dot_hi(c * _sigmoid(c), w_ref[...]) + b_ref[...]


def _modulation(cond8, w_mod, b_mod):
    depth = w_mod.shape[0]
    ncol = w_mod.shape[2]
    tn = 1024
    return pl.pallas_call(
        _mod_kernel,
        grid=(depth, ncol // tn),
        in_specs=[pl.BlockSpec((8, D_MODEL), lambda l, j: (0, 0)),
                  pl.BlockSpec((None, D_MODEL, tn), lambda l, j: (l, 0, j)),
                  pl.BlockSpec((None, 1, tn), lambda l, j: (l, 0, j))],
        out_specs=pl.BlockSpec((None, 8, tn), lambda l, j: (l, 0, j)),
        out_shape=jax.ShapeDtypeStruct((depth, 8, ncol), F32),
        compiler_params=_cparams(("arbitrary", "arbitrary")),
        name="modulation",
    )(cond8, w_mod, b_mod.reshape(depth, 1, ncol))


def _adaln_bf16(x, g, shift, scale):
    return ((_rms_rows(x) * g) * (1.0 + scale) + shift).astype(BF16)


def _proj_even_kernel(h_ref, mod_ref, g_ref, w_ref, wg_ref, cs_ref, sn_ref, proj_ref, gates_ref):
    u = _adaln_bf16(h_ref[...], g_ref[...], mod_ref[0:1, :], mod_ref[1:2, :])
    cs = cs_ref[...]
    sn = sn_ref[...]
    kscale = HEAD_DIM ** -0.5
    for gi in range(EVEN_MAIN // 512):
        acc = jnp.dot(u, w_ref[:, gi * 512:(gi + 1) * 512], preferred_element_type=F32)
        if gi in (3, 4):
            parts = []
            for hh in range(RET_HEADS):
                blk = acc[:, hh * HEAD_DIM:(hh + 1) * HEAD_DIM]
                parts.append(blk * cs + pltpu.roll(blk, HEAD_DIM // 2, 1) * sn)
            acc = jnp.concatenate(parts, axis=1)
        if gi in (1, 4):
            acc = acc * kscale
        proj_ref[:, gi * 512:(gi + 1) * 512] = acc.astype(BF16)
    gates_ref[...] = jnp.dot(u, wg_ref[...], preferred_element_type=F32)


def _combine(h, y0, y1, gk, gate):
    moe = gk[:, 0:1] * y0.astype(F32) + gk[:, 1:2] * y1.astype(F32)
    return h + gate * moe


def _proj_odd_kernel(h_ref, y0_ref, y1_ref, gk_ref, modp_ref, mod_ref, g_ref, w_ref, wvt_ref,
                     c_ref, s1_ref, s2_ref, hout_ref, qk_ref, vt_ref):
    h = _combine(h_ref[...], y0_ref[...], y1_ref[...], gk_ref[...], modp_ref[5:6, :])
    hout_ref[...] = h
    u = _adaln_bf16(h, g_ref[...], mod_ref[0:1, :], mod_ref[1:2, :])
    c = c_ref[...]
    s1 = s1_ref[...]
    s2 = s2_ref[...]
    qscale = DA_DHEAD ** -0.5
    nq = DA_HEADS * 2 * DA_DHEAD
    for gi in range(2 * nq // 512):
        acc = jnp.dot(u, w_ref[:, gi * 512:(gi + 1) * 512], preferred_element_type=F32)
        parts = []
        for hh in range(512 // 128):
            blk = acc[:, hh * 128:(hh + 1) * 128]
            parts.append(blk * c + pltpu.roll(blk, 128 - 16, 1) * s1 + pltpu.roll(blk, 16, 1) * s2)
        acc = jnp.concatenate(parts, axis=1)
        if gi * 512 < nq:
            acc = acc * qscale
        qk_ref[:, gi * 512:(gi + 1) * 512] = acc.astype(BF16)
    vt = lax.dot_general(wvt_ref[...], u, (((1,), (1,)), ((), ())), preferred_element_type=F32)
    vt_ref[...] = vt.astype(BF16)


def _tile_maps(B, S, L, tr):
    lat_tiles = B * S // tr
    per_b = S // tr

    def mod_idx(i):
        return jnp.where(i < lat_tiles, i // per_b, B)

    def rot_idx(i):
        return jnp.where(i < lat_tiles, i % per_b, per_b + (i - lat_tiles))

    return mod_idx, rot_idx


def _proj_even(h, mod, g, w_main, w_gates, cs, sn, dims):
    B, S, L, tr = dims
    n = h.shape[0]
    mod_idx, rot_idx = _tile_maps(B, S, L, tr)
    const = lambda i: (0, 0)
    return pl.pallas_call(
        _proj_even_kernel,
        grid=(n // tr,),
        in_specs=[pl.BlockSpec((tr, D_MODEL), lambda i: (i, 0)),
                  pl.BlockSpec((None, 8, D_MODEL), lambda i: (mod_idx(i), 0, 0)),
                  pl.BlockSpec((1, D_MODEL), const),
                  pl.BlockSpec((D_MODEL, EVEN_MAIN), const),
                  pl.BlockSpec((D_MODEL, GATE_PAD), const),
                  pl.BlockSpec((tr, 128), lambda i: (rot_idx(i), 0)),
                  pl.BlockSpec((tr, 128), lambda i: (rot_idx(i), 0))],
        out_specs=[pl.BlockSpec((tr, EVEN_MAIN), lambda i: (i, 0)),
                   pl.BlockSpec((tr, GATE_PAD), lambda i: (i, 0))],
        out_shape=[jax.ShapeDtypeStruct((n, EVEN_MAIN), BF16),
                   jax.ShapeDtypeStruct((n, GATE_PAD), F32)],
        compiler_params=_cparams(("arbitrary",)),
        name="proj_even",
    )(h, mod, g, w_main, w_gates, cs, sn)


def _proj_odd(h, y0, y1, gk, modp, mod, g, w_qk, w_vt, c, s1, s2, dims):
    B, S, L, tr = dims
    n = h.shape[0]
    mod_idx, rot_idx = _tile_maps(B, S, L, tr)
    const = lambda i: (0, 0)
    row = lambda i: (i, 0)
    nqk = w_qk.shape[1]
    return pl.pallas_call(
        _proj_odd_kernel,
        grid=(n // tr,),
        in_specs=[pl.BlockSpec((tr, D_MODEL), row),
                  pl.BlockSpec((tr, D_MODEL), row),
                  pl.BlockSpec((tr, D_MODEL), row),
                  pl.BlockSpec((tr, TOP_K), row),
                  pl.BlockSpec((None, 8, D_MODEL), lambda i: (mod_idx(i), 0, 0)),
                  pl.BlockSpec((None, 8, D_MODEL), lambda i: (mod_idx(i), 0, 0)),
                  pl.BlockSpec((1, D_MODEL), const),
                  pl.BlockSpec((D_MODEL, nqk), const),
                  pl.BlockSpec((D_MODEL, D_MODEL), const),
                  pl.BlockSpec((tr, 128), lambda i: (rot_idx(i), 0)),
                  pl.BlockSpec((tr, 128), lambda i: (rot_idx(i), 0)),
                  pl.BlockSpec((tr, 128), lambda i: (rot_idx(i), 0))],
        out_specs=[pl.BlockSpec((tr, D_MODEL), row),
                   pl.BlockSpec((tr, nqk), row),
                   pl.BlockSpec((D_MODEL, tr), lambda i: (0, i))],
        out_shape=[jax.ShapeDtypeStruct((n, D_MODEL), F32),
                   jax.ShapeDtypeStruct((n, nqk), BF16),
                   jax.ShapeDtypeStruct((D_MODEL, n), BF16)],
        compiler_params=_cparams(("arbitrary",)),
        name="proj_odd",
    )(h, y0, y1, gk, modp, mod, g, w_qk, w_vt, c, s1, s2)


_NT = (((1,), (1,)), ((), ()))
_TN = (((0,), (0,)), ((), ()))


def _scan_kernel(ld_ref, xf_ref, xb_ref, gf_ref, gb_ref, bias_ref, hf_ref, hb_ref,
                 ct_ref, n_ref, m_ref, r_ref, intra_ref, qd_ref, kd_ref):
    t = pl.program_id(1)
    nh = ML_HEADS
    row_i = lax.broadcasted_iota(jnp.int32, (CHUNK, CHUNK), 0)
    col_i = lax.broadcasted_iota(jnp.int32, (CHUNK, CHUNK), 1)
    row_f = row_i.astype(F32)
    col_f = col_i.astype(F32)

    @pl.when(t == 0)
    def _init():
        ct_ref[...] = jnp.zeros_like(ct_ref)
        n_ref[...] = jnp.zeros_like(n_ref)
        m_ref[...] = jnp.zeros_like(m_ref)
        r_ref[...] = jnp.zeros_like(r_ref)
        for d in range(2):
            for hh in range(nh):
                hd = d * nh + hh
                lg = ld_ref[hd]
                if d == 0:
                    intra = jnp.where(col_i <= row_i, jnp.exp(lg * (row_f - col_f)), 0.0)
                    qd = jnp.exp(lg * (row_f + 1.0))
                    kd = jnp.exp(lg * (CHUNK - 1.0 - row_f))
                else:
                    intra = jnp.where(col_i >= row_i, jnp.exp(lg * (col_f - row_f)), 0.0)
                    qd = jnp.exp(lg * (CHUNK - row_f))
                    kd = jnp.exp(lg * row_f)
                intra_ref[hd] = intra
                qd_ref[hd] = qd
                kd_ref[hd] = kd

    bias = bias_ref[...]
    dot = functools.partial(jnp.dot, preferred_element_type=F32)
    dg = functools.partial(lax.dot_general, preferred_element_type=F32)
    for d, (x_ref, g_ref, o_ref) in enumerate(((xf_ref, gf_ref, hf_ref), (xb_ref, gb_ref, hb_ref))):
        seen = (col_i <= row_i) if d == 0 else (col_i >= row_i)
        G = g_ref[...] + bias
        LF = jnp.minimum(G, 0.0) - jnp.log1p(jnp.exp(-jnp.abs(G)))
        tri = jnp.where(seen, 1.0, 0.0).astype(BF16)
        l_hi, l_mid, l_lo = _split3(LF)
        Bc = dot(tri, l_hi) + (dot(tri, l_mid) + dot(tri, l_lo))
        GT = G.T
        BT = Bc.T
        last = CHUNK - 1 if d == 0 else 0
        for hh in range(nh):
            hd = d * nh + hh
            c0 = hh * HEAD_DIM
            q = x_ref[:, c0:c0 + HEAD_DIM]
            k = x_ref[:, 512 + c0:512 + c0 + HEAD_DIM]
            v = x_ref[:, 1024 + c0:1024 + c0 + HEAD_DIM]
            ic_col = G[:, hd:hd + 1]
            ic_row = GT[hd:hd + 1, :]
            b_col = Bc[:, 8 + hd:9 + hd]
            b_row = BT[8 + hd:9 + hd, :]
            m0 = m_ref[hd][:, 0:1]
            dlog = jnp.where(seen, b_col - b_row + ic_row, NEG_BIG)
            inter = b_col + m0
            m_t = jnp.maximum(inter, jnp.max(dlog, axis=1, keepdims=True))
            s = dg(q, k, _NT) * jnp.exp(dlog - m_t)
            w_inter = jnp.exp(inter - m_t)
            ct = ct_ref[hd]
            n_row = n_ref[hd]
            numer = dot(s.astype(BF16), v) + w_inter * dot(q, ct.astype(BF16))
            denom = (jnp.sum(s, axis=1, keepdims=True)
                     + w_inter * jnp.sum(q.astype(F32) * n_row, axis=1, keepdims=True))
            hval = numer / jnp.maximum(jnp.abs(denom), jnp.exp(-m_t))
            o_ref[:, c0:c0 + HEAD_DIM] = hval.astype(BF16)
            b_last = b_col[last:last + 1, :]
            w_log_row = b_last - b_row + ic_row
            m_new = jnp.maximum(b_last + m0, jnp.max(w_log_row, axis=1, keepdims=True))
            decay = jnp.exp(b_last + m0 - m_new)
            w_col = jnp.exp(b_last - b_col + ic_col - m_new)
            w_row = jnp.exp(w_log_row - m_new)
            vw = (v.astype(F32) * w_col).astype(BF16)
            ct_ref[hd] = decay * ct + dg(k, vw, _TN)
            w8 = jnp.broadcast_to(w_row, (8, CHUNK)).astype(BF16)
            n_ref[hd] = decay * n_row + dot(w8, k)[0:1, :]
            m_ref[hd] = jnp.broadcast_to(m_new, (1, HEAD_DIM))
            rq = x_ref[:, 1536 + c0:1536 + c0 + HEAD_DIM]
            rk = x_ref[:, 2048 + c0:2048 + c0 + HEAD_DIM]
            rv = x_ref[:, 2560 + c0:2560 + c0 + HEAD_DIM]
            rs = dg(rq, rk, _NT) * intra_ref[hd]
            R = r_ref[hd]
            o = dot(rs.astype(BF16), rv) + qd_ref[hd] * dot(rq, R.astype(BF16))
            o_ref[:, 512 + c0:512 + c0 + HEAD_DIM] = o.astype(BF16)
            kdk = (rk.astype(F32) * kd_ref[hd]).astype(BF16)
            cdec = jnp.exp(ld_ref[hd] * jnp.full((1, HEAD_DIM), float(CHUNK), F32))
            r_ref[hd] = cdec * R + dg(kdk, rv, _TN)


def _scan(ret_ld8, proj, gates, bias, B, S, L):
    n = proj.shape[0]
    nlb, ncb = S // CHUNK, L // CHUNK
    nc = nlb + ncb

    def fwd(b, t):
        return jnp.where(t < ncb, B * nlb + b * ncb + t, b * nlb + t - ncb)

    def bwd(b, t):
        return jnp.where(t < ncb, B * nlb + b * ncb + (ncb - 1 - t), b * nlb + (nlb - 1 - (t - ncb)))

    state = pltpu.VMEM((2 * ML_HEADS, HEAD_DIM, HEAD_DIM), F32)
    vec = pltpu.VMEM((2 * ML_HEADS, 1, HEAD_DIM), F32)
    grid_spec = pltpu.PrefetchScalarGridSpec(
        num_scalar_prefetch=1,
        grid=(B, nc),
        in_specs=[pl.BlockSpec((CHUNK, EVEN_SCAN_COLS), lambda b, t, ld: (fwd(b, t), 0)),
                  pl.BlockSpec((CHUNK, EVEN_SCAN_COLS), lambda b, t, ld: (bwd(b, t), 0)),
                  pl.BlockSpec((CHUNK, GATE_PAD), lambda b, t, ld: (fwd(b, t), 0)),
                  pl.BlockSpec((CHUNK, GATE_PAD), lambda b, t, ld: (bwd(b, t), 0)),
                  pl.BlockSpec((1, GATE_PAD), lambda b, t, ld: (0, 0))],
        out_specs=[pl.BlockSpec((CHUNK, D_MODEL), lambda b, t, ld: (fwd(b, t), 0)),
                   pl.BlockSpec((CHUNK, D_MODEL), lambda b, t, ld: (bwd(b, t), 0))],
        scratch_shapes=[state, vec, vec, state, state, state, state],
    )
    return pl.pallas_call(
        _scan_kernel,
        grid_spec=grid_spec,
        out_shape=[jax.ShapeDtypeStruct((n, D_MODEL), BF16),
                   jax.ShapeDtypeStruct((n, D_MODEL), BF16)],
        compiler_params=_cparams(("arbitrary", "arbitrary")),
        name="scan",
    )(ret_ld8, proj, proj, gates, gates, bias)


def _post_tail(merged, h_ref, mod_ref, g2_ref, wout_ref, wr_ref, br_ref, hout_ref, v_ref, lg_ref):
    y = jnp.dot(merged, wout_ref[...], preferred_element_type=F32)
    h = h_ref[...] + mod_ref[2:3, :] * y
    hout_ref[...] = h
    v = (_rms_rows(h) * g2_ref[...]) * (1.0 + mod_ref[4:5, :]) + mod_ref[3:4, :]
    v_ref[...] = v.astype(BF16)
    lg_ref[...] = _dot_hi(v, wr_ref[...]) + br_ref[...]


def _head_ln(x, g):
    mu = jnp.mean(x, axis=-1, keepdims=True)
    xc = x - mu
    var = jnp.mean(xc * xc, axis=-1, keepdims=True)
    return xc * lax.rsqrt(var + EPS) * g


def _post_even_kernel(hf_ref, hb_ref, og_ref, mlg_ref, retg_ref, h_ref, mod_ref, g2_ref, wout_ref,
                      wr_ref, br_ref, hout_ref, v_ref, lg_ref):
    parts = []
    for hh in range(ML_HEADS):
        sl = slice(hh * HEAD_DIM, (hh + 1) * HEAD_DIM)
        ml = hf_ref[:, sl].astype(F32) + hb_ref[:, sl].astype(F32)
        parts.append(_head_ln(_sigmoid(og_ref[:, sl].astype(F32)) * ml, mlg_ref[:, sl]))
    for hh in range(RET_HEADS):
        sl = slice(512 + hh * HEAD_DIM, 512 + (hh + 1) * HEAD_DIM)
        ret = hf_ref[:, sl].astype(F32) + hb_ref[:, sl].astype(F32)
        rg = og_ref[:, sl].astype(F32)
        parts.append((rg * _sigmoid(rg)) * _head_ln(ret, retg_ref[:, hh * HEAD_DIM:(hh + 1) * HEAD_DIM]))
    merged = jnp.concatenate(parts, axis=1).astype(BF16)
    _post_tail(merged, h_ref, mod_ref, g2_ref, wout_ref, wr_ref, br_ref, hout_ref, v_ref, lg_ref)


def _post_odd_kernel(att_ref, h_ref, mod_ref, g2_ref, wout_ref, wr_ref, br_ref, hout_ref, v_ref, lg_ref):
    _post_tail(att_ref[...], h_ref, mod_ref, g2_ref, wout_ref, wr_ref, br_ref, hout_ref, v_ref, lg_ref)


def _post_specs(n, tr, mod_idx):
    const = lambda i: (0, 0)
    row = lambda i: (i, 0)
    tail_in = [pl.BlockSpec((tr, D_MODEL), row),
               pl.BlockSpec((None, 8, D_MODEL), lambda i: (mod_idx(i), 0, 0)),
               pl.BlockSpec((1, D_MODEL), const),
               pl.BlockSpec((D_MODEL, D_MODEL), const),
               pl.BlockSpec((D_MODEL, ROUTER_PAD), const),
               pl.BlockSpec((1, ROUTER_PAD), const)]
    out_specs = [pl.BlockSpec((tr, D_MODEL), row),
                 pl.BlockSpec((tr, D_MODEL), row),
                 pl.BlockSpec((tr, ROUTER_PAD), row)]
    out_shape = [jax.ShapeDtypeStruct((n, D_MODEL), F32),
                 jax.ShapeDtypeStruct((n, D_MODEL), BF16),
                 jax.ShapeDtypeStruct((n, ROUTER_PAD), F32)]
    return tail_in, out_specs, out_shape


def _post_even(hf, hb, proj, mlg, retg, h, mod, g2, wout, wr, br, dims):
    B, S, L, tr = dims
    n = h.shape[0]
    mod_idx, _ = _tile_maps(B, S, L, tr)
    tail_in, out_specs, out_shape = _post_specs(n, tr, mod_idx)
    row = lambda i: (i, 0)
    const = lambda i: (0, 0)
    return pl.pallas_call(
        _post_even_kernel,
        grid=(n // tr,),
        in_specs=[pl.BlockSpec((tr, D_MODEL), row),
                  pl.BlockSpec((tr, D_MODEL), row),
                  pl.BlockSpec((tr, D_MODEL), lambda i: (i, EVEN_SCAN_COLS // D_MODEL)),
                  pl.BlockSpec((1, 512), const),
                  pl.BlockSpec((1, 512), const)] + tail_in,
        out_specs=out_specs,
        out_shape=out_shape,
        compiler_params=_cparams(("arbitrary",)),
        name="post_even",
    )(hf, hb, proj, mlg, retg, h, mod, g2, wout, wr, br)


def _post_odd(att, h, mod, g2, wout, wr, br, dims):
    B, S, L, tr = dims
    n = att.shape[0]
    mod_idx, _ = _tile_maps(B, S, L, tr)
    tail_in, out_specs, out_shape = _post_specs(n, tr, mod_idx)
    return pl.pallas_call(
        _post_odd_kernel,
        grid=(n // tr,),
        in_specs=[pl.BlockSpec((tr, D_MODEL), lambda i: (i, 0))] + tail_in,
        out_specs=out_specs,
        out_shape=out_shape,
        compiler_params=_cparams(("arbitrary",)),
        name="post_odd",
    )(att, h, mod, g2, wout, wr, br)


def _attn_kernel(q_ref, kl_ref, kc_ref, vl_ref, vc_ref, lam_ref, g_ref, o_ref, acc_ref, m_ref, l_ref,
                 *, tk, lam_init):
    tq = q_ref.shape[0]
    s_len = kl_ref.shape[0]
    q = q_ref[...]
    lane = lax.broadcasted_iota(jnp.int32, q.shape, 1)
    zero = jnp.zeros_like(q)
    qm = (jnp.where(lane < DA_DHEAD, q, zero), jnp.where(lane >= DA_DHEAD, q, zero))

    acc_ref[...] = jnp.zeros_like(acc_ref)
    l_ref[...] = jnp.zeros_like(l_ref)
    m_ref[...] = jnp.full(m_ref.shape, NEG_BIG, F32)

    def update(kc, vtc):
        for mi in range(2):
            st = lax.dot_general(kc, qm[mi], _NT, preferred_element_type=F32)
            m_old = m_ref[mi]
            m_new = jnp.maximum(m_old, jnp.max(st, axis=0, keepdims=True))
            alpha = jnp.exp(m_old - m_new)
            p = jnp.exp(st - m_new)
            l_ref[mi] = alpha * l_ref[mi] + jnp.sum(p, axis=0, keepdims=True)
            acc_ref[mi] = alpha * acc_ref[mi] + jnp.dot(vtc, p.astype(BF16), preferred_element_type=F32)
            m_ref[mi] = m_new

    def body(c, carry):
        off = pl.multiple_of(c * tk, tk)
        update(kl_ref[pl.ds(off, tk), :], vl_ref[:, pl.ds(off, tk)])
        return carry

    lax.fori_loop(0, s_len // tk, body, 0)
    update(kc_ref[...], vc_ref[...])

    lp = lam_ref[...]
    lam = (jnp.exp(jnp.sum(lp[0:1, :] * lp[1:2, :], axis=1, keepdims=True))
           - jnp.exp(jnp.sum(lp[2:3, :] * lp[3:4, :], axis=1, keepdims=True)) + lam_init)
    ot = acc_ref[0] / l_ref[0] - lam * (acc_ref[1] / l_ref[1])
    ot = ot * lax.rsqrt(jnp.mean(ot * ot, axis=0, keepdims=True) + EPS)
    o = ot.T * g_ref[...] * (1.0 - lam_init)
    o_ref[...] = o.astype(BF16)


def _attention(qk, vt, lam_p, norm_g, lam_init, B, S, L):
    tq = 512 if S % 512 == 0 else 256
    tk = 512 if S % 512 == 0 else 256
    nq = S // tq
    nh = DA_HEADS
    kcol = nh
    ctx0 = B * S // L
    kern = functools.partial(_attn_kernel, tk=tk, lam_init=lam_init)
    return pl.pallas_call(
        kern,
        grid=(B, nh, nq),
        in_specs=[pl.BlockSpec((tq, 128), lambda b, h, i: (b * nq + i, h)),
                  pl.BlockSpec((S, 128), lambda b, h, i: (b, kcol + h)),
                  pl.BlockSpec((L, 128), lambda b, h, i: (ctx0 + b, kcol + h)),
                  pl.BlockSpec((128, S), lambda b, h, i: (h, b)),
                  pl.BlockSpec((128, L), lambda b, h, i: (h, ctx0 + b)),
                  pl.BlockSpec((4, DA_DHEAD), lambda b, h, i: (0, 0)),
                  pl.BlockSpec((None, 1, 128), lambda b, h, i: (h, 0, 0))],
        out_specs=pl.BlockSpec((tq, 128), lambda b, h, i: (b * nq + i, h)),
        out_shape=jax.ShapeDtypeStruct((B * S, D_MODEL), BF16),
        scratch_shapes=[pltpu.VMEM((2, 128, tq), F32),
                        pltpu.VMEM((2, 1, tq), F32),
                        pltpu.VMEM((2, 1, tq), F32)],
        compiler_params=_cparams(("arbitrary", "arbitrary", "arbitrary")),
        name="diff_attention",
    )(qk, qk, qk, vt, vt, lam_p, norm_g.reshape(nh, 1, 128))


def _expert_kernel(be_ref, nu_ref, x_ref, w1_ref, w3_ref, w2_ref, y_ref):
    i = pl.program_id(0)

    @pl.when(i < nu_ref[0])
    def _compute():
        x = x_ref[...]
        a = jnp.dot(x, w1_ref[...], preferred_element_type=F32)
        b = jnp.dot(x, w3_ref[...], preferred_element_type=F32)
        hm = ((a * _sigmoid(a)) * b).astype(BF16)
        y_ref[...] = jnp.dot(hm, w2_ref[...], preferred_element_type=F32).astype(BF16)

    @pl.when(i >= nu_ref[0])
    def _skip():
        y_ref[...] = jnp.zeros_like(y_ref)


def _experts(blk_expert, n_used, xb, w1, w3, w2):
    p = xb.shape[0]
    nblk = p // MOE_BLOCK
    grid_spec = pltpu.PrefetchScalarGridSpec(
        num_scalar_prefetch=2,
        grid=(nblk,),
        in_specs=[pl.BlockSpec((MOE_BLOCK, D_MODEL), lambda i, be, nu: (i, 0)),
                  pl.BlockSpec((None, D_MODEL, D_EXPERT), lambda i, be, nu: (be[i], 0, 0)),
                  pl.BlockSpec((None, D_MODEL, D_EXPERT), lambda i, be, nu: (be[i], 0, 0)),
                  pl.BlockSpec((None, D_EXPERT, D_MODEL), lambda i, be, nu: (be[i], 0, 0))],
        out_specs=pl.BlockSpec((MOE_BLOCK, D_MODEL), lambda i, be, nu: (i, 0)),
    )
    return pl.pallas_call(
        _expert_kernel,
        grid_spec=grid_spec,
        out_shape=jax.ShapeDtypeStruct((p, D_MODEL), BF16),
        compiler_params=_cparams(("arbitrary",)),
        name="experts",
    )(blk_expert, n_used, xb, w1, w3, w2)


def _route(logits):
    n = logits.shape[0]
    pg = jax.nn.softmax(logits[:, :N_GROUPS], -1)
    pg_top, grp = lax.top_k(pg, 1)
    le = logits[:, N_GROUPS:N_GROUPS + N_EXPERTS].reshape(n, N_GROUPS, EXPERTS_PER_GROUP)
    le = jnp.take_along_axis(le, grp[:, :, None], 1)[:, 0]
    pe_top, e_in = lax.top_k(jax.nn.softmax(le, -1), TOP_K)
    gate = pg_top * pe_top / jnp.sum(pe_top, -1, keepdims=True)
    expert = (grp * EXPERTS_PER_GROUP + e_in).astype(jnp.int32).reshape(-1)
    a = expert.shape[0]
    nblk = -(-a // MOE_BLOCK) + N_EXPERTS
    onehot = (expert[:, None] == jnp.arange(N_EXPERTS, dtype=jnp.int32)[None, :]).astype(jnp.int32)
    rank = jnp.sum(jnp.cumsum(onehot, axis=0) * onehot, axis=1) - 1
    counts = jnp.sum(onehot, axis=0)
    padded = (counts + MOE_BLOCK - 1) // MOE_BLOCK * MOE_BLOCK
    pend = jnp.cumsum(padded)
    dest = (pend - padded)[expert] + rank
    tok = jnp.arange(a, dtype=jnp.int32) // TOP_K
    buf_tok = jnp.zeros((nblk * MOE_BLOCK,), jnp.int32).at[dest].set(tok)
    n_used = (pend[-1] // MOE_BLOCK).astype(jnp.int32)
    blk_expert = jnp.searchsorted(pend, jnp.arange(nblk, dtype=jnp.int32) * MOE_BLOCK, side='right')
    last_e = jnp.clip(jnp.searchsorted(pend, (n_used - 1) * MOE_BLOCK, side='right'), 0, N_EXPERTS - 1)
    blk_expert = jnp.where(jnp.arange(nblk) < n_used, jnp.clip(blk_expert, 0, N_EXPERTS - 1), last_e)
    return (buf_tok, blk_expert.astype(jnp.int32), n_used.reshape(1), dest.reshape(n, TOP_K),
            gate.astype(F32))


def _moe(v, logits, w1, w3, w2):
    buf_tok, blk_expert, n_used, pos, gate = _route(logits)
    xb = jnp.take(v, buf_tok, axis=0)
    yb = _experts(blk_expert, n_used, xb, w1, w3, w2)
    y0 = jnp.take(yb, pos[:, 0], axis=0)
    y1 = jnp.take(yb, pos[:, 1], axis=0)
    return y0, y1, gate


def _final_kernel(h_ref, y0_ref, y1_ref, gk_ref, modp_ref, g_ref, o_ref):
    h = _combine(h_ref[...], y0_ref[...], y1_ref[...], gk_ref[...], modp_ref[5:6, :])
    o_ref[...] = _rms_rows(h) * g_ref[...]


def _final(h, y0, y1, gk, modp, g, dims):
    B, S, L, tr = dims
    n = h.shape[0]
    mod_idx, _ = _tile_maps(B, S, L, tr)
    row = lambda i: (i, 0)
    return pl.pallas_call(
        _final_kernel,
        grid=(n // tr,),
        in_specs=[pl.BlockSpec((tr, D_MODEL), row),
                  pl.BlockSpec((tr, D_MODEL), row),
                  pl.BlockSpec((tr, D_MODEL), row),
                  pl.BlockSpec((tr, TOP_K), row),
                  pl.BlockSpec((None, 8, D_MODEL), lambda i: (mod_idx(i), 0, 0)),
                  pl.BlockSpec((1, D_MODEL), lambda i: (0, 0))],
        out_specs=pl.BlockSpec((tr, D_MODEL), row),
        out_shape=jax.ShapeDtypeStruct((n, D_MODEL), F32),
        compiler_params=_cparams(("arbitrary",)),
        name="final_norm",
    )(h, y0, y1, gk, modp, g)


def _rotary_tables_even(S, L, B):
    nf = HEAD_DIM // 2
    inv = ROPE_BASE ** (-jnp.arange(nf, dtype=F32) / nf)
    pos = jnp.concatenate([L + jnp.arange(S), jnp.tile(jnp.arange(L), B)]).astype(F32)
    ang = pos[:, None] * inv[None, :]
    cos, sin = jnp.cos(ang), jnp.sin(ang)
    return jnp.concatenate([cos, cos], 1), jnp.concatenate([-sin, sin], 1)


def _rotary_tables_odd(S, L, B):
    nf = DA_DHEAD // 4
    inv = ROPE_BASE ** (-jnp.arange(nf, dtype=F32) / nf)
    t = jnp.arange(S)
    rows, cols = (t // GRID_W).astype(F32), (t % GRID_W).astype(F32)
    j = np.arange(128)
    f_idx = j % nf
    use_col = (j % DA_DHEAD) >= DA_DHEAD // 2
    first = (j % (2 * nf)) < nf
    ang = jnp.where(use_col[None, :], cols[:, None], rows[:, None]) * inv[f_idx][None, :]
    cos, sin = jnp.cos(ang), jnp.sin(ang)
    c_lat = cos
    s1_lat = jnp.where(first[None, :], -sin, 0.0)
    s2_lat = jnp.where(first[None, :], 0.0, sin)
    nctx = B * L
    c = jnp.concatenate([c_lat, jnp.ones((nctx, 128), F32)])
    s1 = jnp.concatenate([s1_lat, jnp.zeros((nctx, 128), F32)])
    s2 = jnp.concatenate([s2_lat, jnp.zeros((nctx, 128), F32)])
    return c, s1, s2


def kernel(x, c, ctx, c_ctx, w_mod, b_mod, norm1_g, norm2_g, w_in_even, ml_gate_b, ml_norm_g, ret_log_decay, ret_norm_g, w_out_even, w_in_odd, da_lambda, da_norm_g, w_out_odd, router_g_w, router_g_b, router_e_w, router_e_b, w1, w3, w2, final_norm_g):
    B, S, D = x.shape
    L = ctx.shape[1]
    depth = w_mod.shape[0]
    assert D == D_MODEL and depth == 2 and S % CHUNK == 0 and L % CHUNK == 0 and S % L == 0
    n_lat, n_ctx = B * S, B * L
    tr = _row_tile(S, n_ctx)
    dims = (B, S, L, tr)

    h = jnp.concatenate([x.reshape(n_lat, D), ctx.reshape(n_ctx, D)], 0)

    cond8 = jnp.zeros((8, D), F32).at[:B].set(c).at[B].set(c_ctx)
    mod = jnp.pad(_modulation(cond8, w_mod, b_mod).reshape(depth, 8, 6, D), ((0, 0), (0, 0), (0, 2), (0, 0)))

    def router_w(l):
        wr = jnp.zeros((D, ROUTER_PAD), F32)
        wr = wr.at[:, :N_GROUPS].set(router_g_w[l]).at[:, N_GROUPS:N_GROUPS + N_EXPERTS].set(router_e_w[l])
        br = jnp.zeros((1, ROUTER_PAD), F32)
        br = br.at[0, :N_GROUPS].set(router_g_b[l]).at[0, N_GROUPS:N_GROUPS + N_EXPERTS].set(router_e_b[l])
        return wr, br

    wi = w_in_even[0]
    mq, mk, mv, mo, mi, mf, rq, rk, rv, rg = jnp.split(wi, np.cumsum(
        [512, 512, 512, 512, 8, 8, 512, 512, 512])[:9].tolist(), axis=1)
    w_main = jnp.concatenate([mq, mk, mv, rq, rk, rv, mo, rg], 1).astype(BF16)
    w_gates = jnp.pad(jnp.concatenate([mi, mf], 1), ((0, 0), (0, GATE_PAD - 16))).astype(BF16)
    gate_bias = jnp.pad(jnp.concatenate([ml_gate_b[0][:, 0].reshape(-1), ml_gate_b[0][:, 1].reshape(-1)]),
                        (0, GATE_PAD - 16)).reshape(1, GATE_PAD)
    cs, sn = _rotary_tables_even(S, L, B)
    proj, gates = _proj_even(h, mod[0], norm1_g[0].reshape(1, D), w_main, w_gates, cs, sn, dims)
    hf, hb = _scan(ret_log_decay[0].reshape(-1), proj, gates, gate_bias, B, S, L)
    wr, br = router_w(0)
    h, v, logits = _post_even(hf, hb, proj, ml_norm_g[0].reshape(1, -1), ret_norm_g[0].reshape(1, -1), h,
                              mod[0], norm2_g[0].reshape(1, D), w_out_even[0].astype(BF16), wr, br, dims)
    y0, y1, gk = _moe(v, logits, w1[0].astype(BF16), w3[0].astype(BF16), w2[0].astype(BF16))

    lam_init = 0.8 - 0.6 * math.exp(-0.3 * 1)
    nqk = 4 * DA_HEADS * DA_DHEAD
    w_qk = w_in_odd[0][:, :nqk].astype(BF16)
    w_vt = w_in_odd[0][:, nqk:].T.astype(BF16)
    c2, s1, s2 = _rotary_tables_odd(S, L, B)
    h, qk, vt = _proj_odd(h, y0, y1, gk, mod[0], mod[1], norm1_g[1].reshape(1, D), w_qk, w_vt, c2, s1, s2, dims)
    att = _attention(qk, vt, da_lambda[0], da_norm_g[0], lam_init, B, S, L)
    wr, br = router_w(1)
    h_lat, v, logits = _post_odd(att, h, mod[1], norm2_g[1].reshape(1, D), w_out_odd[0].astype(BF16), wr, br, dims)
    y0, y1, gk = _moe(v, logits, w1[1].astype(BF16), w3[1].astype(BF16), w2[1].astype(BF16))
    out = _final(h_lat, y0, y1, gk, mod[1], final_norm_g.reshape(1, D), dims)
    return out.reshape(B, S, D)
```

```python
import functools
import math

import numpy as np
import jax
import jax.numpy as jnp
from jax import lax
from jax.experimental import pallas as pl
from jax.experimental.pallas import tpu as pltpu

F32 = jnp.float32
BF16 = jnp.bfloat16

D_MODEL = 1024
GRID_W = 64
CHUNK = 128
EPS = 1e-6
ROPE_BASE = 10000.0
ML_HEADS = 4
RET_HEADS = 4
HEAD_DIM = 128
DA_HEADS = 8
DA_DHEAD = 64
N_GROUPS = 4
EXPERTS_PER_GROUP = 8
N_EXPERTS = N_GROUPS * EXPERTS_PER_GROUP
TOP_K = 2
D_EXPERT = D_MODEL // 2
MOE_BLOCK = 256
ROUTER_PAD = 128
GATE_PAD = 128
EVEN_MAIN = 4096
EVEN_SCAN_COLS = 3072
NEG_BIG = -1e30

V7X_VMEM_LIMIT = 56 * 1024 * 1024


def _cparams(sem):
    return pltpu.CompilerParams(dimension_semantics=sem, vmem_limit_bytes=V7X_VMEM_LIMIT)


def _row_tile(n_lat_per_batch, n_ctx_total):
    for t in (512, 256, 128):
        if n_lat_per_batch % t == 0 and n_ctx_total % t == 0:
            return t
    raise ValueError("row counts must be multiples of 128")


def _split3(a):
    hi = a.astype(BF16)
    r1 = a - hi.astype(F32)
    mid = r1.astype(BF16)
    lo = (r1 - mid.astype(F32)).astype(BF16)
    return hi, mid, lo


def _dot_hi(a, b):
    ah, al, _ = _split3(a)
    bh, bl, _ = _split3(b)
    d = functools.partial(jnp.dot, preferred_element_type=F32)
    return d(ah, bh) + (d(ah, bl) + d(al, bh))


def _sigmoid(x):
    return 1.0 / (1.0 + jnp.exp(-x))


def _rms_rows(x):
    return x * lax.rsqrt(jnp.mean(x * x, axis=-1, keepdims=True) + EPS)


def _mod_kernel(c_ref, w_ref, b_ref, o_ref):
    c = c_ref[...]
    o_ref[...] = _dot_hi(c * _sigmoid(c), w_ref[...]) + b_ref[...]


def _modulation(cond8, w_mod, b_mod):
    depth = w_mod.shape[0]
    ncol = w_mod.shape[2]
    tn = 1024
    return pl.pallas_call(
        _mod_kernel,
        grid=(depth, ncol // tn),
        in_specs=[pl.BlockSpec((8, D_MODEL), lambda l, j: (0, 0)),
                  pl.BlockSpec((None, D_MODEL, tn), lambda l, j: (l, 0, j)),
                  pl.BlockSpec((None, 1, tn), lambda l, j: (l, 0, j))],
        out_specs=pl.BlockSpec((None, 8, tn), lambda l, j: (l, 0, j)),
        out_shape=jax.ShapeDtypeStruct((depth, 8, ncol), F32),
        compiler_params=_cparams(("arbitrary", "arbitrary")),
        name="modulation",
    )(cond8, w_mod, b_mod.reshape(depth, 1, ncol))


def _adaln_bf16(x, g, shift, scale):
    return ((_rms_rows(x) * g) * (1.0 + scale) + shift).astype(BF16)


def _proj_even_kernel(h_ref, mod_ref, g_ref, w_ref, wg_ref, cs_ref, sn_ref, proj_ref, gates_ref):
    u = _adaln_bf16(h_ref[...], g_ref[...], mod_ref[0:1, :], mod_ref[1:2, :])
    cs = cs_ref[...]
    sn = sn_ref[...]
    kscale = HEAD_DIM ** -0.5
    for gi in range(EVEN_MAIN // 512):
        acc = jnp.dot(u, w_ref[:, gi * 512:(gi + 1) * 512], preferred_element_type=F32)
        if gi in (3, 4):
            parts = []
            for hh in range(RET_HEADS):
                blk = acc[:, hh * HEAD_DIM:(hh + 1) * HEAD_DIM]
                parts.append(blk * cs + pltpu.roll(blk, HEAD_DIM // 2, 1) * sn)
            acc = jnp.concatenate(parts, axis=1)
        if gi in (1, 4):
            acc = acc * kscale
        proj_ref[:, gi * 512:(gi + 1) * 512] = acc.astype(BF16)
    gates_ref[...] = jnp.dot(u, wg_ref[...], preferred_element_type=F32)


def _combine(h, y0, y1, gk, gate):
    moe = gk[:, 0:1] * y0.astype(F32) + gk[:, 1:2] * y1.astype(F32)
    return h + gate * moe


def _proj_odd_kernel(h_ref, y0_ref, y1_ref, gk_ref, modp_ref, mod_ref, g_ref, w_ref, wvt_ref,
                     c_ref, s1_ref, s2_ref, hout_ref, qk_ref, vt_ref):
    h = _combine(h_ref[...], y0_ref[...], y1_ref[...], gk_ref[...], modp_ref[5:6, :])
    hout_ref[...] = h
    u = _adaln_bf16(h, g_ref[...], mod_ref[0:1, :], mod_ref[1:2, :])
    c = c_ref[...]
    s1 = s1_ref[...]
    s2 = s2_ref[...]
    qscale = DA_DHEAD ** -0.5 * math.log2(math.e)
    nq = DA_HEADS * 2 * DA_DHEAD
    for gi in range(2 * nq // 512):
        acc = jnp.dot(u, w_ref[:, gi * 512:(gi + 1) * 512], preferred_element_type=F32)
        parts = []
        for hh in range(512 // 128):
            blk = acc[:, hh * 128:(hh + 1) * 128]
            parts.append(blk * c + pltpu.roll(blk, 128 - 16, 1) * s1 + pltpu.roll(blk, 16, 1) * s2)
        acc = jnp.concatenate(parts, axis=1)
        if gi * 512 < nq:
            acc = acc * qscale
        qk_ref[:, gi * 512:(gi + 1) * 512] = acc.astype(BF16)
    vt = lax.dot_general(wvt_ref[...], u, (((1,), (1,)), ((), ())), preferred_element_type=F32)
    vt_ref[...] = vt.astype(BF16)


def _tile_maps(B, S, L, tr):
    lat_tiles = B * S // tr
    per_b = S // tr

    def mod_idx(i):
        return jnp.where(i < lat_tiles, i // per_b, B)

    def rot_idx(i):
        return jnp.where(i < lat_tiles, i % per_b, per_b + (i - lat_tiles))

    return mod_idx, rot_idx


def _proj_even(h, mod, g, w_main, w_gates, cs, sn, dims):
    B, S, L, tr = dims
    n = h.shape[0]
    mod_idx, rot_idx = _tile_maps(B, S, L, tr)
    const = lambda i: (0, 0)
    return pl.pallas_call(
        _proj_even_kernel,
        grid=(n // tr,),
        in_specs=[pl.BlockSpec((tr, D_MODEL), lambda i: (i, 0)),
                  pl.BlockSpec((None, 8, D_MODEL), lambda i: (mod_idx(i), 0, 0)),
                  pl.BlockSpec((1, D_MODEL), const),
                  pl.BlockSpec((D_MODEL, EVEN_MAIN), const),
                  pl.BlockSpec((D_MODEL, GATE_PAD), const),
                  pl.BlockSpec((tr, 128), lambda i: (rot_idx(i), 0)),
                  pl.BlockSpec((tr, 128), lambda i: (rot_idx(i), 0))],
        out_specs=[pl.BlockSpec((tr, EVEN_MAIN), lambda i: (i, 0)),
                   pl.BlockSpec((tr, GATE_PAD), lambda i: (i, 0))],
        out_shape=[jax.ShapeDtypeStruct((n, EVEN_MAIN), BF16),
                   jax.ShapeDtypeStruct((n, GATE_PAD), F32)],
        compiler_params=_cparams(("arbitrary",)),
        name="proj_even",
    )(h, mod, g, w_main, w_gates, cs, sn)


def _proj_odd(h, y0, y1, gk, modp, mod, g, w_qk, w_vt, c, s1, s2, dims):
    B, S, L, tr = dims
    n = h.shape[0]
    mod_idx, rot_idx = _tile_maps(B, S, L, tr)
    const = lambda i: (0, 0)
    row = lambda i: (i, 0)
    nqk = w_qk.shape[1]
    return pl.pallas_call(
        _proj_odd_kernel,
        grid=(n // tr,),
        in_specs=[pl.BlockSpec((tr, D_MODEL), row),
                  pl.BlockSpec((tr, D_MODEL), row),
                  pl.BlockSpec((tr, D_MODEL), row),
                  pl.BlockSpec((tr, TOP_K), row),
                  pl.BlockSpec((None, 8, D_MODEL), lambda i: (mod_idx(i), 0, 0)),
                  pl.BlockSpec((None, 8, D_MODEL), lambda i: (mod_idx(i), 0, 0)),
                  pl.BlockSpec((1, D_MODEL), const),
                  pl.BlockSpec((D_MODEL, nqk), const),
                  pl.BlockSpec((D_MODEL, D_MODEL), const),
                  pl.BlockSpec((tr, 128), lambda i: (rot_idx(i), 0)),
                  pl.BlockSpec((tr, 128), lambda i: (rot_idx(i), 0)),
                  pl.BlockSpec((tr, 128), lambda i: (rot_idx(i), 0))],
        out_specs=[pl.BlockSpec((tr, D_MODEL), row),
                   pl.BlockSpec((tr, nqk), row),
                   pl.BlockSpec((D_MODEL, tr), lambda i: (0, i))],
        out_shape=[jax.ShapeDtypeStruct((n, D_MODEL), F32),
                   jax.ShapeDtypeStruct((n, nqk), BF16),
                   jax.ShapeDtypeStruct((D_MODEL, n), BF16)],
        compiler_params=_cparams(("arbitrary",)),
        name="proj_odd",
    )(h, y0, y1, gk, modp, mod, g, w_qk, w_vt, c, s1, s2)


_NT = (((1,), (1,)), ((), ()))
_TN = (((0,), (0,)), ((), ()))


def _scan_kernel(ld_ref, xf_ref, xb_ref, gf_ref, gb_ref, bias_ref, hf_ref, hb_ref,
                 ct_ref, n_ref, m_ref, r_ref, intra_ref, qd_ref, kd_ref):
    t = pl.program_id(1)
    nh = ML_HEADS
    row_i = lax.broadcasted_iota(jnp.int32, (CHUNK, CHUNK), 0)
    col_i = lax.broadcasted_iota(jnp.int32, (CHUNK, CHUNK), 1)
    row_f = row_i.astype(F32)
    col_f = col_i.astype(F32)

    @pl.when(t == 0)
    def _init():
        ct_ref[...] = jnp.zeros_like(ct_ref)
        n_ref[...] = jnp.zeros_like(n_ref)
        m_ref[...] = jnp.zeros_like(m_ref)
        r_ref[...] = jnp.zeros_like(r_ref)
        for d in range(2):
            for hh in range(nh):
                hd = d * nh + hh
                lg = ld_ref[hd]
                if d == 0:
                    intra = jnp.where(col_i <= row_i, jnp.exp(lg * (row_f - col_f)), 0.0)
                    qd = jnp.exp(lg * (row_f + 1.0))
                    kd = jnp.exp(lg * (CHUNK - 1.0 - row_f))
                else:
                    intra = jnp.where(col_i >= row_i, jnp.exp(lg * (col_f - row_f)), 0.0)
                    qd = jnp.exp(lg * (CHUNK - row_f))
                    kd = jnp.exp(lg * row_f)
                intra_ref[hd] = intra
                qd_ref[hd] = qd
                kd_ref[hd] = kd

    bias = bias_ref[...]
    dot = functools.partial(jnp.dot, preferred_element_type=F32)
    dg = functools.partial(lax.dot_general, preferred_element_type=F32)
    for d, (x_ref, g_ref, o_ref) in enumerate(((xf_ref, gf_ref, hf_ref), (xb_ref, gb_ref, hb_ref))):
        seen = (col_i <= row_i) if d == 0 else (col_i >= row_i)
        G = g_ref[...] + bias
        LF = jnp.minimum(G, 0.0) - jnp.log1p(jnp.exp(-jnp.abs(G)))
        tri = jnp.where(seen, 1.0, 0.0).astype(BF16)
        l_hi, l_mid, l_lo = _split3(LF)
        Bc = dot(tri, l_hi) + (dot(tri, l_mid) + dot(tri, l_lo))
        GT = G.T
        BT = Bc.T
        last = CHUNK - 1 if d == 0 else 0
        for hh in range(nh):
            hd = d * nh + hh
            c0 = hh * HEAD_DIM
            q = x_ref[:, c0:c0 + HEAD_DIM]
            k = x_ref[:, 512 + c0:512 + c0 + HEAD_DIM]
            v = x_ref[:, 1024 + c0:1024 + c0 + HEAD_DIM]
            ic_col = G[:, hd:hd + 1]
            ic_row = GT[hd:hd + 1, :]
            b_col = Bc[:, 8 + hd:9 + hd]
            b_row = BT[8 + hd:9 + hd, :]
            m0 = m_ref[hd][:, 0:1]
            dlog = jnp.where(seen, b_col - b_row + ic_row, NEG_BIG)
            inter = b_col + m0
            m_t = jnp.maximum(inter, jnp.max(dlog, axis=1, keepdims=True))
            s = dg(q, k, _NT) * jnp.exp(dlog - m_t)
            w_inter = jnp.exp(inter - m_t)
            ct = ct_ref[hd]
            n_row = n_ref[hd]
            numer = dot(s.astype(BF16), v) + w_inter * dot(q, ct.astype(BF16))
            denom = (jnp.sum(s, axis=1, keepdims=True)
                     + w_inter * jnp.sum(q.astype(F32) * n_row, axis=1, keepdims=True))
            hval = numer / jnp.maximum(jnp.abs(denom), jnp.exp(-m_t))
            o_ref[:, c0:c0 + HEAD_DIM] = hval.astype(BF16)
            b_last = b_col[last:last + 1, :]
            w_log_row = b_last - b_row + ic_row
            m_new = jnp.maximum(b_last + m0, jnp.max(w_log_row, axis=1, keepdims=True))
            decay = jnp.exp(b_last + m0 - m_new)
            w_col = jnp.exp(b_last - b_col + ic_col - m_new)
            w_row = jnp.exp(w_log_row - m_new)
            vw = (v.astype(F32) * w_col).astype(BF16)
            ct_ref[hd] = decay * ct + dg(k, vw, _TN)
            w8 = jnp.broadcast_to(w_row, (8, CHUNK)).astype(BF16)
            n_ref[hd] = decay * n_row + dot(w8, k)[0:1, :]
            m_ref[hd] = jnp.broadcast_to(m_new, (1, HEAD_DIM))
            rq = x_ref[:, 1536 + c0:1536 + c0 + HEAD_DIM]
            rk = x_ref[:, 2048 + c0:2048 + c0 + HEAD_DIM]
            rv = x_ref[:, 2560 + c0:2560 + c0 + HEAD_DIM]
            rs = dg(rq, rk, _NT) * intra_ref[hd]
            R = r_ref[hd]
            o = dot(rs.astype(BF16), rv) + qd_ref[hd] * dot(rq, R.astype(BF16))
            o_ref[:, 512 + c0:512 + c0 + HEAD_DIM] = o.astype(BF16)
            kdk = (rk.astype(F32) * kd_ref[hd]).astype(BF16)
            cdec = jnp.exp(ld_ref[hd] * jnp.full((1, HEAD_DIM), float(CHUNK), F32))
            r_ref[hd] = cdec * R + dg(kdk, rv, _TN)


def _scan(ret_ld8, proj, gates, bias, B, S, L):
    n = proj.shape[0]
    nlb, ncb = S // CHUNK, L // CHUNK
    nc = nlb + ncb

    def fwd(b, t):
        return jnp.where(t < ncb, B * nlb + b * ncb + t, b * nlb + t - ncb)

    def bwd(b, t):
        return jnp.where(t < ncb, B * nlb + b * ncb + (ncb - 1 - t), b * nlb + (nlb - 1 - (t - ncb)))

    state = pltpu.VMEM((2 * ML_HEADS, HEAD_DIM, HEAD_DIM), F32)
    vec = pltpu.VMEM((2 * ML_HEADS, 1, HEAD_DIM), F32)
    grid_spec = pltpu.PrefetchScalarGridSpec(
        num_scalar_prefetch=1,
        grid=(B, nc),
        in_specs=[pl.BlockSpec((CHUNK, EVEN_SCAN_COLS), lambda b, t, ld: (fwd(b, t), 0)),
                  pl.BlockSpec((CHUNK, EVEN_SCAN_COLS), lambda b, t, ld: (bwd(b, t), 0)),
                  pl.BlockSpec((CHUNK, GATE_PAD), lambda b, t, ld: (fwd(b, t), 0)),
                  pl.BlockSpec((CHUNK, GATE_PAD), lambda b, t, ld: (bwd(b, t), 0)),
                  pl.BlockSpec((1, GATE_PAD), lambda b, t, ld: (0, 0))],
        out_specs=[pl.BlockSpec((CHUNK, D_MODEL), lambda b, t, ld: (fwd(b, t), 0)),
                   pl.BlockSpec((CHUNK, D_MODEL), lambda b, t, ld: (bwd(b, t), 0))],
        scratch_shapes=[state, vec, vec, state, state, state, state],
    )
    return pl.pallas_call(
        _scan_kernel,
        grid_spec=grid_spec,
        out_shape=[jax.ShapeDtypeStruct((n, D_MODEL), BF16),
                   jax.ShapeDtypeStruct((n, D_MODEL), BF16)],
        compiler_params=_cparams(("arbitrary", "arbitrary")),
        name="scan",
    )(ret_ld8, proj, proj, gates, gates, bias)


def _post_tail(merged, h_ref, mod_ref, g2_ref, wout_ref, wr_ref, br_ref, hout_ref, v_ref, lg_ref):
    y = jnp.dot(merged, wout_ref[...], preferred_element_type=F32)
    h = h_ref[...] + mod_ref[2:3, :] * y
    hout_ref[...] = h
    v = (_rms_rows(h) * g2_ref[...]) * (1.0 + mod_ref[4:5, :]) + mod_ref[3:4, :]
    v_ref[...] = v.astype(BF16)
    lg_ref[...] = _dot_hi(v, wr_ref[...]) + br_ref[...]


def _head_ln(x, g):
    mu = jnp.mean(x, axis=-1, keepdims=True)
    xc = x - mu
    var = jnp.mean(xc * xc, axis=-1, keepdims=True)
    return xc * lax.rsqrt(var + EPS) * g


def _post_even_kernel(hf_ref, hb_ref, og_ref, mlg_ref, retg_ref, h_ref, mod_ref, g2_ref, wout_ref,
                      wr_ref, br_ref, hout_ref, v_ref, lg_ref):
    parts = []
    for hh in range(ML_HEADS):
        sl = slice(hh * HEAD_DIM, (hh + 1) * HEAD_DIM)
        ml = hf_ref[:, sl].astype(F32) + hb_ref[:, sl].astype(F32)
        parts.append(_head_ln(_sigmoid(og_ref[:, sl].astype(F32)) * ml, mlg_ref[:, sl]))
    for hh in range(RET_HEADS):
        sl = slice(512 + hh * HEAD_DIM, 512 + (hh + 1) * HEAD_DIM)
        ret = hf_ref[:, sl].astype(F32) + hb_ref[:, sl].astype(F32)
        rg = og_ref[:, sl].astype(F32)
        parts.append((rg * _sigmoid(rg)) * _head_ln(ret, retg_ref[:, hh * HEAD_DIM:(hh + 1) * HEAD_DIM]))
    merged = jnp.concatenate(parts, axis=1).astype(BF16)
    _post_tail(merged, h_ref, mod_ref, g2_ref, wout_ref, wr_ref, br_ref, hout_ref, v_ref, lg_ref)


def _post_odd_kernel(att_ref, h_ref, mod_ref, g2_ref, wout_ref, wr_ref, br_ref, hout_ref, v_ref, lg_ref):
    _post_tail(att_ref[...], h_ref, mod_ref, g2_ref, wout_ref, wr_ref, br_ref, hout_ref, v_ref, lg_ref)


def _post_specs(n, tr, mod_idx):
    const = lambda i: (0, 0)
    row = lambda i: (i, 0)
    tail_in = [pl.BlockSpec((tr, D_MODEL), row),
               pl.BlockSpec((None, 8, D_MODEL), lambda i: (mod_idx(i), 0, 0)),
               pl.BlockSpec((1, D_MODEL), const),
               pl.BlockSpec((D_MODEL, D_MODEL), const),
               pl.BlockSpec((D_MODEL, ROUTER_PAD), const),
               pl.BlockSpec((1, ROUTER_PAD), const)]
    out_specs = [pl.BlockSpec((tr, D_MODEL), row),
                 pl.BlockSpec((tr, D_MODEL), row),
                 pl.BlockSpec((tr, ROUTER_PAD), row)]
    out_shape = [jax.ShapeDtypeStruct((n, D_MODEL), F32),
                 jax.ShapeDtypeStruct((n, D_MODEL), BF16),
                 jax.ShapeDtypeStruct((n, ROUTER_PAD), F32)]
    return tail_in, out_specs, out_shape


def _post_even(hf, hb, proj, mlg, retg, h, mod, g2, wout, wr, br, dims):
    B, S, L, tr = dims
    n = h.shape[0]
    mod_idx, _ = _tile_maps(B, S, L, tr)
    tail_in, out_specs, out_shape = _post_specs(n, tr, mod_idx)
    row = lambda i: (i, 0)
    const = lambda i: (0, 0)
    return pl.pallas_call(
        _post_even_kernel,
        grid=(n // tr,),
        in_specs=[pl.BlockSpec((tr, D_MODEL), row),
                  pl.BlockSpec((tr, D_MODEL), row),
                  pl.BlockSpec((tr, D_MODEL), lambda i: (i, EVEN_SCAN_COLS // D_MODEL)),
                  pl.BlockSpec((1, 512), const),
                  pl.BlockSpec((1, 512), const)] + tail_in,
        out_specs=out_specs,
        out_shape=out_shape,
        compiler_params=_cparams(("arbitrary",)),
        name="post_even",
    )(hf, hb, proj, mlg, retg, h, mod, g2, wout, wr, br)


def _post_odd(att, h, mod, g2, wout, wr, br, dims):
    B, S, L, tr = dims
    n = att.shape[0]
    mod_idx, _ = _tile_maps(B, S, L, tr)
    tail_in, out_specs, out_shape = _post_specs(n, tr, mod_idx)
    return pl.pallas_call(
        _post_odd_kernel,
        grid=(n // tr,),
        in_specs=[pl.BlockSpec((tr, D_MODEL), lambda i: (i, 0))] + tail_in,
        out_specs=out_specs,
        out_shape=out_shape,
        compiler_params=_cparams(("arbitrary",)),
        name="post_odd",
    )(att, h, mod, g2, wout, wr, br)


SUM_ROWS = 16


def _attn_kernel(q_ref, kl_ref, kc_ref, vl_ref, vc_ref, lam_ref, g_ref, o_ref, acc_ref, m_ref,
                 *, tk, lam_init):
    tq = q_ref.shape[0]
    s_len = kl_ref.shape[0]
    q = q_ref[...]
    lane = lax.broadcasted_iota(jnp.int32, q.shape, 1)
    zero = jnp.zeros_like(q)
    qm = (jnp.where(lane < DA_DHEAD, q, zero), jnp.where(lane >= DA_DHEAD, q, zero))

    acc_ref[...] = jnp.zeros_like(acc_ref)
    m_ref[...] = jnp.full(m_ref.shape, NEG_BIG, F32)

    def update(kc, vtc):
        vext = jnp.concatenate([vtc, jnp.ones((SUM_ROWS, vtc.shape[1]), BF16)], axis=0)
        for mi in range(2):
            st = lax.dot_general(kc, qm[mi], _NT, preferred_element_type=F32)
            m_old = m_ref[mi]
            m_new = jnp.maximum(m_old, jnp.max(st, axis=0, keepdims=True))
            p = jnp.exp2(st - m_new).astype(BF16)
            acc_ref[mi] = jnp.exp2(m_old - m_new) * acc_ref[mi] + jnp.dot(vext, p, preferred_element_type=F32)
            m_ref[mi] = m_new

    def body(c, carry):
        off = pl.multiple_of(c * tk, tk)
        update(kl_ref[pl.ds(off, tk), :], vl_ref[:, pl.ds(off, tk)])
        return carry

    lax.fori_loop(0, s_len // tk, body, 0)
    update(kc_ref[...], vc_ref[...])

    lp = lam_ref[...]
    lam = (jnp.exp(jnp.sum(lp[0:1, :] * lp[1:2, :], axis=1, keepdims=True))
           - jnp.exp(jnp.sum(lp[2:3, :] * lp[3:4, :], axis=1, keepdims=True)) + lam_init)
    o0 = acc_ref[0, 0:128, :] / acc_ref[0, 128:129, :]
    o1 = acc_ref[1, 0:128, :] / acc_ref[1, 128:129, :]
    ot = o0 - lam * o1
    ot = ot * lax.rsqrt(jnp.mean(ot * ot, axis=0, keepdims=True) + EPS)
    o = ot.T * g_ref[...] * (1.0 - lam_init)
    o_ref[...] = o.astype(BF16)


def _attention(qk, vt, lam_p, norm_g, lam_init, B, S, L):
    tq = 512 if S % 512 == 0 else 256
    tk = 512 if S % 512 == 0 else 256
    nq = S // tq
    nh = DA_HEADS
    kcol = nh
    ctx0 = B * S // L
    kern = functools.partial(_attn_kernel, tk=tk, lam_init=lam_init)
    return pl.pallas_call(
        kern,
        grid=(B, nh, nq),
        in_specs=[pl.BlockSpec((tq, 128), lambda b, h, i: (b * nq + i, h)),
                  pl.BlockSpec((S, 128), lambda b, h, i: (b, kcol + h)),
                  pl.BlockSpec((L, 128), lambda b, h, i: (ctx0 + b, kcol + h)),
                  pl.BlockSpec((128, S), lambda b, h, i: (h, b)),
                  pl.BlockSpec((128, L), lambda b, h, i: (h, ctx0 + b)),
                  pl.BlockSpec((4, DA_DHEAD), lambda b, h, i: (0, 0)),
                  pl.BlockSpec((None, 1, 128), lambda b, h, i: (h, 0, 0))],
        out_specs=pl.BlockSpec((tq, 128), lambda b, h, i: (b * nq + i, h)),
        out_shape=jax.ShapeDtypeStruct((B * S, D_MODEL), BF16),
        scratch_shapes=[pltpu.VMEM((2, 128 + SUM_ROWS, tq), F32),
                        pltpu.VMEM((2, 1, tq), F32)],
        compiler_params=_cparams(("arbitrary", "arbitrary", "arbitrary")),
        name="diff_attention",
    )(qk, qk, qk, vt, vt, lam_p, norm_g.reshape(nh, 1, 128))


def _expert_kernel(be_ref, nu_ref, x_ref, w1_ref, w3_ref, w2_ref, y_ref, w1b_ref, w3b_ref, w2b_ref):
    i = pl.program_id(0)

    @pl.when((i == 0) | (be_ref[i] != be_ref[jnp.maximum(i - 1, 0)]))
    def _new_expert():
        w1b_ref[...] = w1_ref[...].astype(BF16)
        w3b_ref[...] = w3_ref[...].astype(BF16)
        w2b_ref[...] = w2_ref[...].astype(BF16)

    @pl.when(i < nu_ref[0])
    def _compute():
        x = x_ref[...]
        a = jnp.dot(x, w1b_ref[...], preferred_element_type=F32)
        b = jnp.dot(x, w3b_ref[...], preferred_element_type=F32)
        hm = ((a * _sigmoid(a)) * b).astype(BF16)
        y_ref[...] = jnp.dot(hm, w2b_ref[...], preferred_element_type=F32).astype(BF16)

    @pl.when(i >= nu_ref[0])
    def _skip():
        y_ref[...] = jnp.zeros_like(y_ref)


def _experts(blk_expert, n_used, xb, w1, w3, w2, layer):
    p = xb.shape[0]
    nblk = p // MOE_BLOCK
    wmap = lambda i, be, nu: (layer, be[i], 0, 0)
    grid_spec = pltpu.PrefetchScalarGridSpec(
        num_scalar_prefetch=2,
        grid=(nblk,),
        in_specs=[pl.BlockSpec((MOE_BLOCK, D_MODEL), lambda i, be, nu: (i, 0)),
                  pl.BlockSpec((None, None, D_MODEL, D_EXPERT), wmap),
                  pl.BlockSpec((None, None, D_MODEL, D_EXPERT), wmap),
                  pl.BlockSpec((None, None, D_EXPERT, D_MODEL), wmap)],
        out_specs=pl.BlockSpec((MOE_BLOCK, D_MODEL), lambda i, be, nu: (i, 0)),
        scratch_shapes=[pltpu.VMEM((D_MODEL, D_EXPERT), BF16),
                        pltpu.VMEM((D_MODEL, D_EXPERT), BF16),
                        pltpu.VMEM((D_EXPERT, D_MODEL), BF16)],
    )
    return pl.pallas_call(
        _expert_kernel,
        grid_spec=grid_spec,
        out_shape=jax.ShapeDtypeStruct((p, D_MODEL), BF16),
        compiler_params=_cparams(("arbitrary",)),
        name="experts",
    )(blk_expert, n_used, xb, w1, w3, w2)


def _route_kernel(lg_ref, idx_ref, gate_ref, cnt_ref, base_ref):
    tr = lg_ref.shape[0]

    @pl.when(pl.program_id(0) == 0)
    def _init():
        base_ref[...] = jnp.zeros_like(base_ref)

    lg = lg_ref[...]
    lane = lax.broadcasted_iota(jnp.int32, lg.shape, 1)
    first_arg = lambda x, mx: jnp.min(jnp.where(x == mx, lane, 128), axis=1, keepdims=True)
    is_g = lane < N_GROUPS
    gl = jnp.where(is_g, lg, NEG_BIG)
    gmax = jnp.max(gl, axis=1, keepdims=True)
    pg_top = 1.0 / jnp.sum(jnp.where(is_g, jnp.exp(gl - gmax), 0.0), axis=1, keepdims=True)
    grp = first_arg(gl, gmax)
    e_lo = N_GROUPS + grp * EXPERTS_PER_GROUP
    el = jnp.where((lane >= e_lo) & (lane < e_lo + EXPERTS_PER_GROUP), lg, NEG_BIG)
    m1 = jnp.max(el, axis=1, keepdims=True)
    i1 = first_arg(el, m1)
    el2 = jnp.where(lane == i1, NEG_BIG, el)
    m2 = jnp.max(el2, axis=1, keepdims=True)
    i2 = first_arg(el2, m2)
    p2 = jnp.exp(m2 - m1)
    g1 = pg_top / (1.0 + p2)
    g2 = pg_top * p2 / (1.0 + p2)

    hit1 = lane == i1
    hit2 = lane == i2
    onehot = jnp.where(hit1 | hit2, 1.0, 0.0)
    r_i = lax.broadcasted_iota(jnp.int32, (tr, tr), 0)
    c_i = lax.broadcasted_iota(jnp.int32, (tr, tr), 1)
    earlier = jnp.where(c_i < r_i, 1.0, 0.0).astype(BF16)
    before = jnp.dot(earlier, onehot.astype(BF16), preferred_element_type=F32) + base_ref[...]
    rank1 = jnp.sum(jnp.where(hit1, before, 0.0), axis=1, keepdims=True).astype(jnp.int32)
    rank2 = jnp.sum(jnp.where(hit2, before, 0.0), axis=1, keepdims=True).astype(jnp.int32)
    base = base_ref[...] + jnp.sum(onehot, axis=0, keepdims=True)
    base_ref[...] = base
    cnt_ref[...] = jnp.broadcast_to(base, cnt_ref.shape)

    idx_ref[...] = jnp.where(lane == 0, i1 - N_GROUPS,
                             jnp.where(lane == 1, i2 - N_GROUPS,
                                       jnp.where(lane == 2, rank1, jnp.where(lane == 3, rank2, 0))))
    gate_ref[...] = jnp.where(lane == 0, g1, jnp.where(lane == 1, g2, 0.0))


def _route(logits):
    n = logits.shape[0]
    tr = 512 if n % 512 == 0 else 256
    idx, gate, cnt = pl.pallas_call(
        _route_kernel,
        grid=(n // tr,),
        in_specs=[pl.BlockSpec((tr, ROUTER_PAD), lambda i: (i, 0))],
        out_specs=[pl.BlockSpec((tr, ROUTER_PAD), lambda i: (i, 0)),
                   pl.BlockSpec((tr, ROUTER_PAD), lambda i: (i, 0)),
                   pl.BlockSpec((8, ROUTER_PAD), lambda i: (0, 0))],
        out_shape=[jax.ShapeDtypeStruct((n, ROUTER_PAD), jnp.int32),
                   jax.ShapeDtypeStruct((n, ROUTER_PAD), F32),
                   jax.ShapeDtypeStruct((8, ROUTER_PAD), F32)],
        scratch_shapes=[pltpu.VMEM((1, ROUTER_PAD), F32)],
        compiler_params=_cparams(("arbitrary",)),
        name="route",
    )(logits)
    expert = idx[:, 0:TOP_K]
    rank = idx[:, TOP_K:2 * TOP_K]
    counts = cnt[0, N_GROUPS:N_GROUPS + N_EXPERTS].astype(jnp.int32)
    a = n * TOP_K
    nblk = -(-a // MOE_BLOCK) + N_EXPERTS
    padded = (counts + MOE_BLOCK - 1) // MOE_BLOCK * MOE_BLOCK
    pend = jnp.cumsum(padded)
    dest = (pend - padded)[expert] + rank
    tok = jnp.broadcast_to(jnp.arange(n, dtype=jnp.int32)[:, None], (n, TOP_K))
    buf_tok = jnp.zeros((nblk * MOE_BLOCK,), jnp.int32).at[dest.reshape(-1)].set(tok.reshape(-1))
    n_used = (pend[-1] // MOE_BLOCK).astype(jnp.int32)
    blk_expert = jnp.searchsorted(pend, jnp.arange(nblk, dtype=jnp.int32) * MOE_BLOCK, side='right')
    last_e = jnp.clip(jnp.searchsorted(pend, (n_used - 1) * MOE_BLOCK, side='right'), 0, N_EXPERTS - 1)
    blk_expert = jnp.where(jnp.arange(nblk) < n_used, jnp.clip(blk_expert, 0, N_EXPERTS - 1), last_e)
    return buf_tok, blk_expert.astype(jnp.int32), n_used.reshape(1), dest, gate[:, 0:TOP_K]


def _moe(v, logits, w1, w3, w2, layer):
    buf_tok, blk_expert, n_used, pos, gate = _route(logits)
    xb = jnp.take(v, buf_tok, axis=0)
    yb = _experts(blk_expert, n_used, xb, w1, w3, w2, layer)
    y0 = jnp.take(yb, pos[:, 0], axis=0)
    y1 = jnp.take(yb, pos[:, 1], axis=0)
    return y0, y1, gate


def _final_kernel(h_ref, y0_ref, y1_ref, gk_ref, modp_ref, g_ref, o_ref):
    h = _combine(h_ref[...], y0_ref[...], y1_ref[...], gk_ref[...], modp_ref[5:6, :])
    o_ref[...] = _rms_rows(h) * g_ref[...]


def _final(h, y0, y1, gk, modp, g, dims):
    B, S, L, tr = dims
    n = h.shape[0]
    mod_idx, _ = _tile_maps(B, S, L, tr)
    row = lambda i: (i, 0)
    return pl.pallas_call(
        _final_kernel,
        grid=(n // tr,),
        in_specs=[pl.BlockSpec((tr, D_MODEL), row),
                  pl.BlockSpec((tr, D_MODEL), row),
                  pl.BlockSpec((tr, D_MODEL), row),
                  pl.BlockSpec((tr, TOP_K), row),
                  pl.BlockSpec((None, 8, D_MODEL), lambda i: (mod_idx(i), 0, 0)),
                  pl.BlockSpec((1, D_MODEL), lambda i: (0, 0))],
        out_specs=pl.BlockSpec((tr, D_MODEL), row),
        out_shape=jax.ShapeDtypeStruct((n, D_MODEL), F32),
        compiler_params=_cparams(("arbitrary",)),
        name="final_norm",
    )(h, y0, y1, gk, modp, g)


def _rotary_tables_even(S, L, B):
    nf = HEAD_DIM // 2
    inv = ROPE_BASE ** (-jnp.arange(nf, dtype=F32) / nf)
    pos = jnp.concatenate([L + jnp.arange(S), jnp.tile(jnp.arange(L), B)]).astype(F32)
    ang = pos[:, None] * inv[None, :]
    cos, sin = jnp.cos(ang), jnp.sin(ang)
    return jnp.concatenate([cos, cos], 1), jnp.concatenate([-sin, sin], 1)


def _rotary_tables_odd(S, L, B):
    nf = DA_DHEAD // 4
    inv = ROPE_BASE ** (-jnp.arange(nf, dtype=F32) / nf)
    t = jnp.arange(S)
    rows, cols = (t // GRID_W).astype(F32), (t % GRID_W).astype(F32)
    j = np.arange(128)
    f_idx = j % nf
    use_col = (j % DA_DHEAD) >= DA_DHEAD // 2
    first = (j % (2 * nf)) < nf
    ang = jnp.where(use_col[None, :], cols[:, None], rows[:, None]) * inv[f_idx][None, :]
    cos, sin = jnp.cos(ang), jnp.sin(ang)
    c_lat = cos
    s1_lat = jnp.where(first[None, :], -sin, 0.0)
    s2_lat = jnp.where(first[None, :], 0.0, sin)
    nctx = B * L
    c = jnp.concatenate([c_lat, jnp.ones((nctx, 128), F32)])
    s1 = jnp.concatenate([s1_lat, jnp.zeros((nctx, 128), F32)])
    s2 = jnp.concatenate([s2_lat, jnp.zeros((nctx, 128), F32)])
    return c, s1, s2


def kernel(x, c, ctx, c_ctx, w_mod, b_mod, norm1_g, norm2_g, w_in_even, ml_gate_b, ml_norm_g, ret_log_decay, ret_norm_g, w_out_even, w_in_odd, da_lambda, da_norm_g, w_out_odd, router_g_w, router_g_b, router_e_w, router_e_b, w1, w3, w2, final_norm_g):
    B, S, D = x.shape
    L = ctx.shape[1]
    depth = w_mod.shape[0]
    assert D == D_MODEL and depth == 2 and S % CHUNK == 0 and L % CHUNK == 0 and S % L == 0
    n_lat, n_ctx = B * S, B * L
    tr = _row_tile(S, n_ctx)
    dims = (B, S, L, tr)

    h = jnp.concatenate([x.reshape(n_lat, D), ctx.reshape(n_ctx, D)], 0)

    cond8 = jnp.zeros((8, D), F32).at[:B].set(c).at[B].set(c_ctx)
    mod = jnp.pad(_modulation(cond8, w_mod, b_mod).reshape(depth, 8, 6, D), ((0, 0), (0, 0), (0, 2), (0, 0)))

    def router_w(l):
        wr = jnp.zeros((D, ROUTER_PAD), F32)
        wr = wr.at[:, :N_GROUPS].set(router_g_w[l]).at[:, N_GROUPS:N_GROUPS + N_EXPERTS].set(router_e_w[l])
        br = jnp.zeros((1, ROUTER_PAD), F32)
        br = br.at[0, :N_GROUPS].set(router_g_b[l]).at[0, N_GROUPS:N_GROUPS + N_EXPERTS].set(router_e_b[l])
        return wr, br

    wi = w_in_even[0]
    mq, mk, mv, mo, mi, mf, rq, rk, rv, rg = jnp.split(wi, np.cumsum(
        [512, 512, 512, 512, 8, 8, 512, 512, 512])[:9].tolist(), axis=1)
    w_main = jnp.concatenate([mq, mk, mv, rq, rk, rv, mo, rg], 1).astype(BF16)
    w_gates = jnp.pad(jnp.concatenate([mi, mf], 1), ((0, 0), (0, GATE_PAD - 16))).astype(BF16)
    gate_bias = jnp.pad(jnp.concatenate([ml_gate_b[0][:, 0].reshape(-1), ml_gate_b[0][:, 1].reshape(-1)]),
                        (0, GATE_PAD - 16)).reshape(1, GATE_PAD)
    cs, sn = _rotary_tables_even(S, L, B)
    proj, gates = _proj_even(h, mod[0], norm1_g[0].reshape(1, D), w_main, w_gates, cs, sn, dims)
    hf, hb = _scan(ret_log_decay[0].reshape(-1), proj, gates, gate_bias, B, S, L)
    wr, br = router_w(0)
    h, v, logits = _post_even(hf, hb, proj, ml_norm_g[0].reshape(1, -1), ret_norm_g[0].reshape(1, -1), h,
                              mod[0], norm2_g[0].reshape(1, D), w_out_even[0].astype(BF16), wr, br, dims)
    y0, y1, gk = _moe(v, logits, w1, w3, w2, 0)

    lam_init = 0.8 - 0.6 * math.exp(-0.3 * 1)
    nqk = 4 * DA_HEADS * DA_DHEAD
    w_qk = w_in_odd[0][:, :nqk].astype(BF16)
    w_vt = w_in_odd[0][:, nqk:].T.astype(BF16)
    c2, s1, s2 = _rotary_tables_odd(S, L, B)
    h, qk, vt = _proj_odd(h, y0, y1, gk, mod[0], mod[1], norm1_g[1].reshape(1, D), w_qk, w_vt, c2, s1, s2, dims)
    att = _attention(qk, vt, da_lambda[0], da_norm_g[0], lam_init, B, S, L)
    wr, br = router_w(1)
    h_lat, v, logits = _post_odd(att, h, mod[1], norm2_g[1].reshape(1, D), w_out_odd[0].astype(BF16), wr, br, dims)
    y0, y1, gk = _moe(v, logits, w1, w3, w2, 1)
    out = _final(h_lat, y0, y1, gk, mod[1], final_norm_g.reshape(1, D), dims)
    return out.reshape(B, S, D)
```

```python
import functools
import math

import numpy as np
import jax
import jax.numpy as jnp
from jax import lax
from jax.experimental import pallas as pl
from jax.experimental.pallas import tpu as pltpu

F32 = jnp.float32
BF16 = jnp.bfloat16

D_MODEL = 1024
GRID_W = 64
CHUNK = 128
EPS = 1e-6
ROPE_BASE = 10000.0
ML_HEADS = 4
RET_HEADS = 4
HEAD_DIM = 128
DA_HEADS = 8
DA_DHEAD = 64
N_GROUPS = 4
EXPERTS_PER_GROUP = 8
N_EXPERTS = N_GROUPS * EXPERTS_PER_GROUP
TOP_K = 2
D_EXPERT = D_MODEL // 2
MOE_BLOCK = 256
ROUTER_PAD = 128
GATE_PAD = 128
EVEN_MAIN = 4096
EVEN_SCAN_COLS = 3072
NEG_BIG = -1e30

V7X_VMEM_LIMIT = 56 * 1024 * 1024


def _cparams(sem):
    return pltpu.CompilerParams(dimension_semantics=sem, vmem_limit_bytes=V7X_VMEM_LIMIT)


def _row_tile(n_lat_per_batch, n_ctx_total):
    for t in (512, 256, 128):
        if n_lat_per_batch % t == 0 and n_ctx_total % t == 0:
            return t
    raise ValueError("row counts must be multiples of 128")


def _split3(a):
    hi = a.astype(BF16)
    r1 = a - hi.astype(F32)
    mid = r1.astype(BF16)
    lo = (r1 - mid.astype(F32)).astype(BF16)
    return hi, mid, lo


def _dot_hi(a, b):
    ah, al, _ = _split3(a)
    bh, bl, _ = _split3(b)
    d = functools.partial(jnp.dot, preferred_element_type=F32)
    return d(ah, bh) + (d(ah, bl) + d(al, bh))


def _sigmoid(x):
    return 1.0 / (1.0 + jnp.exp(-x))


def _rms_rows(x):
    return x * lax.rsqrt(jnp.mean(x * x, axis=-1, keepdims=True) + EPS)


def _mod_kernel(c_ref, w_ref, b_ref, o_ref):
    c = c_ref[...]
    o_ref[...] = _dot_hi(c * _sigmoid(c), w_ref[...]) + b_ref[...]


def _modulation(cond8, w_mod, b_mod):
    depth = w_mod.shape[0]
    ncol = w_mod.shape[2]
    tn = 1024
    return pl.pallas_call(
        _mod_kernel,
        grid=(depth, ncol // tn),
        in_specs=[pl.BlockSpec((8, D_MODEL), lambda l, j: (0, 0)),
                  pl.BlockSpec((None, D_MODEL, tn), lambda l, j: (l, 0, j)),
                  pl.BlockSpec((None, 1, tn), lambda l, j: (l, 0, j))],
        out_specs=pl.BlockSpec((None, 8, tn), lambda l, j: (l, 0, j)),
        out_shape=jax.ShapeDtypeStruct((depth, 8, ncol), F32),
        compiler_params=_cparams(("arbitrary", "arbitrary")),
        name="modulation",
    )(cond8, w_mod, b_mod.reshape(depth, 1, ncol))


def _adaln_bf16(x, g, shift, scale):
    return ((_rms_rows(x) * g) * (1.0 + scale) + shift).astype(BF16)


def _lat_or_ctx(x_ref, ctx_ref, lat_tiles):
    return jnp.where(pl.program_id(0) < lat_tiles, x_ref[...], ctx_ref[...])


def _proj_even_kernel(x_ref, ctx_ref, mod_ref, g_ref, w_ref, wg_ref, cs_ref, sn_ref, proj_ref, gates_ref,
                      *, lat_tiles):
    u = _adaln_bf16(_lat_or_ctx(x_ref, ctx_ref, lat_tiles), g_ref[...], mod_ref[0:1, :], mod_ref[1:2, :])
    cs = cs_ref[...]
    sn = sn_ref[...]
    kscale = HEAD_DIM ** -0.5
    for gi in range(EVEN_MAIN // 512):
        acc = jnp.dot(u, w_ref[:, gi * 512:(gi + 1) * 512], preferred_element_type=F32)
        if gi in (3, 4):
            parts = []
            for hh in range(RET_HEADS):
                blk = acc[:, hh * HEAD_DIM:(hh + 1) * HEAD_DIM]
                parts.append(blk * cs + pltpu.roll(blk, HEAD_DIM // 2, 1) * sn)
            acc = jnp.concatenate(parts, axis=1)
        if gi in (1, 4):
            acc = acc * kscale
        proj_ref[:, gi * 512:(gi + 1) * 512] = acc.astype(BF16)
    gates_ref[...] = jnp.dot(u, wg_ref[...], preferred_element_type=F32)


def _combine(h, y0, y1, gk, gate):
    moe = gk[:, 0:1] * y0.astype(F32) + gk[:, 1:2] * y1.astype(F32)
    return h + gate * moe


def _proj_odd_kernel(h_ref, y0_ref, y1_ref, gk_ref, modp_ref, mod_ref, g_ref, w_ref, wvt_ref,
                     c_ref, s1_ref, s2_ref, hout_ref, qk_ref, vt_ref):
    h = _combine(h_ref[...], y0_ref[...], y1_ref[...], gk_ref[...], modp_ref[5:6, :])
    hout_ref[...] = h
    u = _adaln_bf16(h, g_ref[...], mod_ref[0:1, :], mod_ref[1:2, :])
    c = c_ref[...]
    s1 = s1_ref[...]
    s2 = s2_ref[...]
    qscale = DA_DHEAD ** -0.5 * math.log2(math.e)
    nq = DA_HEADS * 2 * DA_DHEAD
    for gi in range(2 * nq // 512):
        acc = jnp.dot(u, w_ref[:, gi * 512:(gi + 1) * 512], preferred_element_type=F32)
        parts = []
        for hh in range(512 // 128):
            blk = acc[:, hh * 128:(hh + 1) * 128]
            parts.append(blk * c + pltpu.roll(blk, 128 - 16, 1) * s1 + pltpu.roll(blk, 16, 1) * s2)
        acc = jnp.concatenate(parts, axis=1)
        if gi * 512 < nq:
            acc = acc * qscale
        qk_ref[:, gi * 512:(gi + 1) * 512] = acc.astype(BF16)
    vt = lax.dot_general(wvt_ref[...], u, (((1,), (1,)), ((), ())), preferred_element_type=F32)
    vt_ref[...] = vt.astype(BF16)


def _tile_maps(B, S, L, tr):
    lat_tiles = B * S // tr
    per_b = S // tr

    def mod_idx(i):
        return jnp.where(i < lat_tiles, i // per_b, B)

    def rot_idx(i):
        return jnp.where(i < lat_tiles, i % per_b, per_b + (i - lat_tiles))

    return mod_idx, rot_idx


def _lat_ctx_specs(tr, lat_tiles):
    return [pl.BlockSpec((tr, D_MODEL), lambda i: (jnp.minimum(i, lat_tiles - 1), 0)),
            pl.BlockSpec((tr, D_MODEL), lambda i: (jnp.maximum(i - lat_tiles, 0), 0))]


def _proj_even(x2d, ctx2d, mod, g, w_main, w_gates, cs, sn, dims):
    B, S, L, tr = dims
    n = x2d.shape[0] + ctx2d.shape[0]
    lat_tiles = x2d.shape[0] // tr
    mod_idx, rot_idx = _tile_maps(B, S, L, tr)
    const = lambda i: (0, 0)
    return pl.pallas_call(
        functools.partial(_proj_even_kernel, lat_tiles=lat_tiles),
        grid=(n // tr,),
        in_specs=_lat_ctx_specs(tr, lat_tiles) + [
                  pl.BlockSpec((None, 8, D_MODEL), lambda i: (mod_idx(i), 0, 0)),
                  pl.BlockSpec((1, D_MODEL), const),
                  pl.BlockSpec((D_MODEL, EVEN_MAIN), const),
                  pl.BlockSpec((D_MODEL, GATE_PAD), const),
                  pl.BlockSpec((tr, 128), lambda i: (rot_idx(i), 0)),
                  pl.BlockSpec((tr, 128), lambda i: (rot_idx(i), 0))],
        out_specs=[pl.BlockSpec((tr, EVEN_MAIN), lambda i: (i, 0)),
                   pl.BlockSpec((tr, GATE_PAD), lambda i: (i, 0))],
        out_shape=[jax.ShapeDtypeStruct((n, EVEN_MAIN), BF16),
                   jax.ShapeDtypeStruct((n, GATE_PAD), F32)],
        compiler_params=_cparams(("arbitrary",)),
        name="proj_even",
    )(x2d, ctx2d, mod, g, w_main, w_gates, cs, sn)


def _proj_odd(h, y0, y1, gk, modp, mod, g, w_qk, w_vt, c, s1, s2, dims):
    B, S, L, tr = dims
    n = h.shape[0]
    mod_idx, rot_idx = _tile_maps(B, S, L, tr)
    const = lambda i: (0, 0)
    row = lambda i: (i, 0)
    nqk = w_qk.shape[1]
    return pl.pallas_call(
        _proj_odd_kernel,
        grid=(n // tr,),
        in_specs=[pl.BlockSpec((tr, D_MODEL), row),
                  pl.BlockSpec((tr, D_MODEL), row),
                  pl.BlockSpec((tr, D_MODEL), row),
                  pl.BlockSpec((tr, TOP_K), row),
                  pl.BlockSpec((None, 8, D_MODEL), lambda i: (mod_idx(i), 0, 0)),
                  pl.BlockSpec((None, 8, D_MODEL), lambda i: (mod_idx(i), 0, 0)),
                  pl.BlockSpec((1, D_MODEL), const),
                  pl.BlockSpec((D_MODEL, nqk), const),
                  pl.BlockSpec((D_MODEL, D_MODEL), const),
                  pl.BlockSpec((tr, 128), lambda i: (rot_idx(i), 0)),
                  pl.BlockSpec((tr, 128), lambda i: (rot_idx(i), 0)),
                  pl.BlockSpec((tr, 128), lambda i: (rot_idx(i), 0))],
        out_specs=[pl.BlockSpec((tr, D_MODEL), row),
                   pl.BlockSpec((tr, nqk), row),
                   pl.BlockSpec((D_MODEL, tr), lambda i: (0, i))],
        out_shape=[jax.ShapeDtypeStruct((n, D_MODEL), F32),
                   jax.ShapeDtypeStruct((n, nqk), BF16),
                   jax.ShapeDtypeStruct((D_MODEL, n), BF16)],
        compiler_params=_cparams(("arbitrary",)),
        name="proj_odd",
    )(h, y0, y1, gk, modp, mod, g, w_qk, w_vt, c, s1, s2)


_NT = (((1,), (1,)), ((), ()))
_TN = (((0,), (0,)), ((), ()))


def _scan_kernel(ld_ref, xf_ref, xb_ref, gf_ref, gb_ref, bias_ref, hf_ref, hb_ref,
                 ct_ref, n_ref, m_ref, r_ref, intra_ref, qd_ref, kd_ref):
    t = pl.program_id(1)
    nh = ML_HEADS
    row_i = lax.broadcasted_iota(jnp.int32, (CHUNK, CHUNK), 0)
    col_i = lax.broadcasted_iota(jnp.int32, (CHUNK, CHUNK), 1)
    row_f = row_i.astype(F32)
    col_f = col_i.astype(F32)

    @pl.when(t == 0)
    def _init():
        ct_ref[...] = jnp.zeros_like(ct_ref)
        n_ref[...] = jnp.zeros_like(n_ref)
        m_ref[...] = jnp.zeros_like(m_ref)
        r_ref[...] = jnp.zeros_like(r_ref)
        for d in range(2):
            for hh in range(nh):
                hd = d * nh + hh
                lg = ld_ref[hd]
                if d == 0:
                    intra = jnp.where(col_i <= row_i, jnp.exp(lg * (row_f - col_f)), 0.0)
                    qd = jnp.exp(lg * (row_f + 1.0))
                    kd = jnp.exp(lg * (CHUNK - 1.0 - row_f))
                else:
                    intra = jnp.where(col_i >= row_i, jnp.exp(lg * (col_f - row_f)), 0.0)
                    qd = jnp.exp(lg * (CHUNK - row_f))
                    kd = jnp.exp(lg * row_f)
                intra_ref[hd] = intra
                qd_ref[hd] = qd
                kd_ref[hd] = kd

    bias = bias_ref[...]
    dot = functools.partial(jnp.dot, preferred_element_type=F32)
    dg = functools.partial(lax.dot_general, preferred_element_type=F32)
    for d, (x_ref, g_ref, o_ref) in enumerate(((xf_ref, gf_ref, hf_ref), (xb_ref, gb_ref, hb_ref))):
        seen = (col_i <= row_i) if d == 0 else (col_i >= row_i)
        G = g_ref[...] + bias
        LF = jnp.minimum(G, 0.0) - jnp.log1p(jnp.exp(-jnp.abs(G)))
        tri = jnp.where(seen, 1.0, 0.0).astype(BF16)
        l_hi, l_mid, l_lo = _split3(LF)
        Bc = dot(tri, l_hi) + (dot(tri, l_mid) + dot(tri, l_lo))
        GT = G.T
        BT = Bc.T
        last = CHUNK - 1 if d == 0 else 0
        for hh in range(nh):
            hd = d * nh + hh
            c0 = hh * HEAD_DIM
            q = x_ref[:, c0:c0 + HEAD_DIM]
            k = x_ref[:, 512 + c0:512 + c0 + HEAD_DIM]
            v = x_ref[:, 1024 + c0:1024 + c0 + HEAD_DIM]
            ic_col = G[:, hd:hd + 1]
            ic_row = GT[hd:hd + 1, :]
            b_col = Bc[:, 8 + hd:9 + hd]
            b_row = BT[8 + hd:9 + hd, :]
            m0 = m_ref[hd][:, 0:1]
            dlog = jnp.where(seen, b_col - b_row + ic_row, NEG_BIG)
            inter = b_col + m0
            m_t = jnp.maximum(inter, jnp.max(dlog, axis=1, keepdims=True))
            s = dg(q, k, _NT) * jnp.exp(dlog - m_t)
            w_inter = jnp.exp(inter - m_t)
            ct = ct_ref[hd]
            n_row = n_ref[hd]
            numer = dot(s.astype(BF16), v) + w_inter * dot(q, ct.astype(BF16))
            denom = (jnp.sum(s, axis=1, keepdims=True)
                     + w_inter * jnp.sum(q.astype(F32) * n_row, axis=1, keepdims=True))
            hval = numer / jnp.maximum(jnp.abs(denom), jnp.exp(-m_t))
            o_ref[:, c0:c0 + HEAD_DIM] = hval.astype(BF16)
            b_last = b_col[last:last + 1, :]
            w_log_row = b_last - b_row + ic_row
            m_new = jnp.maximum(b_last + m0, jnp.max(w_log_row, axis=1, keepdims=True))
            decay = jnp.exp(b_last + m0 - m_new)
            w_col = jnp.exp(b_last - b_col + ic_col - m_new)
            w_row = jnp.exp(w_log_row - m_new)
            vw = (v.astype(F32) * w_col).astype(BF16)
            ct_ref[hd] = decay * ct + dg(k, vw, _TN)
            w8 = jnp.broadcast_to(w_row, (8, CHUNK)).astype(BF16)
            n_ref[hd] = decay * n_row + dot(w8, k)[0:1, :]
            m_ref[hd] = jnp.broadcast_to(m_new, (1, HEAD_DIM))
            rq = x_ref[:, 1536 + c0:1536 + c0 + HEAD_DIM]
            rk = x_ref[:, 2048 + c0:2048 + c0 + HEAD_DIM]
            rv = x_ref[:, 2560 + c0:2560 + c0 + HEAD_DIM]
            rs = dg(rq, rk, _NT) * intra_ref[hd]
            R = r_ref[hd]
            o = dot(rs.astype(BF16), rv) + qd_ref[hd] * dot(rq, R.astype(BF16))
            o_ref[:, 512 + c0:512 + c0 + HEAD_DIM] = o.astype(BF16)
            kdk = (rk.astype(F32) * kd_ref[hd]).astype(BF16)
            cdec = jnp.exp(ld_ref[hd] * jnp.full((1, HEAD_DIM), float(CHUNK), F32))
            r_ref[hd] = cdec * R + dg(kdk, rv, _TN)


def _scan(ret_ld8, proj, gates, bias, B, S, L):
    n = proj.shape[0]
    nlb, ncb = S // CHUNK, L // CHUNK
    nc = nlb + ncb

    def fwd(b, t):
        return jnp.where(t < ncb, B * nlb + b * ncb + t, b * nlb + t - ncb)

    def bwd(b, t):
        return jnp.where(t < ncb, B * nlb + b * ncb + (ncb - 1 - t), b * nlb + (nlb - 1 - (t - ncb)))

    state = pltpu.VMEM((2 * ML_HEADS, HEAD_DIM, HEAD_DIM), F32)
    vec = pltpu.VMEM((2 * ML_HEADS, 1, HEAD_DIM), F32)
    grid_spec = pltpu.PrefetchScalarGridSpec(
        num_scalar_prefetch=1,
        grid=(B, nc),
        in_specs=[pl.BlockSpec((CHUNK, EVEN_SCAN_COLS), lambda b, t, ld: (fwd(b, t), 0)),
                  pl.BlockSpec((CHUNK, EVEN_SCAN_COLS), lambda b, t, ld: (bwd(b, t), 0)),
                  pl.BlockSpec((CHUNK, GATE_PAD), lambda b, t, ld: (fwd(b, t), 0)),
                  pl.BlockSpec((CHUNK, GATE_PAD), lambda b, t, ld: (bwd(b, t), 0)),
                  pl.BlockSpec((1, GATE_PAD), lambda b, t, ld: (0, 0))],
        out_specs=[pl.BlockSpec((CHUNK, D_MODEL), lambda b, t, ld: (fwd(b, t), 0)),
                   pl.BlockSpec((CHUNK, D_MODEL), lambda b, t, ld: (bwd(b, t), 0))],
        scratch_shapes=[state, vec, vec, state, state, state, state],
    )
    return pl.pallas_call(
        _scan_kernel,
        grid_spec=grid_spec,
        out_shape=[jax.ShapeDtypeStruct((n, D_MODEL), BF16),
                   jax.ShapeDtypeStruct((n, D_MODEL), BF16)],
        compiler_params=_cparams(("arbitrary", "arbitrary")),
        name="scan",
    )(ret_ld8, proj, proj, gates, gates, bias)


def _post_tail(merged, h_in, mod_ref, g2_ref, wout_ref, wr_ref, br_ref, hout_ref, v_ref, lg_ref):
    y = jnp.dot(merged, wout_ref[...], preferred_element_type=F32)
    h = h_in + mod_ref[2:3, :] * y
    hout_ref[...] = h
    v = (_rms_rows(h) * g2_ref[...]) * (1.0 + mod_ref[4:5, :]) + mod_ref[3:4, :]
    v_ref[...] = v.astype(BF16)
    lg_ref[...] = _dot_hi(v, wr_ref[...]) + br_ref[...]


def _head_ln(x, g):
    mu = jnp.mean(x, axis=-1, keepdims=True)
    xc = x - mu
    var = jnp.mean(xc * xc, axis=-1, keepdims=True)
    return xc * lax.rsqrt(var + EPS) * g


def _post_even_kernel(hf_ref, hb_ref, og_ref, mlg_ref, retg_ref, x_ref, ctx_ref, mod_ref, g2_ref, wout_ref,
                      wr_ref, br_ref, hout_ref, v_ref, lg_ref, *, lat_tiles):
    parts = []
    for hh in range(ML_HEADS):
        sl = slice(hh * HEAD_DIM, (hh + 1) * HEAD_DIM)
        ml = hf_ref[:, sl].astype(F32) + hb_ref[:, sl].astype(F32)
        parts.append(_head_ln(_sigmoid(og_ref[:, sl].astype(F32)) * ml, mlg_ref[:, sl]))
    for hh in range(RET_HEADS):
        sl = slice(512 + hh * HEAD_DIM, 512 + (hh + 1) * HEAD_DIM)
        ret = hf_ref[:, sl].astype(F32) + hb_ref[:, sl].astype(F32)
        rg = og_ref[:, sl].astype(F32)
        parts.append((rg * _sigmoid(rg)) * _head_ln(ret, retg_ref[:, hh * HEAD_DIM:(hh + 1) * HEAD_DIM]))
    merged = jnp.concatenate(parts, axis=1).astype(BF16)
    h_in = _lat_or_ctx(x_ref, ctx_ref, lat_tiles)
    _post_tail(merged, h_in, mod_ref, g2_ref, wout_ref, wr_ref, br_ref, hout_ref, v_ref, lg_ref)


def _post_odd_kernel(att_ref, h_ref, mod_ref, g2_ref, wout_ref, wr_ref, br_ref, hout_ref, v_ref, lg_ref):
    _post_tail(att_ref[...], h_ref[...], mod_ref, g2_ref, wout_ref, wr_ref, br_ref, hout_ref, v_ref, lg_ref)


def _post_specs(n, tr, mod_idx):
    const = lambda i: (0, 0)
    row = lambda i: (i, 0)
    tail_in = [pl.BlockSpec((tr, D_MODEL), row),
               pl.BlockSpec((None, 8, D_MODEL), lambda i: (mod_idx(i), 0, 0)),
               pl.BlockSpec((1, D_MODEL), const),
               pl.BlockSpec((D_MODEL, D_MODEL), const),
               pl.BlockSpec((D_MODEL, ROUTER_PAD), const),
               pl.BlockSpec((1, ROUTER_PAD), const)]
    out_specs = [pl.BlockSpec((tr, D_MODEL), row),
                 pl.BlockSpec((tr, D_MODEL), row),
                 pl.BlockSpec((tr, ROUTER_PAD), row)]
    out_shape = [jax.ShapeDtypeStruct((n, D_MODEL), F32),
                 jax.ShapeDtypeStruct((n, D_MODEL), BF16),
                 jax.ShapeDtypeStruct((n, ROUTER_PAD), F32)]
    return tail_in, out_specs, out_shape


def _post_even(hf, hb, proj, mlg, retg, x2d, ctx2d, mod, g2, wout, wr, br, dims):
    B, S, L, tr = dims
    n = hf.shape[0]
    mod_idx, _ = _tile_maps(B, S, L, tr)
    tail_in, out_specs, out_shape = _post_specs(n, tr, mod_idx)
    lat_tiles = x2d.shape[0] // tr
    row = lambda i: (i, 0)
    const = lambda i: (0, 0)
    return pl.pallas_call(
        functools.partial(_post_even_kernel, lat_tiles=lat_tiles),
        grid=(n // tr,),
        in_specs=[pl.BlockSpec((tr, D_MODEL), row),
                  pl.BlockSpec((tr, D_MODEL), row),
                  pl.BlockSpec((tr, D_MODEL), lambda i: (i, EVEN_SCAN_COLS // D_MODEL)),
                  pl.BlockSpec((1, 512), const),
                  pl.BlockSpec((1, 512), const)] + _lat_ctx_specs(tr, lat_tiles) + tail_in[1:],
        out_specs=out_specs,
        out_shape=out_shape,
        compiler_params=_cparams(("arbitrary",)),
        name="post_even",
    )(hf, hb, proj, mlg, retg, x2d, ctx2d, mod, g2, wout, wr, br)


def _post_odd(att, h, mod, g2, wout, wr, br, dims):
    B, S, L, tr = dims
    n = att.shape[0]
    mod_idx, _ = _tile_maps(B, S, L, tr)
    tail_in, out_specs, out_shape = _post_specs(n, tr, mod_idx)
    return pl.pallas_call(
        _post_odd_kernel,
        grid=(n // tr,),
        in_specs=[pl.BlockSpec((tr, D_MODEL), lambda i: (i, 0))] + tail_in,
        out_specs=out_specs,
        out_shape=out_shape,
        compiler_params=_cparams(("arbitrary",)),
        name="post_odd",
    )(att, h, mod, g2, wout, wr, br)


SUM_ROWS = 16


def _attn_kernel(q_ref, kl_ref, kc_ref, vl_ref, vc_ref, lam_ref, g_ref, o_ref,
                 acc_ref, m_ref, qm_ref, sa_ref, sb_ref, sc_ref, *, tk, lam_init):
    n_lat = kl_ref.shape[0] // tk
    q = q_ref[...]
    lane = lax.broadcasted_iota(jnp.int32, q.shape, 1)
    zero = jnp.zeros_like(q)
    qm_ref[0] = jnp.where(lane < DA_DHEAD, q, zero)
    qm_ref[1] = jnp.where(lane >= DA_DHEAD, q, zero)
    acc_ref[...] = jnp.zeros_like(acc_ref)
    m_ref[...] = jnp.full(m_ref.shape, NEG_BIG, F32)

    def lat_k(c):
        return kl_ref[pl.ds(pl.multiple_of(c * tk, tk), tk), :]

    def lat_v(c):
        return vl_ref[:, pl.ds(pl.multiple_of(c * tk, tk), tk)]

    def scores(dst_ref, kc):
        for mi in range(2):
            dst_ref[mi] = lax.dot_general(kc, qm_ref[mi], _NT, preferred_element_type=F32)

    def absorb(src_ref, vtc):
        vext = jnp.concatenate([vtc, jnp.ones((SUM_ROWS, vtc.shape[1]), BF16)], axis=0)
        for mi in range(2):
            st = src_ref[mi]
            m_old = m_ref[mi]
            m_new = jnp.maximum(m_old, jnp.max(st, axis=0, keepdims=True))
            p = jnp.exp2(st - m_new).astype(BF16)
            acc_ref[mi] = jnp.exp2(m_old - m_new) * acc_ref[mi] + jnp.dot(vext, p, preferred_element_type=F32)
            m_ref[mi] = m_new

    scores(sa_ref, lat_k(0))

    def body(j, carry):
        c = 2 * j
        scores(sb_ref, lat_k(c + 1))
        absorb(sa_ref, lat_v(c))
        scores(sa_ref, lat_k(c + 2))
        absorb(sb_ref, lat_v(c + 1))
        return carry

    lax.fori_loop(0, n_lat // 2 - 1, body, 0)
    scores(sb_ref, lat_k(n_lat - 1))
    absorb(sa_ref, lat_v(n_lat - 2))
    scores(sc_ref, kc_ref[...])
    absorb(sb_ref, lat_v(n_lat - 1))
    absorb(sc_ref, vc_ref[...])

    lp = lam_ref[...]
    lam = (jnp.exp(jnp.sum(lp[0:1, :] * lp[1:2, :], axis=1, keepdims=True))
           - jnp.exp(jnp.sum(lp[2:3, :] * lp[3:4, :], axis=1, keepdims=True)) + lam_init)
    o0 = acc_ref[0, 0:128, :] / acc_ref[0, 128:129, :]
    o1 = acc_ref[1, 0:128, :] / acc_ref[1, 128:129, :]
    ot = o0 - lam * o1
    ot = ot * lax.rsqrt(jnp.mean(ot * ot, axis=0, keepdims=True) + EPS)
    o = ot.T * g_ref[...] * (1.0 - lam_init)
    o_ref[...] = o.astype(BF16)


def _attention(qk, vt, lam_p, norm_g, lam_init, B, S, L):
    tq = 512
    tk = 512
    assert S % tq == 0 and S % (2 * tk) == 0
    nq = S // tq
    nh = DA_HEADS
    kcol = nh
    ctx0 = B * S // L
    kern = functools.partial(_attn_kernel, tk=tk, lam_init=lam_init)
    return pl.pallas_call(
        kern,
        grid=(B, nh, nq),
        in_specs=[pl.BlockSpec((tq, 128), lambda b, h, i: (b * nq + i, h)),
                  pl.BlockSpec((S, 128), lambda b, h, i: (b, kcol + h)),
                  pl.BlockSpec((L, 128), lambda b, h, i: (ctx0 + b, kcol + h)),
                  pl.BlockSpec((128, S), lambda b, h, i: (h, b)),
                  pl.BlockSpec((128, L), lambda b, h, i: (h, ctx0 + b)),
                  pl.BlockSpec((4, DA_DHEAD), lambda b, h, i: (0, 0)),
                  pl.BlockSpec((None, 1, 128), lambda b, h, i: (h, 0, 0))],
        out_specs=pl.BlockSpec((tq, 128), lambda b, h, i: (b * nq + i, h)),
        out_shape=jax.ShapeDtypeStruct((B * S, D_MODEL), BF16),
        scratch_shapes=[pltpu.VMEM((2, 128 + SUM_ROWS, tq), F32),
                        pltpu.VMEM((2, 1, tq), F32),
                        pltpu.VMEM((2, tq, 128), BF16),
                        pltpu.VMEM((2, tk, tq), F32),
                        pltpu.VMEM((2, tk, tq), F32),
                        pltpu.VMEM((2, L, tq), F32)],
        compiler_params=_cparams(("arbitrary", "arbitrary", "arbitrary")),
        name="diff_attention",
    )(qk, qk, qk, vt, vt, lam_p, norm_g.reshape(nh, 1, 128))


def _expert_kernel(be_ref, nu_ref, x_ref, w1_ref, w3_ref, w2_ref, y_ref, w1b_ref, w3b_ref, w2b_ref):
    i = pl.program_id(0)

    @pl.when((i == 0) | (be_ref[i] != be_ref[jnp.maximum(i - 1, 0)]))
    def _new_expert():
        w1b_ref[...] = w1_ref[...].astype(BF16)
        w3b_ref[...] = w3_ref[...].astype(BF16)
        w2b_ref[...] = w2_ref[...].astype(BF16)

    @pl.when(i < nu_ref[0])
    def _compute():
        x = x_ref[...]
        a = jnp.dot(x, w1b_ref[...], preferred_element_type=F32)
        b = jnp.dot(x, w3b_ref[...], preferred_element_type=F32)
        hm = ((a * _sigmoid(a)) * b).astype(BF16)
        y_ref[...] = jnp.dot(hm, w2b_ref[...], preferred_element_type=F32).astype(BF16)

    @pl.when(i >= nu_ref[0])
    def _skip():
        y_ref[...] = jnp.zeros_like(y_ref)


def _experts(blk_expert, n_used, xb, w1, w3, w2, layer):
    p = xb.shape[0]
    nblk = p // MOE_BLOCK
    wmap = lambda i, be, nu: (layer, be[i], 0, 0)
    grid_spec = pltpu.PrefetchScalarGridSpec(
        num_scalar_prefetch=2,
        grid=(nblk,),
        in_specs=[pl.BlockSpec((MOE_BLOCK, D_MODEL), lambda i, be, nu: (i, 0)),
                  pl.BlockSpec((None, None, D_MODEL, D_EXPERT), wmap),
                  pl.BlockSpec((None, None, D_MODEL, D_EXPERT), wmap),
                  pl.BlockSpec((None, None, D_EXPERT, D_MODEL), wmap)],
        out_specs=pl.BlockSpec((MOE_BLOCK, D_MODEL), lambda i, be, nu: (i, 0)),
        scratch_shapes=[pltpu.VMEM((D_MODEL, D_EXPERT), BF16),
                        pltpu.VMEM((D_MODEL, D_EXPERT), BF16),
                        pltpu.VMEM((D_EXPERT, D_MODEL), BF16)],
    )
    return pl.pallas_call(
        _expert_kernel,
        grid_spec=grid_spec,
        out_shape=jax.ShapeDtypeStruct((p, D_MODEL), BF16),
        compiler_params=_cparams(("arbitrary",)),
        name="experts",
    )(blk_expert, n_used, xb, w1, w3, w2)


def _route_kernel(lg_ref, idx_ref, gate_ref, cnt_ref, base_ref):
    tr = lg_ref.shape[0]

    @pl.when(pl.program_id(0) == 0)
    def _init():
        base_ref[...] = jnp.zeros_like(base_ref)

    lg = lg_ref[...]
    lane = lax.broadcasted_iota(jnp.int32, lg.shape, 1)
    first_arg = lambda x, mx: jnp.min(jnp.where(x == mx, lane, 128), axis=1, keepdims=True)
    is_g = lane < N_GROUPS
    gl = jnp.where(is_g, lg, NEG_BIG)
    gmax = jnp.max(gl, axis=1, keepdims=True)
    pg_top = 1.0 / jnp.sum(jnp.where(is_g, jnp.exp(gl - gmax), 0.0), axis=1, keepdims=True)
    grp = first_arg(gl, gmax)
    e_lo = N_GROUPS + grp * EXPERTS_PER_GROUP
    el = jnp.where((lane >= e_lo) & (lane < e_lo + EXPERTS_PER_GROUP), lg, NEG_BIG)
    m1 = jnp.max(el, axis=1, keepdims=True)
    i1 = first_arg(el, m1)
    el2 = jnp.where(lane == i1, NEG_BIG, el)
    m2 = jnp.max(el2, axis=1, keepdims=True)
    i2 = first_arg(el2, m2)
    p2 = jnp.exp(m2 - m1)
    g1 = pg_top / (1.0 + p2)
    g2 = pg_top * p2 / (1.0 + p2)

    hit1 = lane == i1
    hit2 = lane == i2
    onehot = jnp.where(hit1 | hit2, 1.0, 0.0)
    r_i = lax.broadcasted_iota(jnp.int32, (tr, tr), 0)
    c_i = lax.broadcasted_iota(jnp.int32, (tr, tr), 1)
    earlier = jnp.where(c_i < r_i, 1.0, 0.0).astype(BF16)
    before = jnp.dot(earlier, onehot.astype(BF16), preferred_element_type=F32) + base_ref[...]
    rank1 = jnp.sum(jnp.where(hit1, before, 0.0), axis=1, keepdims=True).astype(jnp.int32)
    rank2 = jnp.sum(jnp.where(hit2, before, 0.0), axis=1, keepdims=True).astype(jnp.int32)
    base = base_ref[...] + jnp.sum(onehot, axis=0, keepdims=True)
    base_ref[...] = base
    cnt_ref[...] = jnp.broadcast_to(base, cnt_ref.shape)

    idx_ref[...] = jnp.where(lane == 0, i1 - N_GROUPS,
                             jnp.where(lane == 1, i2 - N_GROUPS,
                                       jnp.where(lane == 2, rank1, jnp.where(lane == 3, rank2, 0))))
    gate_ref[...] = jnp.where(lane == 0, g1, jnp.where(lane == 1, g2, 0.0))


def _route(logits):
    n = logits.shape[0]
    tr = 512 if n % 512 == 0 else 256
    idx, gate, cnt = pl.pallas_call(
        _route_kernel,
        grid=(n // tr,),
        in_specs=[pl.BlockSpec((tr, ROUTER_PAD), lambda i: (i, 0))],
        out_specs=[pl.BlockSpec((tr, ROUTER_PAD), lambda i: (i, 0)),
                   pl.BlockSpec((tr, ROUTER_PAD), lambda i: (i, 0)),
                   pl.BlockSpec((8, ROUTER_PAD), lambda i: (0, 0))],
        out_shape=[jax.ShapeDtypeStruct((n, ROUTER_PAD), jnp.int32),
                   jax.ShapeDtypeStruct((n, ROUTER_PAD), F32),
                   jax.ShapeDtypeStruct((8, ROUTER_PAD), F32)],
        scratch_shapes=[pltpu.VMEM((1, ROUTER_PAD), F32)],
        compiler_params=_cparams(("arbitrary",)),
        name="route",
    )(logits)
    expert = idx[:, 0:TOP_K]
    rank = idx[:, TOP_K:2 * TOP_K]
    counts = cnt[0, N_GROUPS:N_GROUPS + N_EXPERTS].astype(jnp.int32)
    a = n * TOP_K
    nblk = -(-a // MOE_BLOCK) + N_EXPERTS
    padded = (counts + MOE_BLOCK - 1) // MOE_BLOCK * MOE_BLOCK
    pend = jnp.cumsum(padded)
    start = pend - padded
    hit = expert[:, :, None] == jnp.arange(N_EXPERTS, dtype=jnp.int32)[None, None, :]
    dest = jnp.sum(jnp.where(hit, start[None, None, :], 0), axis=-1) + rank
    tok = jnp.broadcast_to(jnp.arange(n, dtype=jnp.int32)[:, None], (n, TOP_K))
    buf_tok = jnp.zeros((nblk * MOE_BLOCK,), jnp.int32).at[dest.reshape(-1)].set(tok.reshape(-1))
    n_used = (pend[-1] // MOE_BLOCK).astype(jnp.int32)
    blk_row = jnp.minimum(jnp.arange(nblk, dtype=jnp.int32), n_used - 1) * MOE_BLOCK
    blk_expert = jnp.sum((pend[None, :] <= blk_row[:, None]).astype(jnp.int32), axis=1)
    blk_expert = jnp.clip(blk_expert, 0, N_EXPERTS - 1)
    return buf_tok, blk_expert.astype(jnp.int32), n_used.reshape(1), dest, gate[:, 0:TOP_K]


def _moe(v, logits, w1, w3, w2, layer):
    buf_tok, blk_expert, n_used, pos, gate = _route(logits)
    xb = jnp.take(v, buf_tok, axis=0)
    yb = _experts(blk_expert, n_used, xb, w1, w3, w2, layer)
    y0 = jnp.take(yb, pos[:, 0], axis=0)
    y1 = jnp.take(yb, pos[:, 1], axis=0)
    return y0, y1, gate


def _final_kernel(h_ref, y0_ref, y1_ref, gk_ref, modp_ref, g_ref, o_ref):
    h = _combine(h_ref[...], y0_ref[...], y1_ref[...], gk_ref[...], modp_ref[5:6, :])
    o_ref[...] = _rms_rows(h) * g_ref[...]


def _final(h, y0, y1, gk, modp, g, dims):
    B, S, L, tr = dims
    n = h.shape[0]
    mod_idx, _ = _tile_maps(B, S, L, tr)
    row = lambda i: (i, 0)
    return pl.pallas_call(
        _final_kernel,
        grid=(n // tr,),
        in_specs=[pl.BlockSpec((tr, D_MODEL), row),
                  pl.BlockSpec((tr, D_MODEL), row),
                  pl.BlockSpec((tr, D_MODEL), row),
                  pl.BlockSpec((tr, TOP_K), row),
                  pl.BlockSpec((None, 8, D_MODEL), lambda i: (mod_idx(i), 0, 0)),
                  pl.BlockSpec((1, D_MODEL), lambda i: (0, 0))],
        out_specs=pl.BlockSpec((tr, D_MODEL), row),
        out_shape=jax.ShapeDtypeStruct((n, D_MODEL), F32),
        compiler_params=_cparams(("arbitrary",)),
        name="final_norm",
    )(h, y0, y1, gk, modp, g)


def _rotary_tables_even(S, L, B):
    nf = HEAD_DIM // 2
    inv = ROPE_BASE ** (-jnp.arange(nf, dtype=F32) / nf)
    pos = jnp.concatenate([L + jnp.arange(S), jnp.tile(jnp.arange(L), B)]).astype(F32)
    ang = pos[:, None] * inv[None, :]
    cos, sin = jnp.cos(ang), jnp.sin(ang)
    return jnp.concatenate([cos, cos], 1), jnp.concatenate([-sin, sin], 1)


def _rotary_tables_odd(S, L, B):
    nf = DA_DHEAD // 4
    inv = ROPE_BASE ** (-jnp.arange(nf, dtype=F32) / nf)
    t = jnp.arange(S)
    rows, cols = (t // GRID_W).astype(F32), (t % GRID_W).astype(F32)
    j = np.arange(128)
    f_idx = j % nf
    use_col = (j % DA_DHEAD) >= DA_DHEAD // 2
    first = (j % (2 * nf)) < nf
    ang = jnp.where(use_col[None, :], cols[:, None], rows[:, None]) * inv[f_idx][None, :]
    cos, sin = jnp.cos(ang), jnp.sin(ang)
    c_lat = cos
    s1_lat = jnp.where(first[None, :], -sin, 0.0)
    s2_lat = jnp.where(first[None, :], 0.0, sin)
    nctx = B * L
    c = jnp.concatenate([c_lat, jnp.ones((nctx, 128), F32)])
    s1 = jnp.concatenate([s1_lat, jnp.zeros((nctx, 128), F32)])
    s2 = jnp.concatenate([s2_lat, jnp.zeros((nctx, 128), F32)])
    return c, s1, s2


def kernel(x, c, ctx, c_ctx, w_mod, b_mod, norm1_g, norm2_g, w_in_even, ml_gate_b, ml_norm_g, ret_log_decay, ret_norm_g, w_out_even, w_in_odd, da_lambda, da_norm_g, w_out_odd, router_g_w, router_g_b, router_e_w, router_e_b, w1, w3, w2, final_norm_g):
    B, S, D = x.shape
    L = ctx.shape[1]
    depth = w_mod.shape[0]
    assert D == D_MODEL and depth == 2 and S % CHUNK == 0 and L % CHUNK == 0 and S % L == 0
    n_lat, n_ctx = B * S, B * L
    tr = _row_tile(S, n_ctx)
    dims = (B, S, L, tr)

    x2d, ctx2d = x.reshape(n_lat, D), ctx.reshape(n_ctx, D)

    cond8 = jnp.zeros((8, D), F32).at[:B].set(c).at[B].set(c_ctx)
    mod = jnp.pad(_modulation(cond8, w_mod, b_mod).reshape(depth, 8, 6, D), ((0, 0), (0, 0), (0, 2), (0, 0)))

    def router_w(l):
        wr = jnp.zeros((D, ROUTER_PAD), F32)
        wr = wr.at[:, :N_GROUPS].set(router_g_w[l]).at[:, N_GROUPS:N_GROUPS + N_EXPERTS].set(router_e_w[l])
        br = jnp.zeros((1, ROUTER_PAD), F32)
        br = br.at[0, :N_GROUPS].set(router_g_b[l]).at[0, N_GROUPS:N_GROUPS + N_EXPERTS].set(router_e_b[l])
        return wr, br

    wi = w_in_even[0]
    mq, mk, mv, mo, mi, mf, rq, rk, rv, rg = jnp.split(wi, np.cumsum(
        [512, 512, 512, 512, 8, 8, 512, 512, 512])[:9].tolist(), axis=1)
    w_main = jnp.concatenate([mq, mk, mv, rq, rk, rv, mo, rg], 1).astype(BF16)
    w_gates = jnp.pad(jnp.concatenate([mi, mf], 1), ((0, 0), (0, GATE_PAD - 16))).astype(BF16)
    gate_bias = jnp.pad(jnp.concatenate([ml_gate_b[0][:, 0].reshape(-1), ml_gate_b[0][:, 1].reshape(-1)]),
                        (0, GATE_PAD - 16)).reshape(1, GATE_PAD)
    cs, sn = _rotary_tables_even(S, L, B)
    proj, gates = _proj_even(x2d, ctx2d, mod[0], norm1_g[0].reshape(1, D), w_main, w_gates, cs, sn, dims)
    hf, hb = _scan(ret_log_decay[0].reshape(-1), proj, gates, gate_bias, B, S, L)
    wr, br = router_w(0)
    h, v, logits = _post_even(hf, hb, proj, ml_norm_g[0].reshape(1, -1), ret_norm_g[0].reshape(1, -1),
                              x2d, ctx2d, mod[0], norm2_g[0].reshape(1, D), w_out_even[0].astype(BF16), wr, br, dims)
    y0, y1, gk = _moe(v, logits, w1, w3, w2, 0)

    lam_init = 0.8 - 0.6 * math.exp(-0.3 * 1)
    nqk = 4 * DA_HEADS * DA_DHEAD
    w_qk = w_in_odd[0][:, :nqk].astype(BF16)
    w_vt = w_in_odd[0][:, nqk:].T.astype(BF16)
    c2, s1, s2 = _rotary_tables_odd(S, L, B)
    h, qk, vt = _proj_odd(h, y0, y1, gk, mod[0], mod[1], norm1_g[1].reshape(1, D), w_qk, w_vt, c2, s1, s2, dims)
    att = _attention(qk, vt, da_lambda[0], da_norm_g[0], lam_init, B, S, L)
    wr, br = router_w(1)
    h_lat, v, logits = _post_odd(att, h, mod[1], norm2_g[1].reshape(1, D), w_out_odd[0].astype(BF16), wr, br, dims)
    y0, y1, gk = _moe(v, logits, w1, w3, w2, 1)
    out = _final(h_lat, y0, y1, gk, mod[1], final_norm_g.reshape(1, D), dims)
    return out.reshape(B, S, D)
```

```python
import functools
import math

import numpy as np
import jax
import jax.numpy as jnp
from jax import lax
from jax.experimental import pallas as pl
from jax.experimental.pallas import tpu as pltpu

F32 = jnp.float32
BF16 = jnp.bfloat16

D_MODEL = 1024
GRID_W = 64
CHUNK = 128
EPS = 1e-6
ROPE_BASE = 10000.0
ML_HEADS = 4
RET_HEADS = 4
HEAD_DIM = 128
DA_HEADS = 8
DA_DHEAD = 64
N_GROUPS = 4
EXPERTS_PER_GROUP = 8
N_EXPERTS = N_GROUPS * EXPERTS_PER_GROUP
TOP_K = 2
D_EXPERT = D_MODEL // 2
MOE_BLOCK = 256
MOE_GROUPS = 4
ROUTER_PAD = 128
GATE_PAD = 128
EVEN_MAIN = 4096
EVEN_SCAN_COLS = 3072
NEG_BIG = -1e30

V7X_VMEM_LIMIT = 56 * 1024 * 1024


def _cparams(sem):
    return pltpu.CompilerParams(dimension_semantics=sem, vmem_limit_bytes=V7X_VMEM_LIMIT)


def _row_tile(n_lat_per_batch, n_ctx_total):
    for t in (512, 256, 128):
        if n_lat_per_batch % t == 0 and n_ctx_total % t == 0:
            return t
    raise ValueError("row counts must be multiples of 128")


def _split3(a):
    hi = a.astype(BF16)
    r1 = a - hi.astype(F32)
    mid = r1.astype(BF16)
    lo = (r1 - mid.astype(F32)).astype(BF16)
    return hi, mid, lo


def _dot_hi(a, b):
    ah, al, _ = _split3(a)
    bh, bl, _ = _split3(b)
    d = functools.partial(jnp.dot, preferred_element_type=F32)
    return d(ah, bh) + (d(ah, bl) + d(al, bh))


def _sigmoid(x):
    return 1.0 / (1.0 + jnp.exp(-x))


def _rms_rows(x):
    return x * lax.rsqrt(jnp.mean(x * x, axis=-1, keepdims=True) + EPS)


def _mod_kernel(c_ref, w_ref, b_ref, o_ref):
    c = c_ref[...]
    o_ref[...] = _dot_hi(c * _sigmoid(c), w_ref[...]) + b_ref[...]


def _modulation(cond8, w_mod, b_mod):
    depth = w_mod.shape[0]
    ncol = w_mod.shape[2]
    tn = 1024
    return pl.pallas_call(
        _mod_kernel,
        grid=(depth, ncol // tn),
        in_specs=[pl.BlockSpec((8, D_MODEL), lambda l, j: (0, 0)),
                  pl.BlockSpec((None, D_MODEL, tn), lambda l, j: (l, 0, j)),
                  pl.BlockSpec((None, 1, tn), lambda l, j: (l, 0, j))],
        out_specs=pl.BlockSpec((None, 8, tn), lambda l, j: (l, 0, j)),
        out_shape=jax.ShapeDtypeStruct((depth, 8, ncol), F32),
        compiler_params=_cparams(("arbitrary", "arbitrary")),
        name="modulation",
    )(cond8, w_mod, b_mod.reshape(depth, 1, ncol))


def _adaln_bf16(x, g, shift, scale):
    return ((_rms_rows(x) * g) * (1.0 + scale) + shift).astype(BF16)


def _lat_or_ctx(x_ref, ctx_ref, lat_tiles):
    return jnp.where(pl.program_id(0) < lat_tiles, x_ref[...], ctx_ref[...])


def _proj_even_kernel(x_ref, ctx_ref, mod_ref, g_ref, w_ref, wg_ref, cs_ref, sn_ref, proj_ref, gates_ref,
                      *, lat_tiles):
    u = _adaln_bf16(_lat_or_ctx(x_ref, ctx_ref, lat_tiles), g_ref[...], mod_ref[0:1, :], mod_ref[1:2, :])
    cs = cs_ref[...]
    sn = sn_ref[...]
    kscale = HEAD_DIM ** -0.5
    for gi in range(EVEN_MAIN // 512):
        acc = jnp.dot(u, w_ref[:, gi * 512:(gi + 1) * 512], preferred_element_type=F32)
        if gi in (3, 4):
            parts = []
            for hh in range(RET_HEADS):
                blk = acc[:, hh * HEAD_DIM:(hh + 1) * HEAD_DIM]
                parts.append(blk * cs + pltpu.roll(blk, HEAD_DIM // 2, 1) * sn)
            acc = jnp.concatenate(parts, axis=1)
        if gi in (1, 4):
            acc = acc * kscale
        proj_ref[:, gi * 512:(gi + 1) * 512] = acc.astype(BF16)
    gates_ref[...] = jnp.dot(u, wg_ref[...], preferred_element_type=F32)


def _combine(h, y0, y1, gk, gate):
    moe = gk[:, 0:1] * y0.astype(F32) + gk[:, 1:2] * y1.astype(F32)
    return h + gate * moe


def _proj_odd_kernel(h_ref, y0_ref, y1_ref, gk_ref, modp_ref, mod_ref, g_ref, w_ref, wvt_ref,
                     c_ref, s1_ref, s2_ref, hout_ref, qk_ref, vt_ref):
    h = _combine(h_ref[...], y0_ref[...], y1_ref[...], gk_ref[...], modp_ref[5:6, :])
    hout_ref[...] = h
    u = _adaln_bf16(h, g_ref[...], mod_ref[0:1, :], mod_ref[1:2, :])
    c = c_ref[...]
    s1 = s1_ref[...]
    s2 = s2_ref[...]
    qscale = DA_DHEAD ** -0.5 * math.log2(math.e)
    nq = DA_HEADS * 2 * DA_DHEAD
    for gi in range(2 * nq // 512):
        acc = jnp.dot(u, w_ref[:, gi * 512:(gi + 1) * 512], preferred_element_type=F32)
        parts = []
        for hh in range(512 // 128):
            blk = acc[:, hh * 128:(hh + 1) * 128]
            parts.append(blk * c + pltpu.roll(blk, 128 - 16, 1) * s1 + pltpu.roll(blk, 16, 1) * s2)
        acc = jnp.concatenate(parts, axis=1)
        if gi * 512 < nq:
            acc = acc * qscale
        qk_ref[:, gi * 512:(gi + 1) * 512] = acc.astype(BF16)
    vt = lax.dot_general(wvt_ref[...], u, (((1,), (1,)), ((), ())), preferred_element_type=F32)
    vt_ref[...] = vt.astype(BF16)


def _tile_maps(B, S, L, tr):
    lat_tiles = B * S // tr
    per_b = S // tr

    def mod_idx(i):
        return jnp.where(i < lat_tiles, i // per_b, B)

    def rot_idx(i):
        return jnp.where(i < lat_tiles, i % per_b, per_b + (i - lat_tiles))

    return mod_idx, rot_idx


def _lat_ctx_specs(tr, lat_tiles):
    return [pl.BlockSpec((tr, D_MODEL), lambda i: (jnp.minimum(i, lat_tiles - 1), 0)),
            pl.BlockSpec((tr, D_MODEL), lambda i: (jnp.maximum(i - lat_tiles, 0), 0))]


def _proj_even(x2d, ctx2d, mod, g, w_main, w_gates, cs, sn, dims):
    B, S, L, tr = dims
    n = x2d.shape[0] + ctx2d.shape[0]
    lat_tiles = x2d.shape[0] // tr
    mod_idx, rot_idx = _tile_maps(B, S, L, tr)
    const = lambda i: (0, 0)
    return pl.pallas_call(
        functools.partial(_proj_even_kernel, lat_tiles=lat_tiles),
        grid=(n // tr,),
        in_specs=_lat_ctx_specs(tr, lat_tiles) + [
                  pl.BlockSpec((None, 8, D_MODEL), lambda i: (mod_idx(i), 0, 0)),
                  pl.BlockSpec((1, D_MODEL), const),
                  pl.BlockSpec((D_MODEL, EVEN_MAIN), const),
                  pl.BlockSpec((D_MODEL, GATE_PAD), const),
                  pl.BlockSpec((tr, 128), lambda i: (rot_idx(i), 0)),
                  pl.BlockSpec((tr, 128), lambda i: (rot_idx(i), 0))],
        out_specs=[pl.BlockSpec((tr, EVEN_MAIN), lambda i: (i, 0)),
                   pl.BlockSpec((tr, GATE_PAD), lambda i: (i, 0))],
        out_shape=[jax.ShapeDtypeStruct((n, EVEN_MAIN), BF16),
                   jax.ShapeDtypeStruct((n, GATE_PAD), F32)],
        compiler_params=_cparams(("arbitrary",)),
        name="proj_even",
    )(x2d, ctx2d, mod, g, w_main, w_gates, cs, sn)


def _proj_odd(h, y0, y1, gk, modp, mod, g, w_qk, w_vt, c, s1, s2, dims):
    B, S, L, tr = dims
    n = h.shape[0]
    mod_idx, rot_idx = _tile_maps(B, S, L, tr)
    const = lambda i: (0, 0)
    row = lambda i: (i, 0)
    nqk = w_qk.shape[1]
    return pl.pallas_call(
        _proj_odd_kernel,
        grid=(n // tr,),
        in_specs=[pl.BlockSpec((tr, D_MODEL), row),
                  pl.BlockSpec((tr, D_MODEL), row),
                  pl.BlockSpec((tr, D_MODEL), row),
                  pl.BlockSpec((tr, TOP_K), row),
                  pl.BlockSpec((None, 8, D_MODEL), lambda i: (mod_idx(i), 0, 0)),
                  pl.BlockSpec((None, 8, D_MODEL), lambda i: (mod_idx(i), 0, 0)),
                  pl.BlockSpec((1, D_MODEL), const),
                  pl.BlockSpec((D_MODEL, nqk), const),
                  pl.BlockSpec((D_MODEL, D_MODEL), const),
                  pl.BlockSpec((tr, 128), lambda i: (rot_idx(i), 0)),
                  pl.BlockSpec((tr, 128), lambda i: (rot_idx(i), 0)),
                  pl.BlockSpec((tr, 128), lambda i: (rot_idx(i), 0))],
        out_specs=[pl.BlockSpec((tr, D_MODEL), row),
                   pl.BlockSpec((tr, nqk), row),
                   pl.BlockSpec((D_MODEL, tr), lambda i: (0, i))],
        out_shape=[jax.ShapeDtypeStruct((n, D_MODEL), F32),
                   jax.ShapeDtypeStruct((n, nqk), BF16),
                   jax.ShapeDtypeStruct((D_MODEL, n), BF16)],
        compiler_params=_cparams(("arbitrary",)),
        name="proj_odd",
    )(h, y0, y1, gk, modp, mod, g, w_qk, w_vt, c, s1, s2)


_NT = (((1,), (1,)), ((), ()))
_TN = (((0,), (0,)), ((), ()))


def _scan_kernel(ld_ref, xf_ref, xb_ref, gf_ref, gb_ref, bias_ref, hf_ref, hb_ref,
                 ct_ref, n_ref, m_ref, r_ref, intra_ref, qd_ref, kd_ref):
    t = pl.program_id(1)
    nh = ML_HEADS
    row_i = lax.broadcasted_iota(jnp.int32, (CHUNK, CHUNK), 0)
    col_i = lax.broadcasted_iota(jnp.int32, (CHUNK, CHUNK), 1)
    row_f = row_i.astype(F32)
    col_f = col_i.astype(F32)

    @pl.when(t == 0)
    def _init():
        ct_ref[...] = jnp.zeros_like(ct_ref)
        n_ref[...] = jnp.zeros_like(n_ref)
        m_ref[...] = jnp.zeros_like(m_ref)
        r_ref[...] = jnp.zeros_like(r_ref)
        for d in range(2):
            for hh in range(nh):
                hd = d * nh + hh
                lg = ld_ref[hd]
                if d == 0:
                    intra = jnp.where(col_i <= row_i, jnp.exp(lg * (row_f - col_f)), 0.0)
                    qd = jnp.exp(lg * (row_f + 1.0))
                    kd = jnp.exp(lg * (CHUNK - 1.0 - row_f))
                else:
                    intra = jnp.where(col_i >= row_i, jnp.exp(lg * (col_f - row_f)), 0.0)
                    qd = jnp.exp(lg * (CHUNK - row_f))
                    kd = jnp.exp(lg * row_f)
                intra_ref[hd] = intra
                qd_ref[hd] = qd
                kd_ref[hd] = kd

    bias = bias_ref[...]
    dot = functools.partial(jnp.dot, preferred_element_type=F32)
    dg = functools.partial(lax.dot_general, preferred_element_type=F32)
    for d, (x_ref, g_ref, o_ref) in enumerate(((xf_ref, gf_ref, hf_ref), (xb_ref, gb_ref, hb_ref))):
        seen = (col_i <= row_i) if d == 0 else (col_i >= row_i)
        G = g_ref[...] + bias
        LF = jnp.minimum(G, 0.0) - jnp.log1p(jnp.exp(-jnp.abs(G)))
        tri = jnp.where(seen, 1.0, 0.0).astype(BF16)
        l_hi, l_mid, l_lo = _split3(LF)
        Bc = dot(tri, l_hi) + (dot(tri, l_mid) + dot(tri, l_lo))
        GT = G.T
        BT = Bc.T
        last = CHUNK - 1 if d == 0 else 0
        for hh in range(nh):
            hd = d * nh + hh
            c0 = hh * HEAD_DIM
            q = x_ref[:, c0:c0 + HEAD_DIM]
            k = x_ref[:, 512 + c0:512 + c0 + HEAD_DIM]
            v = x_ref[:, 1024 + c0:1024 + c0 + HEAD_DIM]
            ic_col = G[:, hd:hd + 1]
            ic_row = GT[hd:hd + 1, :]
            b_col = Bc[:, 8 + hd:9 + hd]
            b_row = BT[8 + hd:9 + hd, :]
            m0 = m_ref[hd][:, 0:1]
            dlog = jnp.where(seen, b_col - b_row + ic_row, NEG_BIG)
            inter = b_col + m0
            m_t = jnp.maximum(inter, jnp.max(dlog, axis=1, keepdims=True))
            s = dg(q, k, _NT) * jnp.exp(dlog - m_t)
            w_inter = jnp.exp(inter - m_t)
            ct = ct_ref[hd]
            n_row = n_ref[hd]
            numer = dot(s.astype(BF16), v) + w_inter * dot(q, ct.astype(BF16))
            denom = (jnp.sum(s, axis=1, keepdims=True)
                     + w_inter * jnp.sum(q.astype(F32) * n_row, axis=1, keepdims=True))
            hval = numer / jnp.maximum(jnp.abs(denom), jnp.exp(-m_t))
            o_ref[:, c0:c0 + HEAD_DIM] = hval.astype(BF16)
            b_last = b_col[last:last + 1, :]
            w_log_row = b_last - b_row + ic_row
            m_new = jnp.maximum(b_last + m0, jnp.max(w_log_row, axis=1, keepdims=True))
            decay = jnp.exp(b_last + m0 - m_new)
            w_col = jnp.exp(b_last - b_col + ic_col - m_new)
            w_row = jnp.exp(w_log_row - m_new)
            vw = (v.astype(F32) * w_col).astype(BF16)
            ct_ref[hd] = decay * ct + dg(k, vw, _TN)
            w8 = jnp.broadcast_to(w_row, (8, CHUNK)).astype(BF16)
            n_ref[hd] = decay * n_row + dot(w8, k)[0:1, :]
            m_ref[hd] = jnp.broadcast_to(m_new, (1, HEAD_DIM))
            rq = x_ref[:, 1536 + c0:1536 + c0 + HEAD_DIM]
            rk = x_ref[:, 2048 + c0:2048 + c0 + HEAD_DIM]
            rv = x_ref[:, 2560 + c0:2560 + c0 + HEAD_DIM]
            rs = dg(rq, rk, _NT) * intra_ref[hd]
            R = r_ref[hd]
            o = dot(rs.astype(BF16), rv) + qd_ref[hd] * dot(rq, R.astype(BF16))
            o_ref[:, 512 + c0:512 + c0 + HEAD_DIM] = o.astype(BF16)
            kdk = (rk.astype(F32) * kd_ref[hd]).astype(BF16)
            cdec = jnp.exp(ld_ref[hd] * jnp.full((1, HEAD_DIM), float(CHUNK), F32))
            r_ref[hd] = cdec * R + dg(kdk, rv, _TN)


def _scan(ret_ld8, proj, gates, bias, B, S, L):
    n = proj.shape[0]
    nlb, ncb = S // CHUNK, L // CHUNK
    nc = nlb + ncb

    def fwd(b, t):
        return jnp.where(t < ncb, B * nlb + b * ncb + t, b * nlb + t - ncb)

    def bwd(b, t):
        return jnp.where(t < ncb, B * nlb + b * ncb + (ncb - 1 - t), b * nlb + (nlb - 1 - (t - ncb)))

    state = pltpu.VMEM((2 * ML_HEADS, HEAD_DIM, HEAD_DIM), F32)
    vec = pltpu.VMEM((2 * ML_HEADS, 1, HEAD_DIM), F32)
    grid_spec = pltpu.PrefetchScalarGridSpec(
        num_scalar_prefetch=1,
        grid=(B, nc),
        in_specs=[pl.BlockSpec((CHUNK, EVEN_SCAN_COLS), lambda b, t, ld: (fwd(b, t), 0)),
                  pl.BlockSpec((CHUNK, EVEN_SCAN_COLS), lambda b, t, ld: (bwd(b, t), 0)),
                  pl.BlockSpec((CHUNK, GATE_PAD), lambda b, t, ld: (fwd(b, t), 0)),
                  pl.BlockSpec((CHUNK, GATE_PAD), lambda b, t, ld: (bwd(b, t), 0)),
                  pl.BlockSpec((1, GATE_PAD), lambda b, t, ld: (0, 0))],
        out_specs=[pl.BlockSpec((CHUNK, D_MODEL), lambda b, t, ld: (fwd(b, t), 0)),
                   pl.BlockSpec((CHUNK, D_MODEL), lambda b, t, ld: (bwd(b, t), 0))],
        scratch_shapes=[state, vec, vec, state, state, state, state],
    )
    return pl.pallas_call(
        _scan_kernel,
        grid_spec=grid_spec,
        out_shape=[jax.ShapeDtypeStruct((n, D_MODEL), BF16),
                   jax.ShapeDtypeStruct((n, D_MODEL), BF16)],
        compiler_params=_cparams(("arbitrary", "arbitrary")),
        name="scan",
    )(ret_ld8, proj, proj, gates, gates, bias)


def _post_tail(merged, h_in, mod_ref, g2_ref, wout_ref, wr_ref, br_ref, hout_ref, v_ref, lg_ref):
    y = jnp.dot(merged, wout_ref[...], preferred_element_type=F32)
    h = h_in + mod_ref[2:3, :] * y
    hout_ref[...] = h
    v = (_rms_rows(h) * g2_ref[...]) * (1.0 + mod_ref[4:5, :]) + mod_ref[3:4, :]
    v_ref[...] = v.astype(BF16)
    lg_ref[...] = _dot_hi(v, wr_ref[...]) + br_ref[...]


def _head_ln(x, g):
    mu = jnp.mean(x, axis=-1, keepdims=True)
    xc = x - mu
    var = jnp.mean(xc * xc, axis=-1, keepdims=True)
    return xc * lax.rsqrt(var + EPS) * g


def _post_even_kernel(hf_ref, hb_ref, og_ref, mlg_ref, retg_ref, x_ref, ctx_ref, mod_ref, g2_ref, wout_ref,
                      wr_ref, br_ref, hout_ref, v_ref, lg_ref, *, lat_tiles):
    parts = []
    for hh in range(ML_HEADS):
        sl = slice(hh * HEAD_DIM, (hh + 1) * HEAD_DIM)
        ml = hf_ref[:, sl].astype(F32) + hb_ref[:, sl].astype(F32)
        parts.append(_head_ln(_sigmoid(og_ref[:, sl].astype(F32)) * ml, mlg_ref[:, sl]))
    for hh in range(RET_HEADS):
        sl = slice(512 + hh * HEAD_DIM, 512 + (hh + 1) * HEAD_DIM)
        ret = hf_ref[:, sl].astype(F32) + hb_ref[:, sl].astype(F32)
        rg = og_ref[:, sl].astype(F32)
        parts.append((rg * _sigmoid(rg)) * _head_ln(ret, retg_ref[:, hh * HEAD_DIM:(hh + 1) * HEAD_DIM]))
    merged = jnp.concatenate(parts, axis=1).astype(BF16)
    h_in = _lat_or_ctx(x_ref, ctx_ref, lat_tiles)
    _post_tail(merged, h_in, mod_ref, g2_ref, wout_ref, wr_ref, br_ref, hout_ref, v_ref, lg_ref)


def _post_odd_kernel(att_ref, h_ref, mod_ref, g2_ref, wout_ref, wr_ref, br_ref, hout_ref, v_ref, lg_ref):
    _post_tail(att_ref[...], h_ref[...], mod_ref, g2_ref, wout_ref, wr_ref, br_ref, hout_ref, v_ref, lg_ref)


def _post_specs(n, tr, mod_idx):
    const = lambda i: (0, 0)
    row = lambda i: (i, 0)
    tail_in = [pl.BlockSpec((tr, D_MODEL), row),
               pl.BlockSpec((None, 8, D_MODEL), lambda i: (mod_idx(i), 0, 0)),
               pl.BlockSpec((1, D_MODEL), const),
               pl.BlockSpec((D_MODEL, D_MODEL), const),
               pl.BlockSpec((D_MODEL, ROUTER_PAD), const),
               pl.BlockSpec((1, ROUTER_PAD), const)]
    out_specs = [pl.BlockSpec((tr, D_MODEL), row),
                 pl.BlockSpec((tr, D_MODEL), row),
                 pl.BlockSpec((tr, ROUTER_PAD), row)]
    out_shape = [jax.ShapeDtypeStruct((n, D_MODEL), F32),
                 jax.ShapeDtypeStruct((n, D_MODEL), BF16),
                 jax.ShapeDtypeStruct((n, ROUTER_PAD), F32)]
    return tail_in, out_specs, out_shape


def _post_even(hf, hb, proj, mlg, retg, x2d, ctx2d, mod, g2, wout, wr, br, dims):
    B, S, L, tr = dims
    n = hf.shape[0]
    mod_idx, _ = _tile_maps(B, S, L, tr)
    tail_in, out_specs, out_shape = _post_specs(n, tr, mod_idx)
    lat_tiles = x2d.shape[0] // tr
    row = lambda i: (i, 0)
    const = lambda i: (0, 0)
    return pl.pallas_call(
        functools.partial(_post_even_kernel, lat_tiles=lat_tiles),
        grid=(n // tr,),
        in_specs=[pl.BlockSpec((tr, D_MODEL), row),
                  pl.BlockSpec((tr, D_MODEL), row),
                  pl.BlockSpec((tr, D_MODEL), lambda i: (i, EVEN_SCAN_COLS // D_MODEL)),
                  pl.BlockSpec((1, 512), const),
                  pl.BlockSpec((1, 512), const)] + _lat_ctx_specs(tr, lat_tiles) + tail_in[1:],
        out_specs=out_specs,
        out_shape=out_shape,
        compiler_params=_cparams(("arbitrary",)),
        name="post_even",
    )(hf, hb, proj, mlg, retg, x2d, ctx2d, mod, g2, wout, wr, br)


def _post_odd(att, h, mod, g2, wout, wr, br, dims):
    B, S, L, tr = dims
    n = att.shape[0]
    mod_idx, _ = _tile_maps(B, S, L, tr)
    tail_in, out_specs, out_shape = _post_specs(n, tr, mod_idx)
    return pl.pallas_call(
        _post_odd_kernel,
        grid=(n // tr,),
        in_specs=[pl.BlockSpec((tr, D_MODEL), lambda i: (i, 0))] + tail_in,
        out_specs=out_specs,
        out_shape=out_shape,
        compiler_params=_cparams(("arbitrary",)),
        name="post_odd",
    )(att, h, mod, g2, wout, wr, br)


SUM_ROWS = 16


def _attn_kernel(q_ref, kl_ref, kc_ref, vl_ref, vc_ref, lam_ref, g_ref, o_ref,
                 acc_ref, m_ref, cmax_ref, qm_ref, sa_ref, sb_ref, sc_ref, *, tk, lam_init):
    n_lat = kl_ref.shape[0] // tk
    q = q_ref[...]
    lane = lax.broadcasted_iota(jnp.int32, q.shape, 1)
    zero = jnp.zeros_like(q)
    qm_ref[0] = jnp.where(lane < DA_DHEAD, q, zero)
    qm_ref[1] = jnp.where(lane >= DA_DHEAD, q, zero)
    acc_ref[...] = jnp.zeros_like(acc_ref)
    m_ref[...] = jnp.full(m_ref.shape, NEG_BIG, F32)

    def lat_k(c):
        return kl_ref[pl.ds(pl.multiple_of(c * tk, tk), tk), :]

    def lat_v(c):
        return vl_ref[:, pl.ds(pl.multiple_of(c * tk, tk), tk)]

    def scores(dst_ref, slot, kc):
        for mi in range(2):
            st = lax.dot_general(kc, qm_ref[mi], _NT, preferred_element_type=F32)
            dst_ref[mi] = st
            cmax_ref[slot, mi] = jnp.max(st, axis=0, keepdims=True)

    def absorb(src_ref, slot, vtc):
        vext = jnp.concatenate([vtc, jnp.ones((SUM_ROWS, vtc.shape[1]), BF16)], axis=0)
        for mi in range(2):
            m_old = m_ref[mi]
            m_new = jnp.maximum(m_old, cmax_ref[slot, mi])
            p = jnp.exp2(src_ref[mi] - m_new).astype(BF16)
            acc_ref[mi] = jnp.exp2(m_old - m_new) * acc_ref[mi] + jnp.dot(vext, p, preferred_element_type=F32)
            m_ref[mi] = m_new

    scores(sa_ref, 0, lat_k(0))

    def body(j, carry):
        c = 2 * j
        scores(sb_ref, 1, lat_k(c + 1))
        absorb(sa_ref, 0, lat_v(c))
        scores(sa_ref, 0, lat_k(c + 2))
        absorb(sb_ref, 1, lat_v(c + 1))
        return carry

    lax.fori_loop(0, n_lat // 2 - 1, body, 0)
    scores(sb_ref, 1, lat_k(n_lat - 1))
    absorb(sa_ref, 0, lat_v(n_lat - 2))
    scores(sc_ref, 2, kc_ref[...])
    absorb(sb_ref, 1, lat_v(n_lat - 1))
    absorb(sc_ref, 2, vc_ref[...])

    lp = lam_ref[...]
    lam = (jnp.exp(jnp.sum(lp[0:1, :] * lp[1:2, :], axis=1, keepdims=True))
           - jnp.exp(jnp.sum(lp[2:3, :] * lp[3:4, :], axis=1, keepdims=True)) + lam_init)
    o0 = acc_ref[0, 0:128, :] / acc_ref[0, 128:129, :]
    o1 = acc_ref[1, 0:128, :] / acc_ref[1, 128:129, :]
    ot = o0 - lam * o1
    ot = ot * lax.rsqrt(jnp.mean(ot * ot, axis=0, keepdims=True) + EPS)
    o = ot.T * g_ref[...] * (1.0 - lam_init)
    o_ref[...] = o.astype(BF16)


def _attention(qk, vt, lam_p, norm_g, lam_init, B, S, L):
    tq = 512
    tk = 512
    assert S % tq == 0 and S % (2 * tk) == 0
    nq = S // tq
    nh = DA_HEADS
    kcol = nh
    ctx0 = B * S // L
    kern = functools.partial(_attn_kernel, tk=tk, lam_init=lam_init)
    return pl.pallas_call(
        kern,
        grid=(B, nh, nq),
        in_specs=[pl.BlockSpec((tq, 128), lambda b, h, i: (b * nq + i, h)),
                  pl.BlockSpec((S, 128), lambda b, h, i: (b, kcol + h)),
                  pl.BlockSpec((L, 128), lambda b, h, i: (ctx0 + b, kcol + h)),
                  pl.BlockSpec((128, S), lambda b, h, i: (h, b)),
                  pl.BlockSpec((128, L), lambda b, h, i: (h, ctx0 + b)),
                  pl.BlockSpec((4, DA_DHEAD), lambda b, h, i: (0, 0)),
                  pl.BlockSpec((None, 1, 128), lambda b, h, i: (h, 0, 0))],
        out_specs=pl.BlockSpec((tq, 128), lambda b, h, i: (b * nq + i, h)),
        out_shape=jax.ShapeDtypeStruct((B * S, D_MODEL), BF16),
        scratch_shapes=[pltpu.VMEM((2, 128 + SUM_ROWS, tq), F32),
                        pltpu.VMEM((2, 1, tq), F32),
                        pltpu.VMEM((3, 2, 1, tq), F32),
                        pltpu.VMEM((2, tq, 128), BF16),
                        pltpu.VMEM((2, tk, tq), F32),
                        pltpu.VMEM((2, tk, tq), F32),
                        pltpu.VMEM((2, L, tq), F32)],
        compiler_params=_cparams(("arbitrary", "arbitrary", "arbitrary")),
        name="diff_attention",
    )(qk, qk, qk, vt, vt, lam_p, norm_g.reshape(nh, 1, 128))


def _expert_kernel(be_ref, nu_ref, x_ref, w1_ref, w3_ref, w2_ref, yprev_ref, y_ref, w1b_ref, w3b_ref, w2b_ref):
    del yprev_ref
    i = pl.program_id(0)

    @pl.when((i == 0) | (be_ref[i] != be_ref[jnp.maximum(i - 1, 0)]))
    def _new_expert():
        w1b_ref[...] = w1_ref[...].astype(BF16)
        w3b_ref[...] = w3_ref[...].astype(BF16)
        w2b_ref[...] = w2_ref[...].astype(BF16)

    @pl.when(i < nu_ref[0])
    def _compute():
        x = x_ref[...]
        a = jnp.dot(x, w1b_ref[...], preferred_element_type=F32)
        b = jnp.dot(x, w3b_ref[...], preferred_element_type=F32)
        hm = ((a * _sigmoid(a)) * b).astype(BF16)
        y_ref[...] = jnp.dot(hm, w2b_ref[...], preferred_element_type=F32).astype(BF16)

    @pl.when(i >= nu_ref[0])
    def _skip():
        y_ref[...] = jnp.zeros_like(y_ref)


def _experts(blk_expert, n_used, xb, y_prev, w1, w3, w2, layer, first_block):
    nblk = blk_expert.shape[0]
    wmap = lambda i, be, nu: (layer, be[i], 0, 0)
    grid_spec = pltpu.PrefetchScalarGridSpec(
        num_scalar_prefetch=2,
        grid=(nblk,),
        in_specs=[pl.BlockSpec((MOE_BLOCK, D_MODEL), lambda i, be, nu: (i, 0)),
                  pl.BlockSpec((None, None, D_MODEL, D_EXPERT), wmap),
                  pl.BlockSpec((None, None, D_MODEL, D_EXPERT), wmap),
                  pl.BlockSpec((None, None, D_EXPERT, D_MODEL), wmap),
                  pl.BlockSpec(memory_space=pl.ANY)],
        out_specs=pl.BlockSpec((MOE_BLOCK, D_MODEL), lambda i, be, nu: (first_block + i, 0)),
        scratch_shapes=[pltpu.VMEM((D_MODEL, D_EXPERT), BF16),
                        pltpu.VMEM((D_MODEL, D_EXPERT), BF16),
                        pltpu.VMEM((D_EXPERT, D_MODEL), BF16)],
    )
    return pl.pallas_call(
        _expert_kernel,
        grid_spec=grid_spec,
        out_shape=jax.ShapeDtypeStruct(y_prev.shape, BF16),
        input_output_aliases={6: 0},
        compiler_params=_cparams(("arbitrary",)),
        name="experts",
    )(blk_expert, n_used, xb, w1, w3, w2, y_prev)


def _unwritten(shape, dtype):
    return pl.pallas_call(
        lambda o_ref: None,
        out_specs=pl.BlockSpec(memory_space=pl.ANY),
        out_shape=jax.ShapeDtypeStruct(shape, dtype),
        name="result_buffer",
    )()


def _route_kernel(lg_ref, idx_ref, gate_ref, cnt_ref, base_ref):
    tr = lg_ref.shape[0]

    @pl.when(pl.program_id(0) == 0)
    def _init():
        base_ref[...] = jnp.zeros_like(base_ref)

    lg = lg_ref[...]
    lane = lax.broadcasted_iota(jnp.int32, lg.shape, 1)
    first_arg = lambda x, mx: jnp.min(jnp.where(x == mx, lane, 128), axis=1, keepdims=True)
    is_g = lane < N_GROUPS
    gl = jnp.where(is_g, lg, NEG_BIG)
    gmax = jnp.max(gl, axis=1, keepdims=True)
    pg_top = 1.0 / jnp.sum(jnp.where(is_g, jnp.exp(gl - gmax), 0.0), axis=1, keepdims=True)
    grp = first_arg(gl, gmax)
    e_lo = N_GROUPS + grp * EXPERTS_PER_GROUP
    el = jnp.where((lane >= e_lo) & (lane < e_lo + EXPERTS_PER_GROUP), lg, NEG_BIG)
    m1 = jnp.max(el, axis=1, keepdims=True)
    i1 = first_arg(el, m1)
    el2 = jnp.where(lane == i1, NEG_BIG, el)
    m2 = jnp.max(el2, axis=1, keepdims=True)
    i2 = first_arg(el2, m2)
    p2 = jnp.exp(m2 - m1)
    g1 = pg_top / (1.0 + p2)
    g2 = pg_top * p2 / (1.0 + p2)

    hit1 = lane == i1
    hit2 = lane == i2
    onehot = jnp.where(hit1 | hit2, 1.0, 0.0)
    r_i = lax.broadcasted_iota(jnp.int32, (tr, tr), 0)
    c_i = lax.broadcasted_iota(jnp.int32, (tr, tr), 1)
    earlier = jnp.where(c_i < r_i, 1.0, 0.0).astype(BF16)
    before = jnp.dot(earlier, onehot.astype(BF16), preferred_element_type=F32) + base_ref[...]
    rank1 = jnp.sum(jnp.where(hit1, before, 0.0), axis=1, keepdims=True).astype(jnp.int32)
    rank2 = jnp.sum(jnp.where(hit2, before, 0.0), axis=1, keepdims=True).astype(jnp.int32)
    base = base_ref[...] + jnp.sum(onehot, axis=0, keepdims=True)
    base_ref[...] = base
    cnt_ref[...] = jnp.broadcast_to(base, cnt_ref.shape)

    idx_ref[...] = jnp.where(lane == 0, i1 - N_GROUPS,
                             jnp.where(lane == 1, i2 - N_GROUPS,
                                       jnp.where(lane == 2, rank1, jnp.where(lane == 3, rank2, 0))))
    gate_ref[...] = jnp.where(lane == 0, g1, jnp.where(lane == 1, g2, 0.0))


def _route(logits):
    n = logits.shape[0]
    tr = 512 if n % 512 == 0 else 256
    idx, gate, cnt = pl.pallas_call(
        _route_kernel,
        grid=(n // tr,),
        in_specs=[pl.BlockSpec((tr, ROUTER_PAD), lambda i: (i, 0))],
        out_specs=[pl.BlockSpec((tr, ROUTER_PAD), lambda i: (i, 0)),
                   pl.BlockSpec((tr, ROUTER_PAD), lambda i: (i, 0)),
                   pl.BlockSpec((8, ROUTER_PAD), lambda i: (0, 0))],
        out_shape=[jax.ShapeDtypeStruct((n, ROUTER_PAD), jnp.int32),
                   jax.ShapeDtypeStruct((n, ROUTER_PAD), F32),
                   jax.ShapeDtypeStruct((8, ROUTER_PAD), F32)],
        scratch_shapes=[pltpu.VMEM((1, ROUTER_PAD), F32)],
        compiler_params=_cparams(("arbitrary",)),
        name="route",
    )(logits)
    expert = idx[:, 0:TOP_K]
    rank = idx[:, TOP_K:2 * TOP_K]
    counts = cnt[0, N_GROUPS:N_GROUPS + N_EXPERTS].astype(jnp.int32)
    a = n * TOP_K
    nblk = -(-a // MOE_BLOCK) + N_EXPERTS
    padded = (counts + MOE_BLOCK - 1) // MOE_BLOCK * MOE_BLOCK
    pend = jnp.cumsum(padded)
    start = pend - padded
    hit = expert[:, :, None] == jnp.arange(N_EXPERTS, dtype=jnp.int32)[None, None, :]
    dest = jnp.sum(jnp.where(hit, start[None, None, :], 0), axis=-1) + rank
    tok = jnp.broadcast_to(jnp.arange(n, dtype=jnp.int32)[:, None], (n, TOP_K))
    buf_tok = jnp.zeros((nblk * MOE_BLOCK,), jnp.int32).at[dest.reshape(-1)].set(
        tok.reshape(-1), mode="promise_in_bounds", unique_indices=True)
    n_used = (pend[-1] // MOE_BLOCK).astype(jnp.int32)
    blk_row = jnp.minimum(jnp.arange(nblk, dtype=jnp.int32), n_used - 1) * MOE_BLOCK
    blk_expert = jnp.sum((pend[None, :] <= blk_row[:, None]).astype(jnp.int32), axis=1)
    blk_expert = jnp.clip(blk_expert, 0, N_EXPERTS - 1)
    return buf_tok, blk_expert.astype(jnp.int32), n_used.reshape(1), dest, gate[:, 0:TOP_K]


def _moe(v, logits, w1, w3, w2, layer):
    buf_tok, blk_expert, n_used, pos, gate = _route(logits)
    nblk = blk_expert.shape[0]
    groups = next(g for g in (MOE_GROUPS, 2, 1) if nblk % g == 0)
    per = nblk // groups
    yb = _unwritten((nblk * MOE_BLOCK, D_MODEL), BF16)
    for g in range(groups):
        rows = buf_tok[g * per * MOE_BLOCK:(g + 1) * per * MOE_BLOCK]
        xb = v.at[rows].get(mode="promise_in_bounds")
        n_g = jnp.clip(n_used - g * per, 0, per)
        yb = _experts(blk_expert[g * per:(g + 1) * per], n_g, xb, yb, w1, w3, w2, layer, g * per)
    y0 = yb.at[pos[:, 0]].get(mode="promise_in_bounds")
    y1 = yb.at[pos[:, 1]].get(mode="promise_in_bounds")
    return y0, y1, gate


def _final_kernel(h_ref, y0_ref, y1_ref, gk_ref, modp_ref, g_ref, o_ref):
    h = _combine(h_ref[...], y0_ref[...], y1_ref[...], gk_ref[...], modp_ref[5:6, :])
    o_ref[...] = _rms_rows(h) * g_ref[...]


def _final(h, y0, y1, gk, modp, g, dims):
    B, S, L, tr = dims
    n = h.shape[0]
    mod_idx, _ = _tile_maps(B, S, L, tr)
    row = lambda i: (i, 0)
    return pl.pallas_call(
        _final_kernel,
        grid=(n // tr,),
        in_specs=[pl.BlockSpec((tr, D_MODEL), row),
                  pl.BlockSpec((tr, D_MODEL), row),
                  pl.BlockSpec((tr, D_MODEL), row),
                  pl.BlockSpec((tr, TOP_K), row),
                  pl.BlockSpec((None, 8, D_MODEL), lambda i: (mod_idx(i), 0, 0)),
                  pl.BlockSpec((1, D_MODEL), lambda i: (0, 0))],
        out_specs=pl.BlockSpec((tr, D_MODEL), row),
        out_shape=jax.ShapeDtypeStruct((n, D_MODEL), F32),
        compiler_params=_cparams(("arbitrary",)),
        name="final_norm",
    )(h, y0, y1, gk, modp, g)


def _rotary_tables_even(S, L, B):
    nf = HEAD_DIM // 2
    inv = ROPE_BASE ** (-jnp.arange(nf, dtype=F32) / nf)
    pos = jnp.concatenate([L + jnp.arange(S), jnp.tile(jnp.arange(L), B)]).astype(F32)
    ang = pos[:, None] * inv[None, :]
    cos, sin = jnp.cos(ang), jnp.sin(ang)
    return jnp.concatenate([cos, cos], 1), jnp.concatenate([-sin, sin], 1)


def _rotary_tables_odd(S, L, B):
    nf = DA_DHEAD // 4
    inv = ROPE_BASE ** (-jnp.arange(nf, dtype=F32) / nf)
    t = jnp.arange(S)
    rows, cols = (t // GRID_W).astype(F32), (t % GRID_W).astype(F32)
    j = np.arange(128)
    f_idx = j % nf
    use_col = (j % DA_DHEAD) >= DA_DHEAD // 2
    first = (j % (2 * nf)) < nf
    ang = jnp.where(use_col[None, :], cols[:, None], rows[:, None]) * inv[f_idx][None, :]
    cos, sin = jnp.cos(ang), jnp.sin(ang)
    c_lat = cos
    s1_lat = jnp.where(first[None, :], -sin, 0.0)
    s2_lat = jnp.where(first[None, :], 0.0, sin)
    nctx = B * L
    c = jnp.concatenate([c_lat, jnp.ones((nctx, 128), F32)])
    s1 = jnp.concatenate([s1_lat, jnp.zeros((nctx, 128), F32)])
    s2 = jnp.concatenate([s2_lat, jnp.zeros((nctx, 128), F32)])
    return c, s1, s2


def kernel(x, c, ctx, c_ctx, w_mod, b_mod, norm1_g, norm2_g, w_in_even, ml_gate_b, ml_norm_g, ret_log_decay, ret_norm_g, w_out_even, w_in_odd, da_lambda, da_norm_g, w_out_odd, router_g_w, router_g_b, router_e_w, router_e_b, w1, w3, w2, final_norm_g):
    B, S, D = x.shape
    L = ctx.shape[1]
    depth = w_mod.shape[0]
    assert D == D_MODEL and depth == 2 and S % CHUNK == 0 and L % CHUNK == 0 and S % L == 0
    n_lat, n_ctx = B * S, B * L
    tr = _row_tile(S, n_ctx)
    dims = (B, S, L, tr)

    x2d, ctx2d = x.reshape(n_lat, D), ctx.reshape(n_ctx, D)

    cond8 = jnp.zeros((8, D), F32).at[:B].set(c).at[B].set(c_ctx)
    mod = jnp.pad(_modulation(cond8, w_mod, b_mod).reshape(depth, 8, 6, D), ((0, 0), (0, 0), (0, 2), (0, 0)))

    def router_w(l):
        wr = jnp.zeros((D, ROUTER_PAD), F32)
        wr = wr.at[:, :N_GROUPS].set(router_g_w[l]).at[:, N_GROUPS:N_GROUPS + N_EXPERTS].set(router_e_w[l])
        br = jnp.zeros((1, ROUTER_PAD), F32)
        br = br.at[0, :N_GROUPS].set(router_g_b[l]).at[0, N_GROUPS:N_GROUPS + N_EXPERTS].set(router_e_b[l])
        return wr, br

    wi = w_in_even[0]
    mq, mk, mv, mo, mi, mf, rq, rk, rv, rg = jnp.split(wi, np.cumsum(
        [512, 512, 512, 512, 8, 8, 512, 512, 512])[:9].tolist(), axis=1)
    w_main = jnp.concatenate([mq, mk, mv, rq, rk, rv, mo, rg], 1).astype(BF16)
    w_gates = jnp.pad(jnp.concatenate([mi, mf], 1), ((0, 0), (0, GATE_PAD - 16))).astype(BF16)
    gate_bias = jnp.pad(jnp.concatenate([ml_gate_b[0][:, 0].reshape(-1), ml_gate_b[0][:, 1].reshape(-1)]),
                        (0, GATE_PAD - 16)).reshape(1, GATE_PAD)
    cs, sn = _rotary_tables_even(S, L, B)
    proj, gates = _proj_even(x2d, ctx2d, mod[0], norm1_g[0].reshape(1, D), w_main, w_gates, cs, sn, dims)
    hf, hb = _scan(ret_log_decay[0].reshape(-1), proj, gates, gate_bias, B, S, L)
    wr, br = router_w(0)
    h, v, logits = _post_even(hf, hb, proj, ml_norm_g[0].reshape(1, -1), ret_norm_g[0].reshape(1, -1),
                              x2d, ctx2d, mod[0], norm2_g[0].reshape(1, D), w_out_even[0].astype(BF16), wr, br, dims)
    y0, y1, gk = _moe(v, logits, w1, w3, w2, 0)

    lam_init = 0.8 - 0.6 * math.exp(-0.3 * 1)
    nqk = 4 * DA_HEADS * DA_DHEAD
    w_qk = w_in_odd[0][:, :nqk].astype(BF16)
    w_vt = w_in_odd[0][:, nqk:].T.astype(BF16)
    c2, s1, s2 = _rotary_tables_odd(S, L, B)
    h, qk, vt = _proj_odd(h, y0, y1, gk, mod[0], mod[1], norm1_g[1].reshape(1, D), w_qk, w_vt, c2, s1, s2, dims)
    att = _attention(qk, vt, da_lambda[0], da_norm_g[0], lam_init, B, S, L)
    wr, br = router_w(1)
    h_lat, v, logits = _post_odd(att, h, mod[1], norm2_g[1].reshape(1, D), w_out_odd[0].astype(BF16), wr, br, dims)
    y0, y1, gk = _moe(v, logits, w1, w3, w2, 1)
    out = _final(h_lat, y0, y1, gk, mod[1], final_norm_g.reshape(1, D), dims)
    return out.reshape(B, S, D)
```

```python
import functools
import math

import numpy as np
import jax
import jax.numpy as jnp
from jax import lax
from jax.experimental import pallas as pl
from jax.experimental.pallas import tpu as pltpu

F32 = jnp.float32
BF16 = jnp.bfloat16

D_MODEL = 1024
GRID_W = 64
CHUNK = 128
EPS = 1e-6
ROPE_BASE = 10000.0
ML_HEADS = 4
RET_HEADS = 4
HEAD_DIM = 128
DA_HEADS = 8
DA_DHEAD = 64
N_GROUPS = 4
EXPERTS_PER_GROUP = 8
N_EXPERTS = N_GROUPS * EXPERTS_PER_GROUP
TOP_K = 2
D_EXPERT = D_MODEL // 2
MOE_BLOCK = 256
ROW_TILE_SUBLANES = D_MODEL // 128
ROUTER_PAD = 128
GATE_PAD = 128
EVEN_MAIN = 4096
EVEN_SCAN_COLS = 3072
NEG_BIG = -1e30

V7X_VMEM_LIMIT = 56 * 1024 * 1024


def _cparams(sem):
    return pltpu.CompilerParams(dimension_semantics=sem, vmem_limit_bytes=V7X_VMEM_LIMIT)


def _row_tile(n_lat_per_batch, n_ctx_total):
    for t in (512, 256, 128):
        if n_lat_per_batch % t == 0 and n_ctx_total % t == 0:
            return t
    raise ValueError("row counts must be multiples of 128")


def _split3(a):
    hi = a.astype(BF16)
    r1 = a - hi.astype(F32)
    mid = r1.astype(BF16)
    lo = (r1 - mid.astype(F32)).astype(BF16)
    return hi, mid, lo


def _dot_hi(a, b):
    ah, al, _ = _split3(a)
    bh, bl, _ = _split3(b)
    d = functools.partial(jnp.dot, preferred_element_type=F32)
    return d(ah, bh) + (d(ah, bl) + d(al, bh))


def _sigmoid(x):
    return 1.0 / (1.0 + jnp.exp(-x))


def _rms_rows(x):
    return x * lax.rsqrt(jnp.mean(x * x, axis=-1, keepdims=True) + EPS)


def _mod_kernel(c_ref, w_ref, b_ref, o_ref):
    c = c_ref[...]
    o_ref[...] = _dot_hi(c * _sigmoid(c), w_ref[...]) + b_ref[...]


def _modulation(cond8, w_mod, b_mod):
    depth = w_mod.shape[0]
    ncol = w_mod.shape[2]
    tn = 1024
    return pl.pallas_call(
        _mod_kernel,
        grid=(depth, ncol // tn),
        in_specs=[pl.BlockSpec((8, D_MODEL), lambda l, j: (0, 0)),
                  pl.BlockSpec((None, D_MODEL, tn), lambda l, j: (l, 0, j)),
                  pl.BlockSpec((None, 1, tn), lambda l, j: (l, 0, j))],
        out_specs=pl.BlockSpec((None, 8, tn), lambda l, j: (l, 0, j)),
        out_shape=jax.ShapeDtypeStruct((depth, 8, ncol), F32),
        compiler_params=_cparams(("arbitrary", "arbitrary")),
        name="modulation",
    )(cond8, w_mod, b_mod.reshape(depth, 1, ncol))


def _adaln_bf16(x, g, shift, scale):
    return ((_rms_rows(x) * g) * (1.0 + scale) + shift).astype(BF16)


def _lat_or_ctx(x_ref, ctx_ref, lat_tiles):
    return jnp.where(pl.program_id(0) < lat_tiles, x_ref[...], ctx_ref[...])


def _proj_even_kernel(x_ref, ctx_ref, mod_ref, g_ref, w_ref, wg_ref, cs_ref, sn_ref, proj_ref, gates_ref,
                      *, lat_tiles):
    u = _adaln_bf16(_lat_or_ctx(x_ref, ctx_ref, lat_tiles), g_ref[...], mod_ref[0:1, :], mod_ref[1:2, :])
    cs = cs_ref[...]
    sn = sn_ref[...]
    kscale = HEAD_DIM ** -0.5
    for gi in range(EVEN_MAIN // 512):
        acc = jnp.dot(u, w_ref[:, gi * 512:(gi + 1) * 512], preferred_element_type=F32)
        if gi in (3, 4):
            parts = []
            for hh in range(RET_HEADS):
                blk = acc[:, hh * HEAD_DIM:(hh + 1) * HEAD_DIM]
                parts.append(blk * cs + pltpu.roll(blk, HEAD_DIM // 2, 1) * sn)
            acc = jnp.concatenate(parts, axis=1)
        if gi in (1, 4):
            acc = acc * kscale
        proj_ref[:, gi * 512:(gi + 1) * 512] = acc.astype(BF16)
    gates_ref[...] = jnp.dot(u, wg_ref[...], preferred_element_type=F32)


def _combine(h, y0, y1, gk, gate):
    moe = gk[:, 0:1] * y0.astype(F32) + gk[:, 1:2] * y1.astype(F32)
    return h + gate * moe


def _proj_odd_kernel(h_ref, y0_ref, y1_ref, gk_ref, modp_ref, mod_ref, g_ref, w_ref, wvt_ref,
                     c_ref, s1_ref, s2_ref, hout_ref, qk_ref, vt_ref):
    h = _combine(h_ref[...], y0_ref[...], y1_ref[...], gk_ref[...], modp_ref[5:6, :])
    hout_ref[...] = h
    u = _adaln_bf16(h, g_ref[...], mod_ref[0:1, :], mod_ref[1:2, :])
    c = c_ref[...]
    s1 = s1_ref[...]
    s2 = s2_ref[...]
    qscale = DA_DHEAD ** -0.5 * math.log2(math.e)
    nq = DA_HEADS * 2 * DA_DHEAD
    for gi in range(2 * nq // 512):
        acc = jnp.dot(u, w_ref[:, gi * 512:(gi + 1) * 512], preferred_element_type=F32)
        parts = []
        for hh in range(512 // 128):
            blk = acc[:, hh * 128:(hh + 1) * 128]
            parts.append(blk * c + pltpu.roll(blk, 128 - 16, 1) * s1 + pltpu.roll(blk, 16, 1) * s2)
        acc = jnp.concatenate(parts, axis=1)
        if gi * 512 < nq:
            acc = acc * qscale
        qk_ref[:, gi * 512:(gi + 1) * 512] = acc.astype(BF16)
    vt = lax.dot_general(wvt_ref[...], u, (((1,), (1,)), ((), ())), preferred_element_type=F32)
    vt_ref[...] = vt.astype(BF16)


def _tile_maps(B, S, L, tr):
    lat_tiles = B * S // tr
    per_b = S // tr

    def mod_idx(i):
        return jnp.where(i < lat_tiles, i // per_b, B)

    def rot_idx(i):
        return jnp.where(i < lat_tiles, i % per_b, per_b + (i - lat_tiles))

    return mod_idx, rot_idx


def _lat_ctx_specs(tr, lat_tiles):
    return [pl.BlockSpec((tr, D_MODEL), lambda i: (jnp.minimum(i, lat_tiles - 1), 0)),
            pl.BlockSpec((tr, D_MODEL), lambda i: (jnp.maximum(i - lat_tiles, 0), 0))]


def _proj_even(x2d, ctx2d, mod, g, w_main, w_gates, cs, sn, dims):
    B, S, L, tr = dims
    n = x2d.shape[0] + ctx2d.shape[0]
    lat_tiles = x2d.shape[0] // tr
    mod_idx, rot_idx = _tile_maps(B, S, L, tr)
    const = lambda i: (0, 0)
    return pl.pallas_call(
        functools.partial(_proj_even_kernel, lat_tiles=lat_tiles),
        grid=(n // tr,),
        in_specs=_lat_ctx_specs(tr, lat_tiles) + [
                  pl.BlockSpec((None, 8, D_MODEL), lambda i: (mod_idx(i), 0, 0)),
                  pl.BlockSpec((1, D_MODEL), const),
                  pl.BlockSpec((D_MODEL, EVEN_MAIN), const),
                  pl.BlockSpec((D_MODEL, GATE_PAD), const),
                  pl.BlockSpec((tr, 128), lambda i: (rot_idx(i), 0)),
                  pl.BlockSpec((tr, 128), lambda i: (rot_idx(i), 0))],
        out_specs=[pl.BlockSpec((tr, EVEN_MAIN), lambda i: (i, 0)),
                   pl.BlockSpec((tr, GATE_PAD), lambda i: (i, 0))],
        out_shape=[jax.ShapeDtypeStruct((n, EVEN_MAIN), BF16),
                   jax.ShapeDtypeStruct((n, GATE_PAD), F32)],
        compiler_params=_cparams(("arbitrary",)),
        name="proj_even",
    )(x2d, ctx2d, mod, g, w_main, w_gates, cs, sn)


def _proj_odd(h, y0, y1, gk, modp, mod, g, w_qk, w_vt, c, s1, s2, dims):
    B, S, L, tr = dims
    n = h.shape[0]
    mod_idx, rot_idx = _tile_maps(B, S, L, tr)
    const = lambda i: (0, 0)
    row = lambda i: (i, 0)
    nqk = w_qk.shape[1]
    return pl.pallas_call(
        _proj_odd_kernel,
        grid=(n // tr,),
        in_specs=[pl.BlockSpec((tr, D_MODEL), row),
                  pl.BlockSpec((tr, D_MODEL), row),
                  pl.BlockSpec((tr, D_MODEL), row),
                  pl.BlockSpec((tr, ROUTER_PAD), row),
                  pl.BlockSpec((None, 8, D_MODEL), lambda i: (mod_idx(i), 0, 0)),
                  pl.BlockSpec((None, 8, D_MODEL), lambda i: (mod_idx(i), 0, 0)),
                  pl.BlockSpec((1, D_MODEL), const),
                  pl.BlockSpec((D_MODEL, nqk), const),
                  pl.BlockSpec((D_MODEL, D_MODEL), const),
                  pl.BlockSpec((tr, 128), lambda i: (rot_idx(i), 0)),
                  pl.BlockSpec((tr, 128), lambda i: (rot_idx(i), 0)),
                  pl.BlockSpec((tr, 128), lambda i: (rot_idx(i), 0))],
        out_specs=[pl.BlockSpec((tr, D_MODEL), row),
                   pl.BlockSpec((tr, nqk), row),
                   pl.BlockSpec((D_MODEL, tr), lambda i: (0, i))],
        out_shape=[jax.ShapeDtypeStruct((n, D_MODEL), F32),
                   jax.ShapeDtypeStruct((n, nqk), BF16),
                   jax.ShapeDtypeStruct((D_MODEL, n), BF16)],
        compiler_params=_cparams(("arbitrary",)),
        name="proj_odd",
    )(h, y0, y1, gk, modp, mod, g, w_qk, w_vt, c, s1, s2)


_NT = (((1,), (1,)), ((), ()))
_TN = (((0,), (0,)), ((), ()))


def _scan_kernel(ld_ref, xf_ref, xb_ref, gf_ref, gb_ref, bias_ref, hf_ref, hb_ref,
                 ct_ref, n_ref, m_ref, r_ref, intra_ref, qd_ref, kd_ref):
    t = pl.program_id(1)
    nh = ML_HEADS
    row_i = lax.broadcasted_iota(jnp.int32, (CHUNK, CHUNK), 0)
    col_i = lax.broadcasted_iota(jnp.int32, (CHUNK, CHUNK), 1)
    row_f = row_i.astype(F32)
    col_f = col_i.astype(F32)

    @pl.when(t == 0)
    def _init():
        ct_ref[...] = jnp.zeros_like(ct_ref)
        n_ref[...] = jnp.zeros_like(n_ref)
        m_ref[...] = jnp.zeros_like(m_ref)
        r_ref[...] = jnp.zeros_like(r_ref)
        for d in range(2):
            for hh in range(nh):
                hd = d * nh + hh
                lg = ld_ref[hd]
                if d == 0:
                    intra = jnp.where(col_i <= row_i, jnp.exp(lg * (row_f - col_f)), 0.0)
                    qd = jnp.exp(lg * (row_f + 1.0))
                    kd = jnp.exp(lg * (CHUNK - 1.0 - row_f))
                else:
                    intra = jnp.where(col_i >= row_i, jnp.exp(lg * (col_f - row_f)), 0.0)
                    qd = jnp.exp(lg * (CHUNK - row_f))
                    kd = jnp.exp(lg * row_f)
                intra_ref[hd] = intra
                qd_ref[hd] = qd
                kd_ref[hd] = kd

    bias = bias_ref[...]
    dot = functools.partial(jnp.dot, preferred_element_type=F32)
    dg = functools.partial(lax.dot_general, preferred_element_type=F32)
    for d, (x_ref, g_ref, o_ref) in enumerate(((xf_ref, gf_ref, hf_ref), (xb_ref, gb_ref, hb_ref))):
        seen = (col_i <= row_i) if d == 0 else (col_i >= row_i)
        G = g_ref[...] + bias
        LF = jnp.minimum(G, 0.0) - jnp.log1p(jnp.exp(-jnp.abs(G)))
        tri = jnp.where(seen, 1.0, 0.0).astype(BF16)
        l_hi, l_mid, l_lo = _split3(LF)
        Bc = dot(tri, l_hi) + (dot(tri, l_mid) + dot(tri, l_lo))
        GT = G.T
        BT = Bc.T
        last = CHUNK - 1 if d == 0 else 0
        for hh in range(nh):
            hd = d * nh + hh
            c0 = hh * HEAD_DIM
            q = x_ref[:, c0:c0 + HEAD_DIM]
            k = x_ref[:, 512 + c0:512 + c0 + HEAD_DIM]
            v = x_ref[:, 1024 + c0:1024 + c0 + HEAD_DIM]
            ic_col = G[:, hd:hd + 1]
            ic_row = GT[hd:hd + 1, :]
            b_col = Bc[:, 8 + hd:9 + hd]
            b_row = BT[8 + hd:9 + hd, :]
            m0 = m_ref[hd][:, 0:1]
            dlog = jnp.where(seen, b_col - b_row + ic_row, NEG_BIG)
            inter = b_col + m0
            m_t = jnp.maximum(inter, jnp.max(dlog, axis=1, keepdims=True))
            s = dg(q, k, _NT) * jnp.exp(dlog - m_t)
            w_inter = jnp.exp(inter - m_t)
            ct = ct_ref[hd]
            n_row = n_ref[hd]
            numer = dot(s.astype(BF16), v) + w_inter * dot(q, ct.astype(BF16))
            denom = (jnp.sum(s, axis=1, keepdims=True)
                     + w_inter * jnp.sum(q.astype(F32) * n_row, axis=1, keepdims=True))
            hval = numer / jnp.maximum(jnp.abs(denom), jnp.exp(-m_t))
            o_ref[:, c0:c0 + HEAD_DIM] = hval.astype(BF16)
            b_last = b_col[last:last + 1, :]
            w_log_row = b_last - b_row + ic_row
            m_new = jnp.maximum(b_last + m0, jnp.max(w_log_row, axis=1, keepdims=True))
            decay = jnp.exp(b_last + m0 - m_new)
            w_col = jnp.exp(b_last - b_col + ic_col - m_new)
            w_row = jnp.exp(w_log_row - m_new)
            vw = (v.astype(F32) * w_col).astype(BF16)
            ct_ref[hd] = decay * ct + dg(k, vw, _TN)
            w8 = jnp.broadcast_to(w_row, (8, CHUNK)).astype(BF16)
            n_ref[hd] = decay * n_row + dot(w8, k)[0:1, :]
            m_ref[hd] = jnp.broadcast_to(m_new, (1, HEAD_DIM))
            rq = x_ref[:, 1536 + c0:1536 + c0 + HEAD_DIM]
            rk = x_ref[:, 2048 + c0:2048 + c0 + HEAD_DIM]
            rv = x_ref[:, 2560 + c0:2560 + c0 + HEAD_DIM]
            rs = dg(rq, rk, _NT) * intra_ref[hd]
            R = r_ref[hd]
            o = dot(rs.astype(BF16), rv) + qd_ref[hd] * dot(rq, R.astype(BF16))
            o_ref[:, 512 + c0:512 + c0 + HEAD_DIM] = o.astype(BF16)
            kdk = (rk.astype(F32) * kd_ref[hd]).astype(BF16)
            cdec = jnp.exp(ld_ref[hd] * jnp.full((1, HEAD_DIM), float(CHUNK), F32))
            r_ref[hd] = cdec * R + dg(kdk, rv, _TN)


def _scan(ret_ld8, proj, gates, bias, B, S, L):
    n = proj.shape[0]
    nlb, ncb = S // CHUNK, L // CHUNK
    nc = nlb + ncb

    def fwd(b, t):
        return jnp.where(t < ncb, B * nlb + b * ncb + t, b * nlb + t - ncb)

    def bwd(b, t):
        return jnp.where(t < ncb, B * nlb + b * ncb + (ncb - 1 - t), b * nlb + (nlb - 1 - (t - ncb)))

    state = pltpu.VMEM((2 * ML_HEADS, HEAD_DIM, HEAD_DIM), F32)
    vec = pltpu.VMEM((2 * ML_HEADS, 1, HEAD_DIM), F32)
    grid_spec = pltpu.PrefetchScalarGridSpec(
        num_scalar_prefetch=1,
        grid=(B, nc),
        in_specs=[pl.BlockSpec((CHUNK, EVEN_SCAN_COLS), lambda b, t, ld: (fwd(b, t), 0)),
                  pl.BlockSpec((CHUNK, EVEN_SCAN_COLS), lambda b, t, ld: (bwd(b, t), 0)),
                  pl.BlockSpec((CHUNK, GATE_PAD), lambda b, t, ld: (fwd(b, t), 0)),
                  pl.BlockSpec((CHUNK, GATE_PAD), lambda b, t, ld: (bwd(b, t), 0)),
                  pl.BlockSpec((1, GATE_PAD), lambda b, t, ld: (0, 0))],
        out_specs=[pl.BlockSpec((CHUNK, D_MODEL), lambda b, t, ld: (fwd(b, t), 0)),
                   pl.BlockSpec((CHUNK, D_MODEL), lambda b, t, ld: (bwd(b, t), 0))],
        scratch_shapes=[state, vec, vec, state, state, state, state],
    )
    return pl.pallas_call(
        _scan_kernel,
        grid_spec=grid_spec,
        out_shape=[jax.ShapeDtypeStruct((n, D_MODEL), BF16),
                   jax.ShapeDtypeStruct((n, D_MODEL), BF16)],
        compiler_params=_cparams(("arbitrary", "arbitrary")),
        name="scan",
    )(ret_ld8, proj, proj, gates, gates, bias)


def _post_tail(merged, h_in, mod_ref, g2_ref, wout_ref, wr_ref, br_ref, hout_ref, v_ref, lg_ref):
    y = jnp.dot(merged, wout_ref[...], preferred_element_type=F32)
    h = h_in + mod_ref[2:3, :] * y
    hout_ref[...] = h
    v = (_rms_rows(h) * g2_ref[...]) * (1.0 + mod_ref[4:5, :]) + mod_ref[3:4, :]
    for s in range(ROW_TILE_SUBLANES):
        v_ref[pl.ds(s, v.shape[0], stride=ROW_TILE_SUBLANES), :] = v[:, s * 128:(s + 1) * 128]
    lg_ref[...] = _dot_hi(v, wr_ref[...]) + br_ref[...]


def _head_ln(x, g):
    mu = jnp.mean(x, axis=-1, keepdims=True)
    xc = x - mu
    var = jnp.mean(xc * xc, axis=-1, keepdims=True)
    return xc * lax.rsqrt(var + EPS) * g


def _post_even_kernel(hf_ref, hb_ref, og_ref, mlg_ref, retg_ref, x_ref, ctx_ref, mod_ref, g2_ref, wout_ref,
                      wr_ref, br_ref, hout_ref, v_ref, lg_ref, *, lat_tiles):
    parts = []
    for hh in range(ML_HEADS):
        sl = slice(hh * HEAD_DIM, (hh + 1) * HEAD_DIM)
        ml = hf_ref[:, sl].astype(F32) + hb_ref[:, sl].astype(F32)
        parts.append(_head_ln(_sigmoid(og_ref[:, sl].astype(F32)) * ml, mlg_ref[:, sl]))
    for hh in range(RET_HEADS):
        sl = slice(512 + hh * HEAD_DIM, 512 + (hh + 1) * HEAD_DIM)
        ret = hf_ref[:, sl].astype(F32) + hb_ref[:, sl].astype(F32)
        rg = og_ref[:, sl].astype(F32)
        parts.append((rg * _sigmoid(rg)) * _head_ln(ret, retg_ref[:, hh * HEAD_DIM:(hh + 1) * HEAD_DIM]))
    merged = jnp.concatenate(parts, axis=1).astype(BF16)
    h_in = _lat_or_ctx(x_ref, ctx_ref, lat_tiles)
    _post_tail(merged, h_in, mod_ref, g2_ref, wout_ref, wr_ref, br_ref, hout_ref, v_ref, lg_ref)


def _post_odd_kernel(att_ref, h_ref, mod_ref, g2_ref, wout_ref, wr_ref, br_ref, hout_ref, v_ref, lg_ref):
    _post_tail(att_ref[...], h_ref[...], mod_ref, g2_ref, wout_ref, wr_ref, br_ref, hout_ref, v_ref, lg_ref)


def _post_specs(n, tr, mod_idx):
    const = lambda i: (0, 0)
    row = lambda i: (i, 0)
    tail_in = [pl.BlockSpec((tr, D_MODEL), row),
               pl.BlockSpec((None, 8, D_MODEL), lambda i: (mod_idx(i), 0, 0)),
               pl.BlockSpec((1, D_MODEL), const),
               pl.BlockSpec((D_MODEL, D_MODEL), const),
               pl.BlockSpec((D_MODEL, ROUTER_PAD), const),
               pl.BlockSpec((1, ROUTER_PAD), const)]
    out_specs = [pl.BlockSpec((tr, D_MODEL), row),
                 pl.BlockSpec((tr * ROW_TILE_SUBLANES, 128), row),
                 pl.BlockSpec((tr, ROUTER_PAD), row)]
    out_shape = [jax.ShapeDtypeStruct((n, D_MODEL), F32),
                 jax.ShapeDtypeStruct((n * ROW_TILE_SUBLANES, 128), F32),
                 jax.ShapeDtypeStruct((n, ROUTER_PAD), F32)]
    return tail_in, out_specs, out_shape


def _post_even(hf, hb, proj, mlg, retg, x2d, ctx2d, mod, g2, wout, wr, br, dims):
    B, S, L, tr = dims
    n = hf.shape[0]
    mod_idx, _ = _tile_maps(B, S, L, tr)
    tail_in, out_specs, out_shape = _post_specs(n, tr, mod_idx)
    lat_tiles = x2d.shape[0] // tr
    row = lambda i: (i, 0)
    const = lambda i: (0, 0)
    return pl.pallas_call(
        functools.partial(_post_even_kernel, lat_tiles=lat_tiles),
        grid=(n // tr,),
        in_specs=[pl.BlockSpec((tr, D_MODEL), row),
                  pl.BlockSpec((tr, D_MODEL), row),
                  pl.BlockSpec((tr, D_MODEL), lambda i: (i, EVEN_SCAN_COLS // D_MODEL)),
                  pl.BlockSpec((1, 512), const),
                  pl.BlockSpec((1, 512), const)] + _lat_ctx_specs(tr, lat_tiles) + tail_in[1:],
        out_specs=out_specs,
        out_shape=out_shape,
        compiler_params=_cparams(("arbitrary",)),
        name="post_even",
    )(hf, hb, proj, mlg, retg, x2d, ctx2d, mod, g2, wout, wr, br)


def _post_odd(att, h, mod, g2, wout, wr, br, dims):
    B, S, L, tr = dims
    n = att.shape[0]
    mod_idx, _ = _tile_maps(B, S, L, tr)
    tail_in, out_specs, out_shape = _post_specs(n, tr, mod_idx)
    return pl.pallas_call(
        _post_odd_kernel,
        grid=(n // tr,),
        in_specs=[pl.BlockSpec((tr, D_MODEL), lambda i: (i, 0))] + tail_in,
        out_specs=out_specs,
        out_shape=out_shape,
        compiler_params=_cparams(("arbitrary",)),
        name="post_odd",
    )(att, h, mod, g2, wout, wr, br)


SUM_ROWS = 16
ATTN_TILE = 1024


def _attn_kernel(q_ref, kl_ref, kc_ref, vl_ref, vc_ref, lam_ref, g_ref, o_ref,
                 acc_ref, m_ref, cmax_ref, qm_ref, sa_ref, sb_ref, sc_ref, *, tk, lam_init):
    n_lat = kl_ref.shape[0] // tk
    q = q_ref[...]
    lane = lax.broadcasted_iota(jnp.int32, q.shape, 1)
    zero = jnp.zeros_like(q)
    qm_ref[0] = jnp.where(lane < DA_DHEAD, q, zero)
    qm_ref[1] = jnp.where(lane >= DA_DHEAD, q, zero)
    acc_ref[...] = jnp.zeros_like(acc_ref)
    m_ref[...] = jnp.full(m_ref.shape, NEG_BIG, F32)

    def lat_k(c):
        return kl_ref[pl.ds(pl.multiple_of(c * tk, tk), tk), :]

    def lat_v(c):
        return vl_ref[:, pl.ds(pl.multiple_of(c * tk, tk), tk)]

    def scores(dst_ref, slot, kc):
        for mi in range(2):
            st = lax.dot_general(kc, qm_ref[mi], _NT, preferred_element_type=F32)
            dst_ref[mi] = st
            cmax_ref[slot, mi] = jnp.max(st, axis=0, keepdims=True)

    def absorb(src_ref, slot, vtc):
        vext = jnp.concatenate([vtc, jnp.ones((SUM_ROWS, vtc.shape[1]), BF16)], axis=0)
        for mi in range(2):
            m_old = m_ref[mi]
            m_new = jnp.maximum(m_old, cmax_ref[slot, mi])
            p = jnp.exp2(src_ref[mi] - m_new).astype(BF16)
            acc_ref[mi] = jnp.exp2(m_old - m_new) * acc_ref[mi] + jnp.dot(vext, p, preferred_element_type=F32)
            m_ref[mi] = m_new

    scores(sa_ref, 0, lat_k(0))

    def body(j, carry):
        c = 2 * j
        scores(sb_ref, 1, lat_k(c + 1))
        absorb(sa_ref, 0, lat_v(c))
        scores(sa_ref, 0, lat_k(c + 2))
        absorb(sb_ref, 1, lat_v(c + 1))
        return carry

    lax.fori_loop(0, n_lat // 2 - 1, body, 0)
    scores(sb_ref, 1, lat_k(n_lat - 1))
    absorb(sa_ref, 0, lat_v(n_lat - 2))
    scores(sc_ref, 2, kc_ref[...])
    absorb(sb_ref, 1, lat_v(n_lat - 1))
    absorb(sc_ref, 2, vc_ref[...])

    lp = lam_ref[...]
    lam = (jnp.exp(jnp.sum(lp[0:1, :] * lp[1:2, :], axis=1, keepdims=True))
           - jnp.exp(jnp.sum(lp[2:3, :] * lp[3:4, :], axis=1, keepdims=True)) + lam_init)
    o0 = acc_ref[0, 0:128, :] / acc_ref[0, 128:129, :]
    o1 = acc_ref[1, 0:128, :] / acc_ref[1, 128:129, :]
    ot = o0 - lam * o1
    ot = ot * lax.rsqrt(jnp.mean(ot * ot, axis=0, keepdims=True) + EPS)
    o = ot.T * g_ref[...] * (1.0 - lam_init)
    o_ref[...] = o.astype(BF16)


def _attention(qk, vt, lam_p, norm_g, lam_init, B, S, L):
    tq = ATTN_TILE
    tk = ATTN_TILE
    assert S % tq == 0 and S % (2 * tk) == 0
    nq = S // tq
    nh = DA_HEADS
    kcol = nh
    ctx0 = B * S // L
    kern = functools.partial(_attn_kernel, tk=tk, lam_init=lam_init)
    return pl.pallas_call(
        kern,
        grid=(B, nh, nq),
        in_specs=[pl.BlockSpec((tq, 128), lambda b, h, i: (b * nq + i, h)),
                  pl.BlockSpec((S, 128), lambda b, h, i: (b, kcol + h)),
                  pl.BlockSpec((L, 128), lambda b, h, i: (ctx0 + b, kcol + h)),
                  pl.BlockSpec((128, S), lambda b, h, i: (h, b)),
                  pl.BlockSpec((128, L), lambda b, h, i: (h, ctx0 + b)),
                  pl.BlockSpec((4, DA_DHEAD), lambda b, h, i: (0, 0)),
                  pl.BlockSpec((None, 1, 128), lambda b, h, i: (h, 0, 0))],
        out_specs=pl.BlockSpec((tq, 128), lambda b, h, i: (b * nq + i, h)),
        out_shape=jax.ShapeDtypeStruct((B * S, D_MODEL), BF16),
        scratch_shapes=[pltpu.VMEM((2, 128 + SUM_ROWS, tq), F32),
                        pltpu.VMEM((2, 1, tq), F32),
                        pltpu.VMEM((3, 2, 1, tq), F32),
                        pltpu.VMEM((2, tq, 128), BF16),
                        pltpu.VMEM((2, tk, tq), F32),
                        pltpu.VMEM((2, tk, tq), F32),
                        pltpu.VMEM((2, L, tq), F32)],
        compiler_params=_cparams(("arbitrary", "arbitrary", "arbitrary")),
        name="diff_attention",
    )(qk, qk, qk, vt, vt, lam_p, norm_g.reshape(nh, 1, 128))


def _expert_kernel(be_ref, nu_ref, tok_ref, tokn_ref, v_hbm, w1_ref, w3_ref, w2_ref, y_ref,
                   w1b_ref, w3b_ref, w2b_ref, xbuf_ref, sem_ref):
    i = pl.program_id(0)
    n_used = nu_ref[0]
    tile = ROW_TILE_SUBLANES

    def row_copy(idx_ref, r, slot):
        src = v_hbm.at[pl.ds(pl.multiple_of(idx_ref[0, r], tile), tile), :]
        return pltpu.make_async_copy(src, xbuf_ref.at[slot, pl.ds(r * tile, tile), :], sem_ref.at[slot])

    def start_gather(idx_ref, slot):
        for r in range(MOE_BLOCK):
            row_copy(idx_ref, r, slot).start(priority=r % 2)

    def wait_gather(slot):
        pltpu.make_async_copy(v_hbm.at[pl.ds(0, MOE_BLOCK * tile), :], xbuf_ref.at[slot], sem_ref.at[slot]).wait()

    @pl.when(i == 0)
    def _first_block():
        start_gather(tok_ref, 0)

    @pl.when((i == 0) | (be_ref[i] != be_ref[jnp.maximum(i - 1, 0)]))
    def _new_expert():
        w1b_ref[...] = w1_ref[...].astype(BF16)
        w3b_ref[...] = w3_ref[...].astype(BF16)
        w2b_ref[...] = w2_ref[...].astype(BF16)

    for slot in range(2):
        @pl.when((i % 2 == slot) & (i < n_used))
        def _compute():
            @pl.when(i + 1 < n_used)
            def _next_block():
                start_gather(tokn_ref, 1 - slot)

            wait_gather(slot)
            x = jnp.concatenate([xbuf_ref[slot, pl.ds(s, MOE_BLOCK, stride=tile), :] for s in range(tile)],
                                axis=1).astype(BF16)
            a = jnp.dot(x, w1b_ref[...], preferred_element_type=F32)
            b = jnp.dot(x, w3b_ref[...], preferred_element_type=F32)
            hm = ((a * _sigmoid(a)) * b).astype(BF16)
            y_ref[...] = jnp.dot(hm, w2b_ref[...], preferred_element_type=F32).astype(BF16)

    @pl.when(i >= n_used)
    def _skip():
        y_ref[...] = jnp.zeros_like(y_ref)


def _experts(blk_expert, n_used, buf_row, v_tiles, w1, w3, w2, layer):
    nblk = blk_expert.shape[0]
    tok3 = buf_row.reshape(nblk, 1, MOE_BLOCK)
    wmap = lambda i, be, nu: (layer, be[i], 0, 0)
    grid_spec = pltpu.PrefetchScalarGridSpec(
        num_scalar_prefetch=2,
        grid=(nblk,),
        in_specs=[pl.BlockSpec((None, 1, MOE_BLOCK), lambda i, be, nu: (i, 0, 0), memory_space=pltpu.SMEM),
                  pl.BlockSpec((None, 1, MOE_BLOCK), lambda i, be, nu: (jnp.minimum(i + 1, nblk - 1), 0, 0),
                               memory_space=pltpu.SMEM),
                  pl.BlockSpec(memory_space=pl.ANY),
                  pl.BlockSpec((None, None, D_MODEL, D_EXPERT), wmap),
                  pl.BlockSpec((None, None, D_MODEL, D_EXPERT), wmap),
                  pl.BlockSpec((None, None, D_EXPERT, D_MODEL), wmap)],
        out_specs=pl.BlockSpec((MOE_BLOCK, D_MODEL), lambda i, be, nu: (i, 0)),
        scratch_shapes=[pltpu.VMEM((D_MODEL, D_EXPERT), BF16),
                        pltpu.VMEM((D_MODEL, D_EXPERT), BF16),
                        pltpu.VMEM((D_EXPERT, D_MODEL), BF16),
                        pltpu.VMEM((2, MOE_BLOCK * ROW_TILE_SUBLANES, 128), F32),
                        pltpu.SemaphoreType.DMA((2,))],
    )
    return pl.pallas_call(
        _expert_kernel,
        grid_spec=grid_spec,
        out_shape=jax.ShapeDtypeStruct((nblk * MOE_BLOCK, D_MODEL), BF16),
        compiler_params=_cparams(("arbitrary",)),
        name="experts",
    )(blk_expert, n_used, tok3, tok3, v_tiles, w1, w3, w2)


def _route_kernel(lg_ref, idx_ref, gate_ref, cnt_ref, base_ref):
    tr = lg_ref.shape[0]

    @pl.when(pl.program_id(0) == 0)
    def _init():
        base_ref[...] = jnp.zeros_like(base_ref)

    lg = lg_ref[...]
    lane = lax.broadcasted_iota(jnp.int32, lg.shape, 1)
    first_arg = lambda x, mx: jnp.min(jnp.where(x == mx, lane, 128), axis=1, keepdims=True)
    is_g = lane < N_GROUPS
    gl = jnp.where(is_g, lg, NEG_BIG)
    gmax = jnp.max(gl, axis=1, keepdims=True)
    pg_top = 1.0 / jnp.sum(jnp.where(is_g, jnp.exp(gl - gmax), 0.0), axis=1, keepdims=True)
    grp = first_arg(gl, gmax)
    e_lo = N_GROUPS + grp * EXPERTS_PER_GROUP
    el = jnp.where((lane >= e_lo) & (lane < e_lo + EXPERTS_PER_GROUP), lg, NEG_BIG)
    m1 = jnp.max(el, axis=1, keepdims=True)
    i1 = first_arg(el, m1)
    el2 = jnp.where(lane == i1, NEG_BIG, el)
    m2 = jnp.max(el2, axis=1, keepdims=True)
    i2 = first_arg(el2, m2)
    p2 = jnp.exp(m2 - m1)
    g1 = pg_top / (1.0 + p2)
    g2 = pg_top * p2 / (1.0 + p2)

    hit1 = lane == i1
    hit2 = lane == i2
    onehot = jnp.where(hit1 | hit2, 1.0, 0.0)
    r_i = lax.broadcasted_iota(jnp.int32, (tr, tr), 0)
    c_i = lax.broadcasted_iota(jnp.int32, (tr, tr), 1)
    earlier = jnp.where(c_i < r_i, 1.0, 0.0).astype(BF16)
    before = jnp.dot(earlier, onehot.astype(BF16), preferred_element_type=F32) + base_ref[...]
    rank1 = jnp.sum(jnp.where(hit1, before, 0.0), axis=1, keepdims=True).astype(jnp.int32)
    rank2 = jnp.sum(jnp.where(hit2, before, 0.0), axis=1, keepdims=True).astype(jnp.int32)
    base = base_ref[...] + jnp.sum(onehot, axis=0, keepdims=True)
    base_ref[...] = base
    cnt_ref[...] = jnp.broadcast_to(base, cnt_ref.shape)

    idx_ref[...] = jnp.where(lane == 0, i1 - N_GROUPS,
                             jnp.where(lane == 1, i2 - N_GROUPS,
                                       jnp.where(lane == 2, rank1, jnp.where(lane == 3, rank2, 0))))
    gate_ref[...] = jnp.where(lane == 0, g1, jnp.where(lane == 1, g2, 0.0))


def _route(logits):
    n = logits.shape[0]
    tr = 512 if n % 512 == 0 else 256
    idx, gate, cnt = pl.pallas_call(
        _route_kernel,
        grid=(n // tr,),
        in_specs=[pl.BlockSpec((tr, ROUTER_PAD), lambda i: (i, 0))],
        out_specs=[pl.BlockSpec((tr, ROUTER_PAD), lambda i: (i, 0)),
                   pl.BlockSpec((tr, ROUTER_PAD), lambda i: (i, 0)),
                   pl.BlockSpec((8, ROUTER_PAD), lambda i: (0, 0))],
        out_shape=[jax.ShapeDtypeStruct((n, ROUTER_PAD), jnp.int32),
                   jax.ShapeDtypeStruct((n, ROUTER_PAD), F32),
                   jax.ShapeDtypeStruct((8, ROUTER_PAD), F32)],
        scratch_shapes=[pltpu.VMEM((1, ROUTER_PAD), F32)],
        compiler_params=_cparams(("arbitrary",)),
        name="route",
    )(logits)
    expert = idx[:, 0:TOP_K]
    rank = idx[:, TOP_K:2 * TOP_K]
    counts = cnt[0, N_GROUPS:N_GROUPS + N_EXPERTS].astype(jnp.int32)
    a = n * TOP_K
    nblk = -(-a // MOE_BLOCK) + N_EXPERTS
    padded = (counts + MOE_BLOCK - 1) // MOE_BLOCK * MOE_BLOCK
    pend = jnp.cumsum(padded)
    start = pend - padded
    hit = expert[:, :, None] == jnp.arange(N_EXPERTS, dtype=jnp.int32)[None, None, :]
    dest = jnp.sum(jnp.where(hit, start[None, None, :], 0), axis=-1) + rank
    tok = jnp.broadcast_to(jnp.arange(n, dtype=jnp.int32)[:, None], (n, TOP_K))
    buf_tok = jnp.zeros((nblk * MOE_BLOCK,), jnp.int32).at[dest.reshape(-1)].set(
        tok.reshape(-1), mode="promise_in_bounds", unique_indices=True)
    n_used = (pend[-1] // MOE_BLOCK).astype(jnp.int32)
    blk_row = jnp.minimum(jnp.arange(nblk, dtype=jnp.int32), n_used - 1) * MOE_BLOCK
    blk_expert = jnp.sum((pend[None, :] <= blk_row[:, None]).astype(jnp.int32), axis=1)
    blk_expert = jnp.clip(blk_expert, 0, N_EXPERTS - 1)
    return buf_tok, blk_expert.astype(jnp.int32), n_used.reshape(1), dest, gate


def _moe(v, logits, w1, w3, w2, layer):
    buf_tok, blk_expert, n_used, pos, gate = _route(logits)
    yb = _experts(blk_expert, n_used, buf_tok * ROW_TILE_SUBLANES, v, w1, w3, w2, layer)
    y0 = yb.at[pos[:, 0]].get(mode="promise_in_bounds")
    y1 = yb.at[pos[:, 1]].get(mode="promise_in_bounds")
    return y0, y1, gate


def _final_kernel(h_ref, y0_ref, y1_ref, gk_ref, modp_ref, g_ref, o_ref):
    h = _combine(h_ref[...], y0_ref[...], y1_ref[...], gk_ref[...], modp_ref[5:6, :])
    o_ref[...] = _rms_rows(h) * g_ref[...]


def _final(h, y0, y1, gk, modp, g, dims):
    B, S, L, tr = dims
    n = h.shape[0]
    mod_idx, _ = _tile_maps(B, S, L, tr)
    row = lambda i: (i, 0)
    return pl.pallas_call(
        _final_kernel,
        grid=(n // tr,),
        in_specs=[pl.BlockSpec((tr, D_MODEL), row),
                  pl.BlockSpec((tr, D_MODEL), row),
                  pl.BlockSpec((tr, D_MODEL), row),
                  pl.BlockSpec((tr, ROUTER_PAD), row),
                  pl.BlockSpec((None, 8, D_MODEL), lambda i: (mod_idx(i), 0, 0)),
                  pl.BlockSpec((1, D_MODEL), lambda i: (0, 0))],
        out_specs=pl.BlockSpec((tr, D_MODEL), row),
        out_shape=jax.ShapeDtypeStruct((n, D_MODEL), F32),
        compiler_params=_cparams(("arbitrary",)),
        name="final_norm",
    )(h, y0, y1, gk, modp, g)


def _rotary_tables_even(S, L, B):
    nf = HEAD_DIM // 2
    inv = ROPE_BASE ** (-jnp.arange(nf, dtype=F32) / nf)
    pos = jnp.concatenate([L + jnp.arange(S), jnp.tile(jnp.arange(L), B)]).astype(F32)
    ang = pos[:, None] * inv[None, :]
    cos, sin = jnp.cos(ang), jnp.sin(ang)
    return jnp.concatenate([cos, cos], 1), jnp.concatenate([-sin, sin], 1)


def _rotary_tables_odd(S, L, B):
    nf = DA_DHEAD // 4
    inv = ROPE_BASE ** (-jnp.arange(nf, dtype=F32) / nf)
    t = jnp.arange(S)
    rows, cols = (t // GRID_W).astype(F32), (t % GRID_W).astype(F32)
    j = np.arange(128)
    f_idx = j % nf
    use_col = (j % DA_DHEAD) >= DA_DHEAD // 2
    first = (j % (2 * nf)) < nf
    ang = jnp.where(use_col[None, :], cols[:, None], rows[:, None]) * inv[f_idx][None, :]
    cos, sin = jnp.cos(ang), jnp.sin(ang)
    c_lat = cos
    s1_lat = jnp.where(first[None, :], -sin, 0.0)
    s2_lat = jnp.where(first[None, :], 0.0, sin)
    nctx = B * L
    c = jnp.concatenate([c_lat, jnp.ones((nctx, 128), F32)])
    s1 = jnp.concatenate([s1_lat, jnp.zeros((nctx, 128), F32)])
    s2 = jnp.concatenate([s2_lat, jnp.zeros((nctx, 128), F32)])
    return c, s1, s2


def kernel(x, c, ctx, c_ctx, w_mod, b_mod, norm1_g, norm2_g, w_in_even, ml_gate_b, ml_norm_g, ret_log_decay, ret_norm_g, w_out_even, w_in_odd, da_lambda, da_norm_g, w_out_odd, router_g_w, router_g_b, router_e_w, router_e_b, w1, w3, w2, final_norm_g):
    B, S, D = x.shape
    L = ctx.shape[1]
    depth = w_mod.shape[0]
    assert D == D_MODEL and depth == 2 and S % CHUNK == 0 and L % CHUNK == 0 and S % L == 0
    n_lat, n_ctx = B * S, B * L
    tr = _row_tile(S, n_ctx)
    dims = (B, S, L, tr)

    x2d, ctx2d = x.reshape(n_lat, D), ctx.reshape(n_ctx, D)

    cond8 = jnp.zeros((8, D), F32).at[:B].set(c).at[B].set(c_ctx)
    mod = jnp.pad(_modulation(cond8, w_mod, b_mod).reshape(depth, 8, 6, D), ((0, 0), (0, 0), (0, 2), (0, 0)))

    def router_w(l):
        wr = jnp.zeros((D, ROUTER_PAD), F32)
        wr = wr.at[:, :N_GROUPS].set(router_g_w[l]).at[:, N_GROUPS:N_GROUPS + N_EXPERTS].set(router_e_w[l])
        br = jnp.zeros((1, ROUTER_PAD), F32)
        br = br.at[0, :N_GROUPS].set(router_g_b[l]).at[0, N_GROUPS:N_GROUPS + N_EXPERTS].set(router_e_b[l])
        return wr, br

    wi = w_in_even[0]
    mq, mk, mv, mo, mi, mf, rq, rk, rv, rg = jnp.split(wi, np.cumsum(
        [512, 512, 512, 512, 8, 8, 512, 512, 512])[:9].tolist(), axis=1)
    w_main = jnp.concatenate([mq, mk, mv, rq, rk, rv, mo, rg], 1).astype(BF16)
    w_gates = jnp.pad(jnp.concatenate([mi, mf], 1), ((0, 0), (0, GATE_PAD - 16))).astype(BF16)
    gate_bias = jnp.pad(jnp.concatenate([ml_gate_b[0][:, 0].reshape(-1), ml_gate_b[0][:, 1].reshape(-1)]),
                        (0, GATE_PAD - 16)).reshape(1, GATE_PAD)
    cs, sn = _rotary_tables_even(S, L, B)
    proj, gates = _proj_even(x2d, ctx2d, mod[0], norm1_g[0].reshape(1, D), w_main, w_gates, cs, sn, dims)
    hf, hb = _scan(ret_log_decay[0].reshape(-1), proj, gates, gate_bias, B, S, L)
    wr, br = router_w(0)
    h, v, logits = _post_even(hf, hb, proj, ml_norm_g[0].reshape(1, -1), ret_norm_g[0].reshape(1, -1),
                              x2d, ctx2d, mod[0], norm2_g[0].reshape(1, D), w_out_even[0].astype(BF16), wr, br, dims)
    y0, y1, gk = _moe(v, logits, w1, w3, w2, 0)

    lam_init = 0.8 - 0.6 * math.exp(-0.3 * 1)
    nqk = 4 * DA_HEADS * DA_DHEAD
    w_qk = w_in_odd[0][:, :nqk].astype(BF16)
    w_vt = w_in_odd[0][:, nqk:].T.astype(BF16)
    c2, s1, s2 = _rotary_tables_odd(S, L, B)
    h, qk, vt = _proj_odd(h, y0, y1, gk, mod[0], mod[1], norm1_g[1].reshape(1, D), w_qk, w_vt, c2, s1, s2, dims)
    att = _attention(qk, vt, da_lambda[0], da_norm_g[0], lam_init, B, S, L)
    wr, br = router_w(1)
    h_lat, v, logits = _post_odd(att, h, mod[1], norm2_g[1].reshape(1, D), w_out_odd[0].astype(BF16), wr, br, dims)
    y0, y1, gk = _moe(v, logits, w1, w3, w2, 1)
    out = _final(h_lat, y0, y1, gk, mod[1], final_norm_g.reshape(1, D), dims)
    return out.reshape(B, S, D)
```

```python
import functools
import math

import numpy as np
import jax
import jax.numpy as jnp
from jax import lax
from jax.experimental import pallas as pl
from jax.experimental.pallas import tpu as pltpu

F32 = jnp.float32
BF16 = jnp.bfloat16

D_MODEL = 1024
GRID_W = 64
CHUNK = 128
EPS = 1e-6
ROPE_BASE = 10000.0
ML_HEADS = 4
RET_HEADS = 4
HEAD_DIM = 128
DA_HEADS = 8
DA_DHEAD = 64
N_GROUPS = 4
EXPERTS_PER_GROUP = 8
N_EXPERTS = N_GROUPS * EXPERTS_PER_GROUP
TOP_K = 2
D_EXPERT = D_MODEL // 2
MOE_BLOCK = 256
ROW_TILE_SUBLANES = D_MODEL // 128
ROUTER_PAD = 128
GATE_PAD = 128
EVEN_MAIN = 4096
EVEN_SCAN_COLS = 3072
NEG_BIG = -1e30

V7X_VMEM_LIMIT = 56 * 1024 * 1024


def _cparams(sem):
    return pltpu.CompilerParams(dimension_semantics=sem, vmem_limit_bytes=V7X_VMEM_LIMIT)


def _row_tile(n_lat_per_batch, n_ctx_total):
    for t in (512, 256, 128):
        if n_lat_per_batch % t == 0 and n_ctx_total % t == 0:
            return t
    raise ValueError("row counts must be multiples of 128")


def _split3(a):
    hi = a.astype(BF16)
    r1 = a - hi.astype(F32)
    mid = r1.astype(BF16)
    lo = (r1 - mid.astype(F32)).astype(BF16)
    return hi, mid, lo


def _dot_hi(a, b):
    ah, al, _ = _split3(a)
    bh, bl, _ = _split3(b)
    d = functools.partial(jnp.dot, preferred_element_type=F32)
    return d(ah, bh) + (d(ah, bl) + d(al, bh))


def _sigmoid(x):
    return 1.0 / (1.0 + jnp.exp(-x))


def _rms_rows(x):
    return x * lax.rsqrt(jnp.mean(x * x, axis=-1, keepdims=True) + EPS)


def _mod_kernel(c_ref, w_ref, b_ref, o_ref):
    c = c_ref[...]
    o_ref[...] = _dot_hi(c * _sigmoid(c), w_ref[...]) + b_ref[...]


def _modulation(cond8, w_mod, b_mod):
    depth = w_mod.shape[0]
    ncol = w_mod.shape[2]
    tn = 1024
    return pl.pallas_call(
        _mod_kernel,
        grid=(depth, ncol // tn),
        in_specs=[pl.BlockSpec((8, D_MODEL), lambda l, j: (0, 0)),
                  pl.BlockSpec((None, D_MODEL, tn), lambda l, j: (l, 0, j)),
                  pl.BlockSpec((None, 1, tn), lambda l, j: (l, 0, j))],
        out_specs=pl.BlockSpec((None, 8, tn), lambda l, j: (l, 0, j)),
        out_shape=jax.ShapeDtypeStruct((depth, 8, ncol), F32),
        compiler_params=_cparams(("arbitrary", "arbitrary")),
        name="modulation",
    )(cond8, w_mod, b_mod.reshape(depth, 1, ncol))


def _adaln_bf16(x, g, shift, scale):
    return ((_rms_rows(x) * g) * (1.0 + scale) + shift).astype(BF16)


def _lat_or_ctx(x_ref, ctx_ref, lat_tiles):
    return jnp.where(pl.program_id(0) < lat_tiles, x_ref[...], ctx_ref[...])


def _proj_even_kernel(x_ref, ctx_ref, mod_ref, g_ref, w_ref, wg_ref, cs_ref, sn_ref, proj_ref, gates_ref,
                      *, lat_tiles):
    u = _adaln_bf16(_lat_or_ctx(x_ref, ctx_ref, lat_tiles), g_ref[...], mod_ref[0:1, :], mod_ref[1:2, :])
    cs = cs_ref[...]
    sn = sn_ref[...]
    kscale = HEAD_DIM ** -0.5
    for gi in range(EVEN_MAIN // 512):
        acc = jnp.dot(u, w_ref[:, gi * 512:(gi + 1) * 512], preferred_element_type=F32)
        if gi in (3, 4):
            parts = []
            for hh in range(RET_HEADS):
                blk = acc[:, hh * HEAD_DIM:(hh + 1) * HEAD_DIM]
                parts.append(blk * cs + pltpu.roll(blk, HEAD_DIM // 2, 1) * sn)
            acc = jnp.concatenate(parts, axis=1)
        if gi in (1, 4):
            acc = acc * kscale
        proj_ref[:, gi * 512:(gi + 1) * 512] = acc.astype(BF16)
    gates_ref[...] = jnp.dot(u, wg_ref[...], preferred_element_type=F32)


def _combine(h, y0, y1, gk, gate):
    moe = gk[:, 0:1] * y0.astype(F32) + gk[:, 1:2] * y1.astype(F32)
    return h + gate * moe


def _proj_odd_kernel(h_ref, y0_ref, y1_ref, gk_ref, modp_ref, mod_ref, g_ref, w_ref, wvt_ref,
                     c_ref, s1_ref, s2_ref, hout_ref, qk_ref, vt_ref):
    h = _combine(h_ref[...], y0_ref[...], y1_ref[...], gk_ref[...], modp_ref[5:6, :])
    hout_ref[...] = h
    u = _adaln_bf16(h, g_ref[...], mod_ref[0:1, :], mod_ref[1:2, :])
    c = c_ref[...]
    s1 = s1_ref[...]
    s2 = s2_ref[...]
    qscale = DA_DHEAD ** -0.5 * math.log2(math.e)
    nq = DA_HEADS * 2 * DA_DHEAD
    for gi in range(2 * nq // 512):
        acc = jnp.dot(u, w_ref[:, gi * 512:(gi + 1) * 512], preferred_element_type=F32)
        parts = []
        for hh in range(512 // 128):
            blk = acc[:, hh * 128:(hh + 1) * 128]
            parts.append(blk * c + pltpu.roll(blk, 128 - 16, 1) * s1 + pltpu.roll(blk, 16, 1) * s2)
        acc = jnp.concatenate(parts, axis=1)
        if gi * 512 < nq:
            acc = acc * qscale
        qk_ref[:, gi * 512:(gi + 1) * 512] = acc.astype(BF16)
    vt = lax.dot_general(wvt_ref[...], u, (((1,), (1,)), ((), ())), preferred_element_type=F32)
    vt_ref[...] = vt.astype(BF16)


def _tile_maps(B, S, L, tr):
    lat_tiles = B * S // tr
    per_b = S // tr

    def mod_idx(i):
        return jnp.where(i < lat_tiles, i // per_b, B)

    def rot_idx(i):
        return jnp.where(i < lat_tiles, i % per_b, per_b + (i - lat_tiles))

    return mod_idx, rot_idx


def _lat_ctx_specs(tr, lat_tiles):
    return [pl.BlockSpec((tr, D_MODEL), lambda i: (jnp.minimum(i, lat_tiles - 1), 0)),
            pl.BlockSpec((tr, D_MODEL), lambda i: (jnp.maximum(i - lat_tiles, 0), 0))]


def _proj_even(x2d, ctx2d, mod, g, w_main, w_gates, cs, sn, dims):
    B, S, L, tr = dims
    n = x2d.shape[0] + ctx2d.shape[0]
    lat_tiles = x2d.shape[0] // tr
    mod_idx, rot_idx = _tile_maps(B, S, L, tr)
    const = lambda i: (0, 0)
    return pl.pallas_call(
        functools.partial(_proj_even_kernel, lat_tiles=lat_tiles),
        grid=(n // tr,),
        in_specs=_lat_ctx_specs(tr, lat_tiles) + [
                  pl.BlockSpec((None, 8, D_MODEL), lambda i: (mod_idx(i), 0, 0)),
                  pl.BlockSpec((1, D_MODEL), const),
                  pl.BlockSpec((D_MODEL, EVEN_MAIN), const),
                  pl.BlockSpec((D_MODEL, GATE_PAD), const),
                  pl.BlockSpec((tr, 128), lambda i: (rot_idx(i), 0)),
                  pl.BlockSpec((tr, 128), lambda i: (rot_idx(i), 0))],
        out_specs=[pl.BlockSpec((tr, EVEN_MAIN), lambda i: (i, 0)),
                   pl.BlockSpec((tr, GATE_PAD), lambda i: (i, 0))],
        out_shape=[jax.ShapeDtypeStruct((n, EVEN_MAIN), BF16),
                   jax.ShapeDtypeStruct((n, GATE_PAD), F32)],
        compiler_params=_cparams(("arbitrary",)),
        name="proj_even",
    )(x2d, ctx2d, mod, g, w_main, w_gates, cs, sn)


def _proj_odd(h, y0, y1, gk, modp, mod, g, w_qk, w_vt, c, s1, s2, dims):
    B, S, L, tr = dims
    n = h.shape[0]
    mod_idx, rot_idx = _tile_maps(B, S, L, tr)
    const = lambda i: (0, 0)
    row = lambda i: (i, 0)
    nqk = w_qk.shape[1]
    return pl.pallas_call(
        _proj_odd_kernel,
        grid=(n // tr,),
        in_specs=[pl.BlockSpec((tr, D_MODEL), row),
                  pl.BlockSpec((tr, D_MODEL), row),
                  pl.BlockSpec((tr, D_MODEL), row),
                  pl.BlockSpec((tr, ROUTER_PAD), row),
                  pl.BlockSpec((None, 8, D_MODEL), lambda i: (mod_idx(i), 0, 0)),
                  pl.BlockSpec((None, 8, D_MODEL), lambda i: (mod_idx(i), 0, 0)),
                  pl.BlockSpec((1, D_MODEL), const),
                  pl.BlockSpec((D_MODEL, nqk), const),
                  pl.BlockSpec((D_MODEL, D_MODEL), const),
                  pl.BlockSpec((tr, 128), lambda i: (rot_idx(i), 0)),
                  pl.BlockSpec((tr, 128), lambda i: (rot_idx(i), 0)),
                  pl.BlockSpec((tr, 128), lambda i: (rot_idx(i), 0))],
        out_specs=[pl.BlockSpec((tr, D_MODEL), row),
                   pl.BlockSpec((tr, nqk), row),
                   pl.BlockSpec((D_MODEL, tr), lambda i: (0, i))],
        out_shape=[jax.ShapeDtypeStruct((n, D_MODEL), F32),
                   jax.ShapeDtypeStruct((n, nqk), BF16),
                   jax.ShapeDtypeStruct((D_MODEL, n), BF16)],
        compiler_params=_cparams(("arbitrary",)),
        name="proj_odd",
    )(h, y0, y1, gk, modp, mod, g, w_qk, w_vt, c, s1, s2)


_NT = (((1,), (1,)), ((), ()))
_TN = (((0,), (0,)), ((), ()))


def _scan_kernel(ld_ref, xf_ref, xb_ref, gf_ref, gb_ref, bias_ref, hf_ref, hb_ref,
                 ct_ref, n_ref, m_ref, r_ref, intra_ref, qd_ref, kd_ref):
    t = pl.program_id(1)
    nh = ML_HEADS
    row_i = lax.broadcasted_iota(jnp.int32, (CHUNK, CHUNK), 0)
    col_i = lax.broadcasted_iota(jnp.int32, (CHUNK, CHUNK), 1)
    row_f = row_i.astype(F32)
    col_f = col_i.astype(F32)

    @pl.when(t == 0)
    def _init():
        ct_ref[...] = jnp.zeros_like(ct_ref)
        n_ref[...] = jnp.zeros_like(n_ref)
        m_ref[...] = jnp.zeros_like(m_ref)
        r_ref[...] = jnp.zeros_like(r_ref)
        for d in range(2):
            for hh in range(nh):
                hd = d * nh + hh
                lg = ld_ref[hd]
                if d == 0:
                    intra = jnp.where(col_i <= row_i, jnp.exp(lg * (row_f - col_f)), 0.0)
                    qd = jnp.exp(lg * (row_f + 1.0))
                    kd = jnp.exp(lg * (CHUNK - 1.0 - row_f))
                else:
                    intra = jnp.where(col_i >= row_i, jnp.exp(lg * (col_f - row_f)), 0.0)
                    qd = jnp.exp(lg * (CHUNK - row_f))
                    kd = jnp.exp(lg * row_f)
                intra_ref[hd] = intra
                qd_ref[hd] = qd
                kd_ref[hd] = kd

    bias = bias_ref[...]
    dot = functools.partial(jnp.dot, preferred_element_type=F32)
    dg = functools.partial(lax.dot_general, preferred_element_type=F32)
    for d, (x_ref, g_ref, o_ref) in enumerate(((xf_ref, gf_ref, hf_ref), (xb_ref, gb_ref, hb_ref))):
        seen = (col_i <= row_i) if d == 0 else (col_i >= row_i)
        G = g_ref[...] + bias
        LF = jnp.minimum(G, 0.0) - jnp.log1p(jnp.exp(-jnp.abs(G)))
        tri = jnp.where(seen, 1.0, 0.0).astype(BF16)
        l_hi, l_mid, l_lo = _split3(LF)
        Bc = dot(tri, l_hi) + (dot(tri, l_mid) + dot(tri, l_lo))
        GT = G.T
        BT = Bc.T
        last = CHUNK - 1 if d == 0 else 0
        for hh in range(nh):
            hd = d * nh + hh
            c0 = hh * HEAD_DIM
            q = x_ref[:, c0:c0 + HEAD_DIM]
            k = x_ref[:, 512 + c0:512 + c0 + HEAD_DIM]
            v = x_ref[:, 1024 + c0:1024 + c0 + HEAD_DIM]
            ic_col = G[:, hd:hd + 1]
            ic_row = GT[hd:hd + 1, :]
            b_col = Bc[:, 8 + hd:9 + hd]
            b_row = BT[8 + hd:9 + hd, :]
            m0 = m_ref[hd][:, 0:1]
            dlog = jnp.where(seen, b_col - b_row + ic_row, NEG_BIG)
            inter = b_col + m0
            m_t = jnp.maximum(inter, jnp.max(dlog, axis=1, keepdims=True))
            s = dg(q, k, _NT) * jnp.exp(dlog - m_t)
            w_inter = jnp.exp(inter - m_t)
            ct = ct_ref[hd]
            n_row = n_ref[hd]
            numer = dot(s.astype(BF16), v) + w_inter * dot(q, ct.astype(BF16))
            denom = (jnp.sum(s, axis=1, keepdims=True)
                     + w_inter * jnp.sum(q.astype(F32) * n_row, axis=1, keepdims=True))
            hval = numer / jnp.maximum(jnp.abs(denom), jnp.exp(-m_t))
            o_ref[:, c0:c0 + HEAD_DIM] = hval.astype(BF16)
            b_last = b_col[last:last + 1, :]
            w_log_row = b_last - b_row + ic_row
            m_new = jnp.maximum(b_last + m0, jnp.max(w_log_row, axis=1, keepdims=True))
            decay = jnp.exp(b_last + m0 - m_new)
            w_col = jnp.exp(b_last - b_col + ic_col - m_new)
            w_row = jnp.exp(w_log_row - m_new)
            vw = (v.astype(F32) * w_col).astype(BF16)
            ct_ref[hd] = decay * ct + dg(k, vw, _TN)
            w8 = jnp.broadcast_to(w_row, (8, CHUNK)).astype(BF16)
            n_ref[hd] = decay * n_row + dot(w8, k)[0:1, :]
            m_ref[hd] = jnp.broadcast_to(m_new, (1, HEAD_DIM))
            rq = x_ref[:, 1536 + c0:1536 + c0 + HEAD_DIM]
            rk = x_ref[:, 2048 + c0:2048 + c0 + HEAD_DIM]
            rv = x_ref[:, 2560 + c0:2560 + c0 + HEAD_DIM]
            rs = dg(rq, rk, _NT) * intra_ref[hd]
            R = r_ref[hd]
            o = dot(rs.astype(BF16), rv) + qd_ref[hd] * dot(rq, R.astype(BF16))
            o_ref[:, 512 + c0:512 + c0 + HEAD_DIM] = o.astype(BF16)
            kdk = (rk.astype(F32) * kd_ref[hd]).astype(BF16)
            cdec = jnp.exp(ld_ref[hd] * jnp.full((1, HEAD_DIM), float(CHUNK), F32))
            r_ref[hd] = cdec * R + dg(kdk, rv, _TN)


def _scan(ret_ld8, proj, gates, bias, B, S, L):
    n = proj.shape[0]
    nlb, ncb = S // CHUNK, L // CHUNK
    nc = nlb + ncb

    def fwd(b, t):
        return jnp.where(t < ncb, B * nlb + b * ncb + t, b * nlb + t - ncb)

    def bwd(b, t):
        return jnp.where(t < ncb, B * nlb + b * ncb + (ncb - 1 - t), b * nlb + (nlb - 1 - (t - ncb)))

    state = pltpu.VMEM((2 * ML_HEADS, HEAD_DIM, HEAD_DIM), F32)
    vec = pltpu.VMEM((2 * ML_HEADS, 1, HEAD_DIM), F32)
    grid_spec = pltpu.PrefetchScalarGridSpec(
        num_scalar_prefetch=1,
        grid=(B, nc),
        in_specs=[pl.BlockSpec((CHUNK, EVEN_SCAN_COLS), lambda b, t, ld: (fwd(b, t), 0)),
                  pl.BlockSpec((CHUNK, EVEN_SCAN_COLS), lambda b, t, ld: (bwd(b, t), 0)),
                  pl.BlockSpec((CHUNK, GATE_PAD), lambda b, t, ld: (fwd(b, t), 0)),
                  pl.BlockSpec((CHUNK, GATE_PAD), lambda b, t, ld: (bwd(b, t), 0)),
                  pl.BlockSpec((1, GATE_PAD), lambda b, t, ld: (0, 0))],
        out_specs=[pl.BlockSpec((CHUNK, D_MODEL), lambda b, t, ld: (fwd(b, t), 0)),
                   pl.BlockSpec((CHUNK, D_MODEL), lambda b, t, ld: (bwd(b, t), 0))],
        scratch_shapes=[state, vec, vec, state, state, state, state],
    )
    return pl.pallas_call(
        _scan_kernel,
        grid_spec=grid_spec,
        out_shape=[jax.ShapeDtypeStruct((n, D_MODEL), BF16),
                   jax.ShapeDtypeStruct((n, D_MODEL), BF16)],
        compiler_params=_cparams(("arbitrary", "arbitrary")),
        name="scan",
    )(ret_ld8, proj, proj, gates, gates, bias)


def _post_tail(merged, h_in, mod_ref, g2_ref, wout_ref, wr_ref, br_ref, hout_ref, v_ref, lg_ref):
    y = jnp.dot(merged, wout_ref[...], preferred_element_type=F32)
    h = h_in + mod_ref[2:3, :] * y
    hout_ref[...] = h
    v = (_rms_rows(h) * g2_ref[...]) * (1.0 + mod_ref[4:5, :]) + mod_ref[3:4, :]
    for s in range(ROW_TILE_SUBLANES):
        v_ref[pl.ds(s, v.shape[0], stride=ROW_TILE_SUBLANES), :] = v[:, s * 128:(s + 1) * 128]
    lg_ref[...] = _dot_hi(v, wr_ref[...]) + br_ref[...]


def _head_ln(x, g):
    mu = jnp.mean(x, axis=-1, keepdims=True)
    xc = x - mu
    var = jnp.mean(xc * xc, axis=-1, keepdims=True)
    return xc * lax.rsqrt(var + EPS) * g


def _post_even_kernel(hf_ref, hb_ref, og_ref, mlg_ref, retg_ref, x_ref, ctx_ref, mod_ref, g2_ref, wout_ref,
                      wr_ref, br_ref, hout_ref, v_ref, lg_ref, *, lat_tiles):
    parts = []
    for hh in range(ML_HEADS):
        sl = slice(hh * HEAD_DIM, (hh + 1) * HEAD_DIM)
        ml = hf_ref[:, sl].astype(F32) + hb_ref[:, sl].astype(F32)
        parts.append(_head_ln(_sigmoid(og_ref[:, sl].astype(F32)) * ml, mlg_ref[:, sl]))
    for hh in range(RET_HEADS):
        sl = slice(512 + hh * HEAD_DIM, 512 + (hh + 1) * HEAD_DIM)
        ret = hf_ref[:, sl].astype(F32) + hb_ref[:, sl].astype(F32)
        rg = og_ref[:, sl].astype(F32)
        parts.append((rg * _sigmoid(rg)) * _head_ln(ret, retg_ref[:, hh * HEAD_DIM:(hh + 1) * HEAD_DIM]))
    merged = jnp.concatenate(parts, axis=1).astype(BF16)
    h_in = _lat_or_ctx(x_ref, ctx_ref, lat_tiles)
    _post_tail(merged, h_in, mod_ref, g2_ref, wout_ref, wr_ref, br_ref, hout_ref, v_ref, lg_ref)


def _post_odd_kernel(att_ref, h_ref, mod_ref, g2_ref, wout_ref, wr_ref, br_ref, hout_ref, v_ref, lg_ref):
    _post_tail(att_ref[...], h_ref[...], mod_ref, g2_ref, wout_ref, wr_ref, br_ref, hout_ref, v_ref, lg_ref)


def _post_specs(n, tr, mod_idx):
    const = lambda i: (0, 0)
    row = lambda i: (i, 0)
    tail_in = [pl.BlockSpec((tr, D_MODEL), row),
               pl.BlockSpec((None, 8, D_MODEL), lambda i: (mod_idx(i), 0, 0)),
               pl.BlockSpec((1, D_MODEL), const),
               pl.BlockSpec((D_MODEL, D_MODEL), const),
               pl.BlockSpec((D_MODEL, ROUTER_PAD), const),
               pl.BlockSpec((1, ROUTER_PAD), const)]
    out_specs = [pl.BlockSpec((tr, D_MODEL), row),
                 pl.BlockSpec((tr * ROW_TILE_SUBLANES, 128), row),
                 pl.BlockSpec((tr, ROUTER_PAD), row)]
    out_shape = [jax.ShapeDtypeStruct((n, D_MODEL), F32),
                 jax.ShapeDtypeStruct((n * ROW_TILE_SUBLANES, 128), F32),
                 jax.ShapeDtypeStruct((n, ROUTER_PAD), F32)]
    return tail_in, out_specs, out_shape


def _post_even(hf, hb, proj, mlg, retg, x2d, ctx2d, mod, g2, wout, wr, br, dims):
    B, S, L, tr = dims
    n = hf.shape[0]
    mod_idx, _ = _tile_maps(B, S, L, tr)
    tail_in, out_specs, out_shape = _post_specs(n, tr, mod_idx)
    lat_tiles = x2d.shape[0] // tr
    row = lambda i: (i, 0)
    const = lambda i: (0, 0)
    return pl.pallas_call(
        functools.partial(_post_even_kernel, lat_tiles=lat_tiles),
        grid=(n // tr,),
        in_specs=[pl.BlockSpec((tr, D_MODEL), row),
                  pl.BlockSpec((tr, D_MODEL), row),
                  pl.BlockSpec((tr, D_MODEL), lambda i: (i, EVEN_SCAN_COLS // D_MODEL)),
                  pl.BlockSpec((1, 512), const),
                  pl.BlockSpec((1, 512), const)] + _lat_ctx_specs(tr, lat_tiles) + tail_in[1:],
        out_specs=out_specs,
        out_shape=out_shape,
        compiler_params=_cparams(("arbitrary",)),
        name="post_even",
    )(hf, hb, proj, mlg, retg, x2d, ctx2d, mod, g2, wout, wr, br)


def _post_odd(att, h, mod, g2, wout, wr, br, dims):
    B, S, L, tr = dims
    n = att.shape[0]
    mod_idx, _ = _tile_maps(B, S, L, tr)
    tail_in, out_specs, out_shape = _post_specs(n, tr, mod_idx)
    return pl.pallas_call(
        _post_odd_kernel,
        grid=(n // tr,),
        in_specs=[pl.BlockSpec((tr, D_MODEL), lambda i: (i, 0))] + tail_in,
        out_specs=out_specs,
        out_shape=out_shape,
        compiler_params=_cparams(("arbitrary",)),
        name="post_odd",
    )(att, h, mod, g2, wout, wr, br)


SUM_ROWS = 16
ATTN_TILE = 1024


def _attn_kernel(q_ref, kl_ref, kc_ref, vl_ref, vc_ref, lam_ref, g_ref, o_ref,
                 acc_ref, m_ref, cmax_ref, qm_ref, sa_ref, sb_ref, sc_ref, *, tk, lam_init):
    n_lat = kl_ref.shape[0] // tk
    q = q_ref[...]
    lane = lax.broadcasted_iota(jnp.int32, q.shape, 1)
    zero = jnp.zeros_like(q)
    qm_ref[0] = jnp.where(lane < DA_DHEAD, q, zero)
    qm_ref[1] = jnp.where(lane >= DA_DHEAD, q, zero)
    acc_ref[...] = jnp.zeros_like(acc_ref)
    m_ref[...] = jnp.full(m_ref.shape, NEG_BIG, F32)

    def lat_k(c):
        return kl_ref[pl.ds(pl.multiple_of(c * tk, tk), tk), :]

    def lat_v(c):
        return vl_ref[:, pl.ds(pl.multiple_of(c * tk, tk), tk)]

    def scores(dst_ref, slot, kc):
        for mi in range(2):
            st = lax.dot_general(kc, qm_ref[mi], _NT, preferred_element_type=F32)
            dst_ref[mi] = st
            cmax_ref[slot, mi] = jnp.max(st, axis=0, keepdims=True)

    def absorb(src_ref, slot, vtc):
        vext = jnp.concatenate([vtc, jnp.ones((SUM_ROWS, vtc.shape[1]), BF16)], axis=0)
        for mi in range(2):
            m_old = m_ref[mi]
            m_new = jnp.maximum(m_old, cmax_ref[slot, mi])
            p = jnp.exp2(src_ref[mi] - m_new).astype(BF16)
            acc_ref[mi] = jnp.exp2(m_old - m_new) * acc_ref[mi] + jnp.dot(vext, p, preferred_element_type=F32)
            m_ref[mi] = m_new

    scores(sa_ref, 0, lat_k(0))

    def body(j, carry):
        c = 2 * j
        scores(sb_ref, 1, lat_k(c + 1))
        absorb(sa_ref, 0, lat_v(c))
        scores(sa_ref, 0, lat_k(c + 2))
        absorb(sb_ref, 1, lat_v(c + 1))
        return carry

    lax.fori_loop(0, n_lat // 2 - 1, body, 0)
    scores(sb_ref, 1, lat_k(n_lat - 1))
    absorb(sa_ref, 0, lat_v(n_lat - 2))
    scores(sc_ref, 2, kc_ref[...])
    absorb(sb_ref, 1, lat_v(n_lat - 1))
    absorb(sc_ref, 2, vc_ref[...])

    lp = lam_ref[...]
    lam = (jnp.exp(jnp.sum(lp[0:1, :] * lp[1:2, :], axis=1, keepdims=True))
           - jnp.exp(jnp.sum(lp[2:3, :] * lp[3:4, :], axis=1, keepdims=True)) + lam_init)
    o0 = acc_ref[0, 0:128, :] / acc_ref[0, 128:129, :]
    o1 = acc_ref[1, 0:128, :] / acc_ref[1, 128:129, :]
    ot = o0 - lam * o1
    ot = ot * lax.rsqrt(jnp.mean(ot * ot, axis=0, keepdims=True) + EPS)
    o = ot.T * g_ref[...] * (1.0 - lam_init)
    o_ref[...] = o.astype(BF16)


def _attention(qk, vt, lam_p, norm_g, lam_init, B, S, L):
    tq = ATTN_TILE
    tk = ATTN_TILE
    assert S % tq == 0 and S % (2 * tk) == 0
    nq = S // tq
    nh = DA_HEADS
    kcol = nh
    ctx0 = B * S // L
    kern = functools.partial(_attn_kernel, tk=tk, lam_init=lam_init)
    return pl.pallas_call(
        kern,
        grid=(B, nh, nq),
        in_specs=[pl.BlockSpec((tq, 128), lambda b, h, i: (b * nq + i, h)),
                  pl.BlockSpec((S, 128), lambda b, h, i: (b, kcol + h)),
                  pl.BlockSpec((L, 128), lambda b, h, i: (ctx0 + b, kcol + h)),
                  pl.BlockSpec((128, S), lambda b, h, i: (h, b)),
                  pl.BlockSpec((128, L), lambda b, h, i: (h, ctx0 + b)),
                  pl.BlockSpec((4, DA_DHEAD), lambda b, h, i: (0, 0)),
                  pl.BlockSpec((None, 1, 128), lambda b, h, i: (h, 0, 0))],
        out_specs=pl.BlockSpec((tq, 128), lambda b, h, i: (b * nq + i, h)),
        out_shape=jax.ShapeDtypeStruct((B * S, D_MODEL), BF16),
        scratch_shapes=[pltpu.VMEM((2, 128 + SUM_ROWS, tq), F32),
                        pltpu.VMEM((2, 1, tq), F32),
                        pltpu.VMEM((3, 2, 1, tq), F32),
                        pltpu.VMEM((2, tq, 128), BF16),
                        pltpu.VMEM((2, tk, tq), F32),
                        pltpu.VMEM((2, tk, tq), F32),
                        pltpu.VMEM((2, L, tq), F32)],
        compiler_params=_cparams(("arbitrary", "arbitrary", "arbitrary")),
        name="diff_attention",
    )(qk, qk, qk, vt, vt, lam_p, norm_g.reshape(nh, 1, 128))


def _expert_kernel(be_ref, nu_ref, tok_ref, tokn_ref, v_hbm, w1_ref, w3_ref, w2_ref, y_ref,
                   w1b_ref, w3b_ref, w2b_ref, xbuf_ref, sem_ref):
    i = pl.program_id(0)
    n_used = nu_ref[0]
    tile = ROW_TILE_SUBLANES

    def row_copy(idx_ref, r, slot):
        src = v_hbm.at[pl.ds(pl.multiple_of(idx_ref[0, r], tile), tile), :]
        return pltpu.make_async_copy(src, xbuf_ref.at[slot, pl.ds(r * tile, tile), :], sem_ref.at[slot])

    def start_gather(idx_ref, slot):
        for r in range(MOE_BLOCK):
            row_copy(idx_ref, r, slot).start(priority=r % 2)

    def wait_gather(slot):
        pltpu.make_async_copy(v_hbm.at[pl.ds(0, MOE_BLOCK * tile), :], xbuf_ref.at[slot], sem_ref.at[slot]).wait()

    @pl.when(i == 0)
    def _first_block():
        start_gather(tok_ref, 0)

    @pl.when((i == 0) | (be_ref[i] != be_ref[jnp.maximum(i - 1, 0)]))
    def _new_expert():
        w1b_ref[...] = w1_ref[...].astype(BF16)
        w3b_ref[...] = w3_ref[...].astype(BF16)
        w2b_ref[...] = w2_ref[...].astype(BF16)

    for slot in range(2):
        @pl.when((i % 2 == slot) & (i < n_used))
        def _compute():
            @pl.when(i + 1 < n_used)
            def _next_block():
                start_gather(tokn_ref, 1 - slot)

            wait_gather(slot)
            x = jnp.concatenate([xbuf_ref[slot, pl.ds(s, MOE_BLOCK, stride=tile), :] for s in range(tile)],
                                axis=1).astype(BF16)
            a = jnp.dot(x, w1b_ref[...], preferred_element_type=F32)
            b = jnp.dot(x, w3b_ref[...], preferred_element_type=F32)
            hm = ((a * _sigmoid(a)) * b).astype(BF16)
            y_ref[...] = jnp.dot(hm, w2b_ref[...], preferred_element_type=F32).astype(BF16)

    @pl.when(i >= n_used)
    def _skip():
        y_ref[...] = jnp.zeros_like(y_ref)


def _experts(blk_expert, n_used, buf_row, v_tiles, w1, w3, w2, layer):
    nblk = blk_expert.shape[0]
    tok3 = buf_row.reshape(nblk, 1, MOE_BLOCK)
    wmap = lambda i, be, nu: (layer, be[i], 0, 0)
    grid_spec = pltpu.PrefetchScalarGridSpec(
        num_scalar_prefetch=2,
        grid=(nblk,),
        in_specs=[pl.BlockSpec((None, 1, MOE_BLOCK), lambda i, be, nu: (i, 0, 0), memory_space=pltpu.SMEM),
                  pl.BlockSpec((None, 1, MOE_BLOCK), lambda i, be, nu: (jnp.minimum(i + 1, nblk - 1), 0, 0),
                               memory_space=pltpu.SMEM),
                  pl.BlockSpec(memory_space=pl.ANY),
                  pl.BlockSpec((None, None, D_MODEL, D_EXPERT), wmap),
                  pl.BlockSpec((None, None, D_MODEL, D_EXPERT), wmap),
                  pl.BlockSpec((None, None, D_EXPERT, D_MODEL), wmap)],
        out_specs=pl.BlockSpec((MOE_BLOCK, D_MODEL), lambda i, be, nu: (i, 0)),
        scratch_shapes=[pltpu.VMEM((D_MODEL, D_EXPERT), BF16),
                        pltpu.VMEM((D_MODEL, D_EXPERT), BF16),
                        pltpu.VMEM((D_EXPERT, D_MODEL), BF16),
                        pltpu.VMEM((2, MOE_BLOCK * ROW_TILE_SUBLANES, 128), F32),
                        pltpu.SemaphoreType.DMA((2,))],
    )
    return pl.pallas_call(
        _expert_kernel,
        grid_spec=grid_spec,
        out_shape=jax.ShapeDtypeStruct((nblk * MOE_BLOCK, D_MODEL), BF16),
        compiler_params=_cparams(("arbitrary",)),
        name="experts",
    )(blk_expert, n_used, tok3, tok3, v_tiles, w1, w3, w2)


def _route_kernel(lg_ref, idx_ref, gate_ref, cnt_ref, base_ref):
    tr = lg_ref.shape[0]

    @pl.when(pl.program_id(0) == 0)
    def _init():
        base_ref[...] = jnp.zeros_like(base_ref)

    lg = lg_ref[...]
    lane = lax.broadcasted_iota(jnp.int32, lg.shape, 1)
    first_arg = lambda x, mx: jnp.min(jnp.where(x == mx, lane, 128), axis=1, keepdims=True)
    is_g = lane < N_GROUPS
    gl = jnp.where(is_g, lg, NEG_BIG)
    gmax = jnp.max(gl, axis=1, keepdims=True)
    pg_top = 1.0 / jnp.sum(jnp.where(is_g, jnp.exp(gl - gmax), 0.0), axis=1, keepdims=True)
    grp = first_arg(gl, gmax)
    e_lo = N_GROUPS + grp * EXPERTS_PER_GROUP
    el = jnp.where((lane >= e_lo) & (lane < e_lo + EXPERTS_PER_GROUP), lg, NEG_BIG)
    m1 = jnp.max(el, axis=1, keepdims=True)
    i1 = first_arg(el, m1)
    el2 = jnp.where(lane == i1, NEG_BIG, el)
    m2 = jnp.max(el2, axis=1, keepdims=True)
    i2 = first_arg(el2, m2)
    p2 = jnp.exp(m2 - m1)
    g1 = pg_top / (1.0 + p2)
    g2 = pg_top * p2 / (1.0 + p2)

    hit1 = lane == i1
    hit2 = lane == i2
    onehot = jnp.where(hit1 | hit2, 1.0, 0.0)
    r_i = lax.broadcasted_iota(jnp.int32, (tr, tr), 0)
    c_i = lax.broadcasted_iota(jnp.int32, (tr, tr), 1)
    earlier = jnp.where(c_i < r_i, 1.0, 0.0).astype(BF16)
    before = jnp.dot(earlier, onehot.astype(BF16), preferred_element_type=F32) + base_ref[...]
    rank1 = jnp.sum(jnp.where(hit1, before, 0.0), axis=1, keepdims=True).astype(jnp.int32)
    rank2 = jnp.sum(jnp.where(hit2, before, 0.0), axis=1, keepdims=True).astype(jnp.int32)
    base = base_ref[...] + jnp.sum(onehot, axis=0, keepdims=True)
    base_ref[...] = base
    cnt_ref[...] = jnp.broadcast_to(base, cnt_ref.shape)

    idx = jnp.where(lane == 0, i1 - N_GROUPS,
                    jnp.where(lane == 1, i2 - N_GROUPS,
                              jnp.where(lane == 2, rank1, jnp.where(lane == 3, rank2, 0))))
    idx_ref[...] = idx.T[0:8, :]
    gate_ref[...] = jnp.where(lane == 0, g1, jnp.where(lane == 1, g2, 0.0))


def _route(logits):
    n = logits.shape[0]
    tr = 512 if n % 512 == 0 else 256
    idx, gate, cnt = pl.pallas_call(
        _route_kernel,
        grid=(n // tr,),
        in_specs=[pl.BlockSpec((tr, ROUTER_PAD), lambda i: (i, 0))],
        out_specs=[pl.BlockSpec((8, tr), lambda i: (0, i)),
                   pl.BlockSpec((tr, ROUTER_PAD), lambda i: (i, 0)),
                   pl.BlockSpec((8, ROUTER_PAD), lambda i: (0, 0))],
        out_shape=[jax.ShapeDtypeStruct((8, n), jnp.int32),
                   jax.ShapeDtypeStruct((n, ROUTER_PAD), F32),
                   jax.ShapeDtypeStruct((8, ROUTER_PAD), F32)],
        scratch_shapes=[pltpu.VMEM((1, ROUTER_PAD), F32)],
        compiler_params=_cparams(("arbitrary",)),
        name="route",
    )(logits)
    expert = idx[0:TOP_K]
    rank = idx[TOP_K:2 * TOP_K]
    counts = cnt[0, N_GROUPS:N_GROUPS + N_EXPERTS].astype(jnp.int32)
    a = n * TOP_K
    nblk = -(-a // MOE_BLOCK) + N_EXPERTS
    padded = (counts + MOE_BLOCK - 1) // MOE_BLOCK * MOE_BLOCK
    pend = jnp.cumsum(padded)
    start = pend - padded
    cstart = jnp.cumsum(counts) - counts
    hit = expert[:, None, :] == jnp.arange(N_EXPERTS, dtype=jnp.int32)[None, :, None]
    dest = jnp.sum(jnp.where(hit, start[None, :, None], 0), axis=1) + rank
    tok = jnp.broadcast_to(jnp.arange(n, dtype=jnp.int32)[None, :], (TOP_K, n))
    _, by_row = lax.sort_key_val(dest.reshape(-1), tok.reshape(-1))
    n_used = (pend[-1] // MOE_BLOCK).astype(jnp.int32)
    blk_row = jnp.minimum(jnp.arange(nblk, dtype=jnp.int32), n_used - 1) * MOE_BLOCK
    blk_expert = jnp.clip(jnp.sum((pend[None, :] <= blk_row[:, None]).astype(jnp.int32), axis=1), 0, N_EXPERTS - 1)
    row = jnp.arange(nblk * MOE_BLOCK, dtype=jnp.int32).reshape(nblk, MOE_BLOCK)
    in_expert = row - start[blk_expert][:, None]
    src = jnp.clip(in_expert + cstart[blk_expert][:, None], 0, a - 1)
    live = (in_expert < counts[blk_expert][:, None]) & (jnp.arange(nblk)[:, None] < n_used)
    buf_tok = jnp.where(live, by_row.at[src].get(mode="promise_in_bounds"), 0).reshape(-1)
    return buf_tok, blk_expert.astype(jnp.int32), n_used.reshape(1), dest, gate


def _moe(v, logits, w1, w3, w2, layer):
    buf_tok, blk_expert, n_used, pos, gate = _route(logits)
    yb = _experts(blk_expert, n_used, buf_tok * ROW_TILE_SUBLANES, v, w1, w3, w2, layer)
    y0 = yb.at[pos[0]].get(mode="promise_in_bounds")
    y1 = yb.at[pos[1]].get(mode="promise_in_bounds")
    return y0, y1, gate


def _final_kernel(h_ref, y0_ref, y1_ref, gk_ref, modp_ref, g_ref, o_ref):
    h = _combine(h_ref[...], y0_ref[...], y1_ref[...], gk_ref[...], modp_ref[5:6, :])
    o_ref[...] = _rms_rows(h) * g_ref[...]


def _final(h, y0, y1, gk, modp, g, dims):
    B, S, L, tr = dims
    n = h.shape[0]
    mod_idx, _ = _tile_maps(B, S, L, tr)
    row = lambda i: (i, 0)
    return pl.pallas_call(
        _final_kernel,
        grid=(n // tr,),
        in_specs=[pl.BlockSpec((tr, D_MODEL), row),
                  pl.BlockSpec((tr, D_MODEL), row),
                  pl.BlockSpec((tr, D_MODEL), row),
                  pl.BlockSpec((tr, ROUTER_PAD), row),
                  pl.BlockSpec((None, 8, D_MODEL), lambda i: (mod_idx(i), 0, 0)),
                  pl.BlockSpec((1, D_MODEL), lambda i: (0, 0))],
        out_specs=pl.BlockSpec((tr, D_MODEL), row),
        out_shape=jax.ShapeDtypeStruct((n, D_MODEL), F32),
        compiler_params=_cparams(("arbitrary",)),
        name="final_norm",
    )(h, y0, y1, gk, modp, g)


def _rotary_tables_even(S, L, B):
    nf = HEAD_DIM // 2
    inv = ROPE_BASE ** (-jnp.arange(nf, dtype=F32) / nf)
    pos = jnp.concatenate([L + jnp.arange(S), jnp.tile(jnp.arange(L), B)]).astype(F32)
    ang = pos[:, None] * inv[None, :]
    cos, sin = jnp.cos(ang), jnp.sin(ang)
    return jnp.concatenate([cos, cos], 1), jnp.concatenate([-sin, sin], 1)


def _rotary_tables_odd(S, L, B):
    nf = DA_DHEAD // 4
    inv = ROPE_BASE ** (-jnp.arange(nf, dtype=F32) / nf)
    t = jnp.arange(S)
    rows, cols = (t // GRID_W).astype(F32), (t % GRID_W).astype(F32)
    j = np.arange(128)
    f_idx = j % nf
    use_col = (j % DA_DHEAD) >= DA_DHEAD // 2
    first = (j % (2 * nf)) < nf
    ang = jnp.where(use_col[None, :], cols[:, None], rows[:, None]) * inv[f_idx][None, :]
    cos, sin = jnp.cos(ang), jnp.sin(ang)
    c_lat = cos
    s1_lat = jnp.where(first[None, :], -sin, 0.0)
    s2_lat = jnp.where(first[None, :], 0.0, sin)
    nctx = B * L
    c = jnp.concatenate([c_lat, jnp.ones((nctx, 128), F32)])
    s1 = jnp.concatenate([s1_lat, jnp.zeros((nctx, 128), F32)])
    s2 = jnp.concatenate([s2_lat, jnp.zeros((nctx, 128), F32)])
    return c, s1, s2


def kernel(x, c, ctx, c_ctx, w_mod, b_mod, norm1_g, norm2_g, w_in_even, ml_gate_b, ml_norm_g, ret_log_decay, ret_norm_g, w_out_even, w_in_odd, da_lambda, da_norm_g, w_out_odd, router_g_w, router_g_b, router_e_w, router_e_b, w1, w3, w2, final_norm_g):
    B, S, D = x.shape
    L = ctx.shape[1]
    depth = w_mod.shape[0]
    assert D == D_MODEL and depth == 2 and S % CHUNK == 0 and L % CHUNK == 0 and S % L == 0
    n_lat, n_ctx = B * S, B * L
    tr = _row_tile(S, n_ctx)
    dims = (B, S, L, tr)

    x2d, ctx2d = x.reshape(n_lat, D), ctx.reshape(n_ctx, D)

    cond8 = jnp.zeros((8, D), F32).at[:B].set(c).at[B].set(c_ctx)
    mod = jnp.pad(_modulation(cond8, w_mod, b_mod).reshape(depth, 8, 6, D), ((0, 0), (0, 0), (0, 2), (0, 0)))

    def router_w(l):
        wr = jnp.zeros((D, ROUTER_PAD), F32)
        wr = wr.at[:, :N_GROUPS].set(router_g_w[l]).at[:, N_GROUPS:N_GROUPS + N_EXPERTS].set(router_e_w[l])
        br = jnp.zeros((1, ROUTER_PAD), F32)
        br = br.at[0, :N_GROUPS].set(router_g_b[l]).at[0, N_GROUPS:N_GROUPS + N_EXPERTS].set(router_e_b[l])
        return wr, br

    wi = w_in_even[0]
    mq, mk, mv, mo, mi, mf, rq, rk, rv, rg = jnp.split(wi, np.cumsum(
        [512, 512, 512, 512, 8, 8, 512, 512, 512])[:9].tolist(), axis=1)
    w_main = jnp.concatenate([mq, mk, mv, rq, rk, rv, mo, rg], 1).astype(BF16)
    w_gates = jnp.pad(jnp.concatenate([mi, mf], 1), ((0, 0), (0, GATE_PAD - 16))).astype(BF16)
    gate_bias = jnp.pad(jnp.concatenate([ml_gate_b[0][:, 0].reshape(-1), ml_gate_b[0][:, 1].reshape(-1)]),
                        (0, GATE_PAD - 16)).reshape(1, GATE_PAD)
    cs, sn = _rotary_tables_even(S, L, B)
    proj, gates = _proj_even(x2d, ctx2d, mod[0], norm1_g[0].reshape(1, D), w_main, w_gates, cs, sn, dims)
    hf, hb = _scan(ret_log_decay[0].reshape(-1), proj, gates, gate_bias, B, S, L)
    wr, br = router_w(0)
    h, v, logits = _post_even(hf, hb, proj, ml_norm_g[0].reshape(1, -1), ret_norm_g[0].reshape(1, -1),
                              x2d, ctx2d, mod[0], norm2_g[0].reshape(1, D), w_out_even[0].astype(BF16), wr, br, dims)
    y0, y1, gk = _moe(v, logits, w1, w3, w2, 0)

    lam_init = 0.8 - 0.6 * math.exp(-0.3 * 1)
    nqk = 4 * DA_HEADS * DA_DHEAD
    w_qk = w_in_odd[0][:, :nqk].astype(BF16)
    w_vt = w_in_odd[0][:, nqk:].T.astype(BF16)
    c2, s1, s2 = _rotary_tables_odd(S, L, B)
    h, qk, vt = _proj_odd(h, y0, y1, gk, mod[0], mod[1], norm1_g[1].reshape(1, D), w_qk, w_vt, c2, s1, s2, dims)
    att = _attention(qk, vt, da_lambda[0], da_norm_g[0], lam_init, B, S, L)
    wr, br = router_w(1)
    h_lat, v, logits = _post_odd(att, h, mod[1], norm2_g[1].reshape(1, D), w_out_odd[0].astype(BF16), wr, br, dims)
    y0, y1, gk = _moe(v, logits, w1, w3, w2, 1)
    out = _final(h_lat, y0, y1, gk, mod[1], final_norm_g.reshape(1, D), dims)
    return out.reshape(B, S, D)
```

```python
import functools
import math

import numpy as np
import jax
import jax.numpy as jnp
from jax import lax
from jax.experimental import pallas as pl
from jax.experimental.pallas import tpu as pltpu

F32 = jnp.float32
BF16 = jnp.bfloat16

D_MODEL = 1024
GRID_W = 64
CHUNK = 128
EPS = 1e-6
ROPE_BASE = 10000.0
ML_HEADS = 4
RET_HEADS = 4
HEAD_DIM = 128
DA_HEADS = 8
DA_DHEAD = 64
N_GROUPS = 4
EXPERTS_PER_GROUP = 8
N_EXPERTS = N_GROUPS * EXPERTS_PER_GROUP
TOP_K = 2
D_EXPERT = D_MODEL // 2
MOE_BLOCK = 256
ROW_TILE_SUBLANES = D_MODEL // 128
ROUTER_PAD = 128
GATE_PAD = 128
EVEN_MAIN = 4096
EVEN_SCAN_COLS = 3072
NEG_BIG = -1e30

V7X_VMEM_LIMIT = 56 * 1024 * 1024


def _cparams(sem):
    return pltpu.CompilerParams(dimension_semantics=sem, vmem_limit_bytes=V7X_VMEM_LIMIT)


def _row_tile(n_lat_per_batch, n_ctx_total):
    for t in (512, 256, 128):
        if n_lat_per_batch % t == 0 and n_ctx_total % t == 0:
            return t
    raise ValueError("row counts must be multiples of 128")


def _split3(a):
    hi = a.astype(BF16)
    r1 = a - hi.astype(F32)
    mid = r1.astype(BF16)
    lo = (r1 - mid.astype(F32)).astype(BF16)
    return hi, mid, lo


def _dot_hi(a, b):
    ah, al, _ = _split3(a)
    bh, bl, _ = _split3(b)
    d = functools.partial(jnp.dot, preferred_element_type=F32)
    return d(ah, bh) + (d(ah, bl) + d(al, bh))


def _sigmoid(x):
    return 1.0 / (1.0 + jnp.exp(-x))


def _rms_rows(x):
    return x * lax.rsqrt(jnp.mean(x * x, axis=-1, keepdims=True) + EPS)


def _mod_kernel(c_ref, w_ref, b_ref, o_ref):
    c = c_ref[...]
    o_ref[...] = _dot_hi(c * _sigmoid(c), w_ref[...]) + b_ref[...]


def _modulation(cond8, w_mod, b_mod):
    depth = w_mod.shape[0]
    ncol = w_mod.shape[2]
    tn = 1024
    return pl.pallas_call(
        _mod_kernel,
        grid=(depth, ncol // tn),
        in_specs=[pl.BlockSpec((8, D_MODEL), lambda l, j: (0, 0)),
                  pl.BlockSpec((None, D_MODEL, tn), lambda l, j: (l, 0, j)),
                  pl.BlockSpec((None, 1, tn), lambda l, j: (l, 0, j))],
        out_specs=pl.BlockSpec((None, 8, tn), lambda l, j: (l, 0, j)),
        out_shape=jax.ShapeDtypeStruct((depth, 8, ncol), F32),
        compiler_params=_cparams(("arbitrary", "arbitrary")),
        name="modulation",
    )(cond8, w_mod, b_mod.reshape(depth, 1, ncol))


def _adaln_bf16(x, g, shift, scale):
    return ((_rms_rows(x) * g) * (1.0 + scale) + shift).astype(BF16)


def _lat_or_ctx(x_ref, ctx_ref, lat_tiles):
    return jnp.where(pl.program_id(0) < lat_tiles, x_ref[...], ctx_ref[...])


def _proj_even_kernel(x_ref, ctx_ref, mod_ref, g_ref, w_ref, wg_ref, cs_ref, sn_ref, proj_ref, gates_ref,
                      *, lat_tiles):
    u = _adaln_bf16(_lat_or_ctx(x_ref, ctx_ref, lat_tiles), g_ref[...], mod_ref[0:1, :], mod_ref[1:2, :])
    cs = cs_ref[...]
    sn = sn_ref[...]
    kscale = HEAD_DIM ** -0.5
    for gi in range(EVEN_MAIN // 512):
        acc = jnp.dot(u, w_ref[:, gi * 512:(gi + 1) * 512], preferred_element_type=F32)
        if gi in (3, 4):
            parts = []
            for hh in range(RET_HEADS):
                blk = acc[:, hh * HEAD_DIM:(hh + 1) * HEAD_DIM]
                parts.append(blk * cs + pltpu.roll(blk, HEAD_DIM // 2, 1) * sn)
            acc = jnp.concatenate(parts, axis=1)
        if gi in (1, 4):
            acc = acc * kscale
        proj_ref[:, gi * 512:(gi + 1) * 512] = acc.astype(BF16)
    gates_ref[...] = jnp.dot(u, wg_ref[...], preferred_element_type=F32)


def _combine(h, y0, y1, gk, gate):
    moe = gk[:, 0:1] * y0.astype(F32) + gk[:, 1:2] * y1.astype(F32)
    return h + gate * moe


def _proj_odd_kernel(h_ref, y0_ref, y1_ref, gk_ref, modp_ref, mod_ref, g_ref, w_ref, wvt_ref,
                     c_ref, s1_ref, s2_ref, hout_ref, qk_ref, vt_ref):
    h = _combine(h_ref[...], y0_ref[...], y1_ref[...], gk_ref[...], modp_ref[5:6, :])
    hout_ref[...] = h
    u = _adaln_bf16(h, g_ref[...], mod_ref[0:1, :], mod_ref[1:2, :])
    c = c_ref[...]
    s1 = s1_ref[...]
    s2 = s2_ref[...]
    qscale = DA_DHEAD ** -0.5 * math.log2(math.e)
    nq = DA_HEADS * 2 * DA_DHEAD
    for gi in range(2 * nq // 512):
        acc = jnp.dot(u, w_ref[:, gi * 512:(gi + 1) * 512], preferred_element_type=F32)
        parts = []
        for hh in range(512 // 128):
            blk = acc[:, hh * 128:(hh + 1) * 128]
            parts.append(blk * c + pltpu.roll(blk, 128 - 16, 1) * s1 + pltpu.roll(blk, 16, 1) * s2)
        acc = jnp.concatenate(parts, axis=1)
        if gi * 512 < nq:
            acc = acc * qscale
        qk_ref[:, gi * 512:(gi + 1) * 512] = acc.astype(BF16)
    vt = lax.dot_general(wvt_ref[...], u, (((1,), (1,)), ((), ())), preferred_element_type=F32)
    vt_ref[...] = vt.astype(BF16)


def _tile_maps(B, S, L, tr):
    lat_tiles = B * S // tr
    per_b = S // tr

    def mod_idx(i):
        return jnp.where(i < lat_tiles, i // per_b, B)

    def rot_idx(i):
        return jnp.where(i < lat_tiles, i % per_b, per_b + (i - lat_tiles))

    return mod_idx, rot_idx


def _lat_ctx_specs(tr, lat_tiles):
    return [pl.BlockSpec((tr, D_MODEL), lambda i: (jnp.minimum(i, lat_tiles - 1), 0)),
            pl.BlockSpec((tr, D_MODEL), lambda i: (jnp.maximum(i - lat_tiles, 0), 0))]


def _proj_even(x2d, ctx2d, mod, g, w_main, w_gates, cs, sn, dims):
    B, S, L, tr = dims
    n = x2d.shape[0] + ctx2d.shape[0]
    lat_tiles = x2d.shape[0] // tr
    mod_idx, rot_idx = _tile_maps(B, S, L, tr)
    const = lambda i: (0, 0)
    return pl.pallas_call(
        functools.partial(_proj_even_kernel, lat_tiles=lat_tiles),
        grid=(n // tr,),
        in_specs=_lat_ctx_specs(tr, lat_tiles) + [
                  pl.BlockSpec((None, 8, D_MODEL), lambda i: (mod_idx(i), 0, 0)),
                  pl.BlockSpec((1, D_MODEL), const),
                  pl.BlockSpec((D_MODEL, EVEN_MAIN), const),
                  pl.BlockSpec((D_MODEL, GATE_PAD), const),
                  pl.BlockSpec((tr, 128), lambda i: (rot_idx(i), 0)),
                  pl.BlockSpec((tr, 128), lambda i: (rot_idx(i), 0))],
        out_specs=[pl.BlockSpec((tr, EVEN_MAIN), lambda i: (i, 0)),
                   pl.BlockSpec((tr, GATE_PAD), lambda i: (i, 0))],
        out_shape=[jax.ShapeDtypeStruct((n, EVEN_MAIN), BF16),
                   jax.ShapeDtypeStruct((n, GATE_PAD), F32)],
        compiler_params=_cparams(("arbitrary",)),
        name="proj_even",
    )(x2d, ctx2d, mod, g, w_main, w_gates, cs, sn)


def _proj_odd(h, y0, y1, gk, modp, mod, g, w_qk, w_vt, c, s1, s2, dims):
    B, S, L, tr = dims
    n = h.shape[0]
    mod_idx, rot_idx = _tile_maps(B, S, L, tr)
    const = lambda i: (0, 0)
    row = lambda i: (i, 0)
    nqk = w_qk.shape[1]
    return pl.pallas_call(
        _proj_odd_kernel,
        grid=(n // tr,),
        in_specs=[pl.BlockSpec((tr, D_MODEL), row),
                  pl.BlockSpec((tr, D_MODEL), row),
                  pl.BlockSpec((tr, D_MODEL), row),
                  pl.BlockSpec((tr, ROUTER_PAD), row),
                  pl.BlockSpec((None, 8, D_MODEL), lambda i: (mod_idx(i), 0, 0)),
                  pl.BlockSpec((None, 8, D_MODEL), lambda i: (mod_idx(i), 0, 0)),
                  pl.BlockSpec((1, D_MODEL), const),
                  pl.BlockSpec((D_MODEL, nqk), const),
                  pl.BlockSpec((D_MODEL, D_MODEL), const),
                  pl.BlockSpec((tr, 128), lambda i: (rot_idx(i), 0)),
                  pl.BlockSpec((tr, 128), lambda i: (rot_idx(i), 0)),
                  pl.BlockSpec((tr, 128), lambda i: (rot_idx(i), 0))],
        out_specs=[pl.BlockSpec((tr, D_MODEL), row),
                   pl.BlockSpec((tr, nqk), row),
                   pl.BlockSpec((D_MODEL, tr), lambda i: (0, i))],
        out_shape=[jax.ShapeDtypeStruct((n, D_MODEL), F32),
                   jax.ShapeDtypeStruct((n, nqk), BF16),
                   jax.ShapeDtypeStruct((D_MODEL, n), BF16)],
        compiler_params=_cparams(("arbitrary",)),
        name="proj_odd",
    )(h, y0, y1, gk, modp, mod, g, w_qk, w_vt, c, s1, s2)


_NT = (((1,), (1,)), ((), ()))
_TN = (((0,), (0,)), ((), ()))


def _scan_kernel(ld_ref, xf_ref, xb_ref, gf_ref, gb_ref, gfn_ref, gbn_ref, bias_ref, hf_ref, hb_ref,
                 ct_ref, n_ref, m_ref, r_ref, intra_ref, qd_ref, kd_ref, prep_ref):
    t = pl.program_id(1)
    nh = ML_HEADS
    row_i = lax.broadcasted_iota(jnp.int32, (CHUNK, CHUNK), 0)
    col_i = lax.broadcasted_iota(jnp.int32, (CHUNK, CHUNK), 1)
    row_f = row_i.astype(F32)
    col_f = col_i.astype(F32)
    bias = bias_ref[...]
    dot = functools.partial(jnp.dot, preferred_element_type=F32)
    dg = functools.partial(lax.dot_general, preferred_element_type=F32)

    def seen_mask(d):
        return (col_i <= row_i) if d == 0 else (col_i >= row_i)

    def gate_prep(g_ref, d, slot):
        G = g_ref[...] + bias
        LF = jnp.minimum(G, 0.0) - jnp.log1p(jnp.exp(-jnp.abs(G)))
        tri = jnp.where(seen_mask(d), 1.0, 0.0).astype(BF16)
        l_hi, l_mid, l_lo = _split3(LF)
        Bc = dot(tri, l_hi) + (dot(tri, l_mid) + dot(tri, l_lo))
        prep_ref[slot, d, 0] = G
        prep_ref[slot, d, 1] = Bc
        prep_ref[slot, d, 2] = G.T
        prep_ref[slot, d, 3] = Bc.T

    @pl.when(t == 0)
    def _init():
        ct_ref[...] = jnp.zeros_like(ct_ref)
        n_ref[...] = jnp.zeros_like(n_ref)
        m_ref[...] = jnp.zeros_like(m_ref)
        r_ref[...] = jnp.zeros_like(r_ref)
        gate_prep(gf_ref, 0, 0)
        gate_prep(gb_ref, 1, 0)
        for d in range(2):
            for hh in range(nh):
                hd = d * nh + hh
                lg = ld_ref[hd]
                if d == 0:
                    intra = jnp.where(col_i <= row_i, jnp.exp(lg * (row_f - col_f)), 0.0)
                    qd = jnp.exp(lg * (row_f + 1.0))
                    kd = jnp.exp(lg * (CHUNK - 1.0 - row_f))
                else:
                    intra = jnp.where(col_i >= row_i, jnp.exp(lg * (col_f - row_f)), 0.0)
                    qd = jnp.exp(lg * (CHUNK - row_f))
                    kd = jnp.exp(lg * row_f)
                intra_ref[hd] = intra
                qd_ref[hd] = qd
                kd_ref[hd] = kd

    cur = t % 2
    gate_prep(gfn_ref, 0, 1 - cur)
    gate_prep(gbn_ref, 1, 1 - cur)

    for d, (x_ref, o_ref) in enumerate(((xf_ref, hf_ref), (xb_ref, hb_ref))):
        seen = seen_mask(d)
        G = prep_ref[cur, d, 0]
        Bc = prep_ref[cur, d, 1]
        GT = prep_ref[cur, d, 2]
        BT = prep_ref[cur, d, 3]
        last = CHUNK - 1 if d == 0 else 0
        for hh in range(nh):
            hd = d * nh + hh
            c0 = hh * HEAD_DIM
            q = x_ref[:, c0:c0 + HEAD_DIM]
            k = x_ref[:, 512 + c0:512 + c0 + HEAD_DIM]
            v = x_ref[:, 1024 + c0:1024 + c0 + HEAD_DIM]
            ic_col = G[:, hd:hd + 1]
            ic_row = GT[hd:hd + 1, :]
            b_col = Bc[:, 8 + hd:9 + hd]
            b_row = BT[8 + hd:9 + hd, :]
            m0 = m_ref[hd][:, 0:1]
            dlog = jnp.where(seen, b_col - b_row + ic_row, NEG_BIG)
            inter = b_col + m0
            m_t = jnp.maximum(inter, jnp.max(dlog, axis=1, keepdims=True))
            s = dg(q, k, _NT) * jnp.exp(dlog - m_t)
            w_inter = jnp.exp(inter - m_t)
            ct = ct_ref[hd]
            n_row = n_ref[hd]
            numer = dot(s.astype(BF16), v) + w_inter * dot(q, ct.astype(BF16))
            denom = (jnp.sum(s, axis=1, keepdims=True)
                     + w_inter * jnp.sum(q.astype(F32) * n_row, axis=1, keepdims=True))
            hval = numer / jnp.maximum(jnp.abs(denom), jnp.exp(-m_t))
            o_ref[:, c0:c0 + HEAD_DIM] = hval.astype(BF16)
            b_last = b_col[last:last + 1, :]
            w_log_row = b_last - b_row + ic_row
            m_new = jnp.maximum(b_last + m0, jnp.max(w_log_row, axis=1, keepdims=True))
            decay = jnp.exp(b_last + m0 - m_new)
            w_col = jnp.exp(b_last - b_col + ic_col - m_new)
            w_row = jnp.exp(w_log_row - m_new)
            vw = (v.astype(F32) * w_col).astype(BF16)
            ct_ref[hd] = decay * ct + dg(k, vw, _TN)
            w8 = jnp.broadcast_to(w_row, (8, CHUNK)).astype(BF16)
            n_ref[hd] = decay * n_row + dot(w8, k)[0:1, :]
            m_ref[hd] = jnp.broadcast_to(m_new, (1, HEAD_DIM))
            rq = x_ref[:, 1536 + c0:1536 + c0 + HEAD_DIM]
            rk = x_ref[:, 2048 + c0:2048 + c0 + HEAD_DIM]
            rv = x_ref[:, 2560 + c0:2560 + c0 + HEAD_DIM]
            rs = dg(rq, rk, _NT) * intra_ref[hd]
            R = r_ref[hd]
            o = dot(rs.astype(BF16), rv) + qd_ref[hd] * dot(rq, R.astype(BF16))
            o_ref[:, 512 + c0:512 + c0 + HEAD_DIM] = o.astype(BF16)
            kdk = (rk.astype(F32) * kd_ref[hd]).astype(BF16)
            cdec = jnp.exp(ld_ref[hd] * jnp.full((1, HEAD_DIM), float(CHUNK), F32))
            r_ref[hd] = cdec * R + dg(kdk, rv, _TN)


def _scan(ret_ld8, proj, gates, bias, B, S, L):
    n = proj.shape[0]
    nlb, ncb = S // CHUNK, L // CHUNK
    nc = nlb + ncb

    def fwd(b, t):
        return jnp.where(t < ncb, B * nlb + b * ncb + t, b * nlb + t - ncb)

    def bwd(b, t):
        return jnp.where(t < ncb, B * nlb + b * ncb + (ncb - 1 - t), b * nlb + (nlb - 1 - (t - ncb)))

    state = pltpu.VMEM((2 * ML_HEADS, HEAD_DIM, HEAD_DIM), F32)
    vec = pltpu.VMEM((2 * ML_HEADS, 1, HEAD_DIM), F32)
    grid_spec = pltpu.PrefetchScalarGridSpec(
        num_scalar_prefetch=1,
        grid=(B, nc),
        in_specs=[pl.BlockSpec((CHUNK, EVEN_SCAN_COLS), lambda b, t, ld: (fwd(b, t), 0)),
                  pl.BlockSpec((CHUNK, EVEN_SCAN_COLS), lambda b, t, ld: (bwd(b, t), 0)),
                  pl.BlockSpec((CHUNK, GATE_PAD), lambda b, t, ld: (fwd(b, t), 0)),
                  pl.BlockSpec((CHUNK, GATE_PAD), lambda b, t, ld: (bwd(b, t), 0)),
                  pl.BlockSpec((CHUNK, GATE_PAD), lambda b, t, ld: (fwd(b, jnp.minimum(t + 1, nc - 1)), 0)),
                  pl.BlockSpec((CHUNK, GATE_PAD), lambda b, t, ld: (bwd(b, jnp.minimum(t + 1, nc - 1)), 0)),
                  pl.BlockSpec((1, GATE_PAD), lambda b, t, ld: (0, 0))],
        out_specs=[pl.BlockSpec((CHUNK, D_MODEL), lambda b, t, ld: (fwd(b, t), 0)),
                   pl.BlockSpec((CHUNK, D_MODEL), lambda b, t, ld: (bwd(b, t), 0))],
        scratch_shapes=[state, vec, vec, state, state, state, state,
                        pltpu.VMEM((2, 2, 4, CHUNK, CHUNK), F32)],
    )
    return pl.pallas_call(
        _scan_kernel,
        grid_spec=grid_spec,
        out_shape=[jax.ShapeDtypeStruct((n, D_MODEL), BF16),
                   jax.ShapeDtypeStruct((n, D_MODEL), BF16)],
        compiler_params=_cparams(("arbitrary", "arbitrary")),
        name="scan",
    )(ret_ld8, proj, proj, gates, gates, gates, gates, bias)


def _post_tail(merged, h_in, mod_ref, g2_ref, wout_ref, wr_ref, br_ref, hout_ref, v_ref, idx_ref, gate_ref,
               cnt_ref, base_ref):
    y = jnp.dot(merged, wout_ref[...], preferred_element_type=F32)
    h = h_in + mod_ref[2:3, :] * y
    hout_ref[...] = h
    v = (_rms_rows(h) * g2_ref[...]) * (1.0 + mod_ref[4:5, :]) + mod_ref[3:4, :]
    for s in range(ROW_TILE_SUBLANES):
        v_ref[pl.ds(s, v.shape[0], stride=ROW_TILE_SUBLANES), :] = v[:, s * 128:(s + 1) * 128]
    _route_tile(_dot_hi(v, wr_ref[...]) + br_ref[...], idx_ref, gate_ref, cnt_ref, base_ref)


def _head_ln(x, g):
    mu = jnp.mean(x, axis=-1, keepdims=True)
    xc = x - mu
    var = jnp.mean(xc * xc, axis=-1, keepdims=True)
    return xc * lax.rsqrt(var + EPS) * g


def _post_even_kernel(hf_ref, hb_ref, og_ref, mlg_ref, retg_ref, x_ref, ctx_ref, mod_ref, g2_ref, wout_ref,
                      wr_ref, br_ref, hout_ref, v_ref, idx_ref, gate_ref, cnt_ref, base_ref, *, lat_tiles):
    parts = []
    for hh in range(ML_HEADS):
        sl = slice(hh * HEAD_DIM, (hh + 1) * HEAD_DIM)
        ml = hf_ref[:, sl].astype(F32) + hb_ref[:, sl].astype(F32)
        parts.append(_head_ln(_sigmoid(og_ref[:, sl].astype(F32)) * ml, mlg_ref[:, sl]))
    for hh in range(RET_HEADS):
        sl = slice(512 + hh * HEAD_DIM, 512 + (hh + 1) * HEAD_DIM)
        ret = hf_ref[:, sl].astype(F32) + hb_ref[:, sl].astype(F32)
        rg = og_ref[:, sl].astype(F32)
        parts.append((rg * _sigmoid(rg)) * _head_ln(ret, retg_ref[:, hh * HEAD_DIM:(hh + 1) * HEAD_DIM]))
    merged = jnp.concatenate(parts, axis=1).astype(BF16)
    h_in = _lat_or_ctx(x_ref, ctx_ref, lat_tiles)
    _post_tail(merged, h_in, mod_ref, g2_ref, wout_ref, wr_ref, br_ref, hout_ref, v_ref, idx_ref, gate_ref,
               cnt_ref, base_ref)


def _post_odd_kernel(att_ref, h_ref, mod_ref, g2_ref, wout_ref, wr_ref, br_ref, hout_ref, v_ref, idx_ref, gate_ref,
                     cnt_ref, base_ref):
    _post_tail(att_ref[...], h_ref[...], mod_ref, g2_ref, wout_ref, wr_ref, br_ref, hout_ref, v_ref, idx_ref,
               gate_ref, cnt_ref, base_ref)


def _post_specs(n, tr, mod_idx):
    const = lambda i: (0, 0)
    row = lambda i: (i, 0)
    tail_in = [pl.BlockSpec((tr, D_MODEL), row),
               pl.BlockSpec((None, 8, D_MODEL), lambda i: (mod_idx(i), 0, 0)),
               pl.BlockSpec((1, D_MODEL), const),
               pl.BlockSpec((D_MODEL, D_MODEL), const),
               pl.BlockSpec((D_MODEL, ROUTER_PAD), const),
               pl.BlockSpec((1, ROUTER_PAD), const)]
    out_specs = [pl.BlockSpec((tr, D_MODEL), row),
                 pl.BlockSpec((tr * ROW_TILE_SUBLANES, 128), row),
                 pl.BlockSpec((8, tr), lambda i: (0, i)),
                 pl.BlockSpec((tr, ROUTER_PAD), row),
                 pl.BlockSpec((8, ROUTER_PAD), const)]
    out_shape = [jax.ShapeDtypeStruct((n, D_MODEL), F32),
                 jax.ShapeDtypeStruct((n * ROW_TILE_SUBLANES, 128), F32),
                 jax.ShapeDtypeStruct((8, n), jnp.int32),
                 jax.ShapeDtypeStruct((n, ROUTER_PAD), F32),
                 jax.ShapeDtypeStruct((8, ROUTER_PAD), F32)]
    return tail_in, out_specs, out_shape


def _post_even(hf, hb, proj, mlg, retg, x2d, ctx2d, mod, g2, wout, wr, br, dims):
    B, S, L, tr = dims
    n = hf.shape[0]
    mod_idx, _ = _tile_maps(B, S, L, tr)
    tail_in, out_specs, out_shape = _post_specs(n, tr, mod_idx)
    lat_tiles = x2d.shape[0] // tr
    row = lambda i: (i, 0)
    const = lambda i: (0, 0)
    return pl.pallas_call(
        functools.partial(_post_even_kernel, lat_tiles=lat_tiles),
        grid=(n // tr,),
        in_specs=[pl.BlockSpec((tr, D_MODEL), row),
                  pl.BlockSpec((tr, D_MODEL), row),
                  pl.BlockSpec((tr, D_MODEL), lambda i: (i, EVEN_SCAN_COLS // D_MODEL)),
                  pl.BlockSpec((1, 512), const),
                  pl.BlockSpec((1, 512), const)] + _lat_ctx_specs(tr, lat_tiles) + tail_in[1:],
        out_specs=out_specs,
        out_shape=out_shape,
        scratch_shapes=[pltpu.VMEM((1, ROUTER_PAD), F32)],
        compiler_params=_cparams(("arbitrary",)),
        name="post_even",
    )(hf, hb, proj, mlg, retg, x2d, ctx2d, mod, g2, wout, wr, br)


def _post_odd(att, h, mod, g2, wout, wr, br, dims):
    B, S, L, tr = dims
    n = att.shape[0]
    mod_idx, _ = _tile_maps(B, S, L, tr)
    tail_in, out_specs, out_shape = _post_specs(n, tr, mod_idx)
    return pl.pallas_call(
        _post_odd_kernel,
        grid=(n // tr,),
        in_specs=[pl.BlockSpec((tr, D_MODEL), lambda i: (i, 0))] + tail_in,
        out_specs=out_specs,
        out_shape=out_shape,
        scratch_shapes=[pltpu.VMEM((1, ROUTER_PAD), F32)],
        compiler_params=_cparams(("arbitrary",)),
        name="post_odd",
    )(att, h, mod, g2, wout, wr, br)


SUM_ROWS = 16
ATTN_TILE = 1024


def _attn_kernel(q_ref, kl_ref, kc_ref, vl_ref, vc_ref, lam_ref, g_ref, o_ref,
                 acc_ref, m_ref, cmax_ref, qm_ref, sa_ref, sb_ref, sc_ref, *, tk, lam_init):
    n_lat = kl_ref.shape[0] // tk
    q = q_ref[...]
    lane = lax.broadcasted_iota(jnp.int32, q.shape, 1)
    zero = jnp.zeros_like(q)
    qm_ref[0] = jnp.where(lane < DA_DHEAD, q, zero)
    qm_ref[1] = jnp.where(lane >= DA_DHEAD, q, zero)
    acc_ref[...] = jnp.zeros_like(acc_ref)
    m_ref[...] = jnp.full(m_ref.shape, NEG_BIG, F32)

    def lat_k(c):
        return kl_ref[pl.ds(pl.multiple_of(c * tk, tk), tk), :]

    def lat_v(c):
        return vl_ref[:, pl.ds(pl.multiple_of(c * tk, tk), tk)]

    def scores(dst_ref, slot, kc):
        for mi in range(2):
            st = lax.dot_general(kc, qm_ref[mi], _NT, preferred_element_type=F32)
            dst_ref[mi] = st
            cmax_ref[slot, mi] = jnp.max(st, axis=0, keepdims=True)

    def absorb(src_ref, slot, vtc):
        vext = jnp.concatenate([vtc, jnp.ones((SUM_ROWS, vtc.shape[1]), BF16)], axis=0)
        for mi in range(2):
            m_old = m_ref[mi]
            m_new = jnp.maximum(m_old, cmax_ref[slot, mi])
            p = jnp.exp2(src_ref[mi] - m_new).astype(BF16)
            acc_ref[mi] = jnp.exp2(m_old - m_new) * acc_ref[mi] + jnp.dot(vext, p, preferred_element_type=F32)
            m_ref[mi] = m_new

    scores(sa_ref, 0, lat_k(0))

    def body(j, carry):
        c = 2 * j
        scores(sb_ref, 1, lat_k(c + 1))
        absorb(sa_ref, 0, lat_v(c))
        scores(sa_ref, 0, lat_k(c + 2))
        absorb(sb_ref, 1, lat_v(c + 1))
        return carry

    lax.fori_loop(0, n_lat // 2 - 1, body, 0)
    scores(sb_ref, 1, lat_k(n_lat - 1))
    absorb(sa_ref, 0, lat_v(n_lat - 2))
    scores(sc_ref, 2, kc_ref[...])
    absorb(sb_ref, 1, lat_v(n_lat - 1))
    absorb(sc_ref, 2, vc_ref[...])

    lp = lam_ref[...]
    lam = (jnp.exp(jnp.sum(lp[0:1, :] * lp[1:2, :], axis=1, keepdims=True))
           - jnp.exp(jnp.sum(lp[2:3, :] * lp[3:4, :], axis=1, keepdims=True)) + lam_init)
    o0 = acc_ref[0, 0:128, :] / acc_ref[0, 128:129, :]
    o1 = acc_ref[1, 0:128, :] / acc_ref[1, 128:129, :]
    ot = o0 - lam * o1
    ot = ot * lax.rsqrt(jnp.mean(ot * ot, axis=0, keepdims=True) + EPS)
    o = ot.T * g_ref[...] * (1.0 - lam_init)
    o_ref[...] = o.astype(BF16)


def _attention(qk, vt, lam_p, norm_g, lam_init, B, S, L):
    tq = ATTN_TILE
    tk = ATTN_TILE
    assert S % tq == 0 and S % (2 * tk) == 0
    nq = S // tq
    nh = DA_HEADS
    kcol = nh
    ctx0 = B * S // L
    kern = functools.partial(_attn_kernel, tk=tk, lam_init=lam_init)
    return pl.pallas_call(
        kern,
        grid=(B, nh, nq),
        in_specs=[pl.BlockSpec((tq, 128), lambda b, h, i: (b * nq + i, h)),
                  pl.BlockSpec((S, 128), lambda b, h, i: (b, kcol + h)),
                  pl.BlockSpec((L, 128), lambda b, h, i: (ctx0 + b, kcol + h)),
                  pl.BlockSpec((128, S), lambda b, h, i: (h, b)),
                  pl.BlockSpec((128, L), lambda b, h, i: (h, ctx0 + b)),
                  pl.BlockSpec((4, DA_DHEAD), lambda b, h, i: (0, 0)),
                  pl.BlockSpec((None, 1, 128), lambda b, h, i: (h, 0, 0))],
        out_specs=pl.BlockSpec((tq, 128), lambda b, h, i: (b * nq + i, h)),
        out_shape=jax.ShapeDtypeStruct((B * S, D_MODEL), BF16),
        scratch_shapes=[pltpu.VMEM((2, 128 + SUM_ROWS, tq), F32),
                        pltpu.VMEM((2, 1, tq), F32),
                        pltpu.VMEM((3, 2, 1, tq), F32),
                        pltpu.VMEM((2, tq, 128), BF16),
                        pltpu.VMEM((2, tk, tq), F32),
                        pltpu.VMEM((2, tk, tq), F32),
                        pltpu.VMEM((2, L, tq), F32)],
        compiler_params=_cparams(("arbitrary", "arbitrary", "arbitrary")),
        name="diff_attention",
    )(qk, qk, qk, vt, vt, lam_p, norm_g.reshape(nh, 1, 128))


def _expert_kernel(be_ref, nu_ref, tok_ref, tokn_ref, v_hbm, w1_ref, w3_ref, w2_ref, y_ref,
                   w1b_ref, w3b_ref, w2b_ref, xbuf_ref, sem_ref):
    i = pl.program_id(0)
    n_used = nu_ref[0]
    tile = ROW_TILE_SUBLANES

    def row_copy(idx_ref, r, slot):
        src = v_hbm.at[pl.ds(pl.multiple_of(idx_ref[0, r], tile), tile), :]
        return pltpu.make_async_copy(src, xbuf_ref.at[slot, pl.ds(r * tile, tile), :], sem_ref.at[slot])

    def start_gather(idx_ref, slot):
        for r in range(MOE_BLOCK):
            row_copy(idx_ref, r, slot).start(priority=r % 2)

    def wait_gather(slot):
        pltpu.make_async_copy(v_hbm.at[pl.ds(0, MOE_BLOCK * tile), :], xbuf_ref.at[slot], sem_ref.at[slot]).wait()

    @pl.when(i == 0)
    def _first_block():
        start_gather(tok_ref, 0)

    @pl.when((i == 0) | (be_ref[i] != be_ref[jnp.maximum(i - 1, 0)]))
    def _new_expert():
        w1b_ref[...] = w1_ref[...].astype(BF16)
        w3b_ref[...] = w3_ref[...].astype(BF16)
        w2b_ref[...] = w2_ref[...].astype(BF16)

    for slot in range(2):
        @pl.when((i % 2 == slot) & (i < n_used))
        def _compute():
            @pl.when(i + 1 < n_used)
            def _next_block():
                start_gather(tokn_ref, 1 - slot)

            wait_gather(slot)
            x = jnp.concatenate([xbuf_ref[slot, pl.ds(s, MOE_BLOCK, stride=tile), :] for s in range(tile)],
                                axis=1).astype(BF16)
            a = jnp.dot(x, w1b_ref[...], preferred_element_type=F32)
            b = jnp.dot(x, w3b_ref[...], preferred_element_type=F32)
            hm = ((a * _sigmoid(a)) * b).astype(BF16)
            y_ref[...] = jnp.dot(hm, w2b_ref[...], preferred_element_type=F32).astype(BF16)

    @pl.when(i >= n_used)
    def _skip():
        y_ref[...] = jnp.zeros_like(y_ref)


def _experts(blk_expert, n_used, buf_row, v_tiles, w1, w3, w2, layer):
    nblk = blk_expert.shape[0]
    tok3 = buf_row.reshape(nblk, 1, MOE_BLOCK)
    wmap = lambda i, be, nu: (layer, be[i], 0, 0)
    grid_spec = pltpu.PrefetchScalarGridSpec(
        num_scalar_prefetch=2,
        grid=(nblk,),
        in_specs=[pl.BlockSpec((None, 1, MOE_BLOCK), lambda i, be, nu: (i, 0, 0), memory_space=pltpu.SMEM),
                  pl.BlockSpec((None, 1, MOE_BLOCK), lambda i, be, nu: (jnp.minimum(i + 1, nblk - 1), 0, 0),
                               memory_space=pltpu.SMEM),
                  pl.BlockSpec(memory_space=pl.ANY),
                  pl.BlockSpec((None, None, D_MODEL, D_EXPERT), wmap),
                  pl.BlockSpec((None, None, D_MODEL, D_EXPERT), wmap),
                  pl.BlockSpec((None, None, D_EXPERT, D_MODEL), wmap)],
        out_specs=pl.BlockSpec((MOE_BLOCK, D_MODEL), lambda i, be, nu: (i, 0)),
        scratch_shapes=[pltpu.VMEM((D_MODEL, D_EXPERT), BF16),
                        pltpu.VMEM((D_MODEL, D_EXPERT), BF16),
                        pltpu.VMEM((D_EXPERT, D_MODEL), BF16),
                        pltpu.VMEM((2, MOE_BLOCK * ROW_TILE_SUBLANES, 128), F32),
                        pltpu.SemaphoreType.DMA((2,))],
    )
    return pl.pallas_call(
        _expert_kernel,
        grid_spec=grid_spec,
        out_shape=jax.ShapeDtypeStruct((nblk * MOE_BLOCK, D_MODEL), BF16),
        compiler_params=_cparams(("arbitrary",)),
        name="experts",
    )(blk_expert, n_used, tok3, tok3, v_tiles, w1, w3, w2)


def _route_tile(lg, idx_ref, gate_ref, cnt_ref, base_ref):
    tr = lg.shape[0]

    @pl.when(pl.program_id(0) == 0)
    def _init():
        base_ref[...] = jnp.zeros_like(base_ref)

    lane = lax.broadcasted_iota(jnp.int32, lg.shape, 1)
    first_arg = lambda x, mx: jnp.min(jnp.where(x == mx, lane, 128), axis=1, keepdims=True)
    is_g = lane < N_GROUPS
    gl = jnp.where(is_g, lg, NEG_BIG)
    gmax = jnp.max(gl, axis=1, keepdims=True)
    pg_top = 1.0 / jnp.sum(jnp.where(is_g, jnp.exp(gl - gmax), 0.0), axis=1, keepdims=True)
    grp = first_arg(gl, gmax)
    e_lo = N_GROUPS + grp * EXPERTS_PER_GROUP
    el = jnp.where((lane >= e_lo) & (lane < e_lo + EXPERTS_PER_GROUP), lg, NEG_BIG)
    m1 = jnp.max(el, axis=1, keepdims=True)
    i1 = first_arg(el, m1)
    el2 = jnp.where(lane == i1, NEG_BIG, el)
    m2 = jnp.max(el2, axis=1, keepdims=True)
    i2 = first_arg(el2, m2)
    p2 = jnp.exp(m2 - m1)
    g1 = pg_top / (1.0 + p2)
    g2 = pg_top * p2 / (1.0 + p2)

    hit1 = lane == i1
    hit2 = lane == i2
    onehot = jnp.where(hit1 | hit2, 1.0, 0.0)
    r_i = lax.broadcasted_iota(jnp.int32, (tr, tr), 0)
    c_i = lax.broadcasted_iota(jnp.int32, (tr, tr), 1)
    earlier = jnp.where(c_i < r_i, 1.0, 0.0).astype(BF16)
    before = jnp.dot(earlier, onehot.astype(BF16), preferred_element_type=F32) + base_ref[...]
    rank1 = jnp.sum(jnp.where(hit1, before, 0.0), axis=1, keepdims=True).astype(jnp.int32)
    rank2 = jnp.sum(jnp.where(hit2, before, 0.0), axis=1, keepdims=True).astype(jnp.int32)
    base = base_ref[...] + jnp.sum(onehot, axis=0, keepdims=True)
    base_ref[...] = base
    cnt_ref[...] = jnp.broadcast_to(base, cnt_ref.shape)

    idx = jnp.where(lane == 0, i1 - N_GROUPS,
                    jnp.where(lane == 1, i2 - N_GROUPS,
                              jnp.where(lane == 2, rank1, jnp.where(lane == 3, rank2, 0))))
    idx_ref[...] = idx.T[0:8, :]
    gate_ref[...] = jnp.where(lane == 0, g1, jnp.where(lane == 1, g2, 0.0))


def _block_layout(idx, gate, cnt):
    n = idx.shape[1]
    expert = idx[0:TOP_K]
    rank = idx[TOP_K:2 * TOP_K]
    counts = cnt[0, N_GROUPS:N_GROUPS + N_EXPERTS].astype(jnp.int32)
    a = n * TOP_K
    nblk = -(-a // MOE_BLOCK) + N_EXPERTS
    padded = (counts + MOE_BLOCK - 1) // MOE_BLOCK * MOE_BLOCK
    pend = jnp.cumsum(padded)
    start = pend - padded
    cstart = jnp.cumsum(counts) - counts
    hit = expert[:, None, :] == jnp.arange(N_EXPERTS, dtype=jnp.int32)[None, :, None]
    dest = jnp.sum(jnp.where(hit, start[None, :, None], 0), axis=1) + rank
    tok = jnp.broadcast_to(jnp.arange(n, dtype=jnp.int32)[None, :], (TOP_K, n))
    _, by_row = lax.sort_key_val(dest.reshape(-1), tok.reshape(-1))
    n_used = (pend[-1] // MOE_BLOCK).astype(jnp.int32)
    blk_row = jnp.minimum(jnp.arange(nblk, dtype=jnp.int32), n_used - 1) * MOE_BLOCK
    blk_expert = jnp.clip(jnp.sum((pend[None, :] <= blk_row[:, None]).astype(jnp.int32), axis=1), 0, N_EXPERTS - 1)
    row = jnp.arange(nblk * MOE_BLOCK, dtype=jnp.int32).reshape(nblk, MOE_BLOCK)
    in_expert = row - start[blk_expert][:, None]
    src = jnp.clip(in_expert + cstart[blk_expert][:, None], 0, a - 1)
    live = (in_expert < counts[blk_expert][:, None]) & (jnp.arange(nblk)[:, None] < n_used)
    buf_tok = jnp.where(live, by_row.at[src].get(mode="promise_in_bounds"), 0).reshape(-1)
    return buf_tok, blk_expert.astype(jnp.int32), n_used.reshape(1), dest, gate


def _moe(v, routing, w1, w3, w2, layer):
    buf_tok, blk_expert, n_used, pos, gate = _block_layout(*routing)
    yb = _experts(blk_expert, n_used, buf_tok * ROW_TILE_SUBLANES, v, w1, w3, w2, layer)
    y0 = yb.at[pos[0]].get(mode="promise_in_bounds")
    y1 = yb.at[pos[1]].get(mode="promise_in_bounds")
    return y0, y1, gate


def _final_kernel(h_ref, y0_ref, y1_ref, gk_ref, modp_ref, g_ref, o_ref):
    h = _combine(h_ref[...], y0_ref[...], y1_ref[...], gk_ref[...], modp_ref[5:6, :])
    o_ref[...] = _rms_rows(h) * g_ref[...]


def _final(h, y0, y1, gk, modp, g, dims):
    B, S, L, tr = dims
    n = h.shape[0]
    mod_idx, _ = _tile_maps(B, S, L, tr)
    row = lambda i: (i, 0)
    return pl.pallas_call(
        _final_kernel,
        grid=(n // tr,),
        in_specs=[pl.BlockSpec((tr, D_MODEL), row),
                  pl.BlockSpec((tr, D_MODEL), row),
                  pl.BlockSpec((tr, D_MODEL), row),
                  pl.BlockSpec((tr, ROUTER_PAD), row),
                  pl.BlockSpec((None, 8, D_MODEL), lambda i: (mod_idx(i), 0, 0)),
                  pl.BlockSpec((1, D_MODEL), lambda i: (0, 0))],
        out_specs=pl.BlockSpec((tr, D_MODEL), row),
        out_shape=jax.ShapeDtypeStruct((n, D_MODEL), F32),
        compiler_params=_cparams(("arbitrary",)),
        name="final_norm",
    )(h, y0, y1, gk, modp, g)


def _rotary_tables_even(S, L, B):
    nf = HEAD_DIM // 2
    inv = ROPE_BASE ** (-jnp.arange(nf, dtype=F32) / nf)
    pos = jnp.concatenate([L + jnp.arange(S), jnp.tile(jnp.arange(L), B)]).astype(F32)
    ang = pos[:, None] * inv[None, :]
    cos, sin = jnp.cos(ang), jnp.sin(ang)
    return jnp.concatenate([cos, cos], 1), jnp.concatenate([-sin, sin], 1)


def _rotary_tables_odd(S, L, B):
    nf = DA_DHEAD // 4
    inv = ROPE_BASE ** (-jnp.arange(nf, dtype=F32) / nf)
    t = jnp.arange(S)
    rows, cols = (t // GRID_W).astype(F32), (t % GRID_W).astype(F32)
    j = np.arange(128)
    f_idx = j % nf
    use_col = (j % DA_DHEAD) >= DA_DHEAD // 2
    first = (j % (2 * nf)) < nf
    ang = jnp.where(use_col[None, :], cols[:, None], rows[:, None]) * inv[f_idx][None, :]
    cos, sin = jnp.cos(ang), jnp.sin(ang)
    c_lat = cos
    s1_lat = jnp.where(first[None, :], -sin, 0.0)
    s2_lat = jnp.where(first[None, :], 0.0, sin)
    nctx = B * L
    c = jnp.concatenate([c_lat, jnp.ones((nctx, 128), F32)])
    s1 = jnp.concatenate([s1_lat, jnp.zeros((nctx, 128), F32)])
    s2 = jnp.concatenate([s2_lat, jnp.zeros((nctx, 128), F32)])
    return c, s1, s2


def kernel(x, c, ctx, c_ctx, w_mod, b_mod, norm1_g, norm2_g, w_in_even, ml_gate_b, ml_norm_g, ret_log_decay, ret_norm_g, w_out_even, w_in_odd, da_lambda, da_norm_g, w_out_odd, router_g_w, router_g_b, router_e_w, router_e_b, w1, w3, w2, final_norm_g):
    B, S, D = x.shape
    L = ctx.shape[1]
    depth = w_mod.shape[0]
    assert D == D_MODEL and depth == 2 and S % CHUNK == 0 and L % CHUNK == 0 and S % L == 0
    n_lat, n_ctx = B * S, B * L
    tr = _row_tile(S, n_ctx)
    dims = (B, S, L, tr)

    x2d, ctx2d = x.reshape(n_lat, D), ctx.reshape(n_ctx, D)

    cond8 = jnp.zeros((8, D), F32).at[:B].set(c).at[B].set(c_ctx)
    mod = jnp.pad(_modulation(cond8, w_mod, b_mod).reshape(depth, 8, 6, D), ((0, 0), (0, 0), (0, 2), (0, 0)))

    def router_w(l):
        wr = jnp.zeros((D, ROUTER_PAD), F32)
        wr = wr.at[:, :N_GROUPS].set(router_g_w[l]).at[:, N_GROUPS:N_GROUPS + N_EXPERTS].set(router_e_w[l])
        br = jnp.zeros((1, ROUTER_PAD), F32)
        br = br.at[0, :N_GROUPS].set(router_g_b[l]).at[0, N_GROUPS:N_GROUPS + N_EXPERTS].set(router_e_b[l])
        return wr, br

    wi = w_in_even[0]
    mq, mk, mv, mo, mi, mf, rq, rk, rv, rg = jnp.split(wi, np.cumsum(
        [512, 512, 512, 512, 8, 8, 512, 512, 512])[:9].tolist(), axis=1)
    w_main = jnp.concatenate([mq, mk, mv, rq, rk, rv, mo, rg], 1).astype(BF16)
    w_gates = jnp.pad(jnp.concatenate([mi, mf], 1), ((0, 0), (0, GATE_PAD - 16))).astype(BF16)
    gate_bias = jnp.pad(jnp.concatenate([ml_gate_b[0][:, 0].reshape(-1), ml_gate_b[0][:, 1].reshape(-1)]),
                        (0, GATE_PAD - 16)).reshape(1, GATE_PAD)
    cs, sn = _rotary_tables_even(S, L, B)
    proj, gates = _proj_even(x2d, ctx2d, mod[0], norm1_g[0].reshape(1, D), w_main, w_gates, cs, sn, dims)
    hf, hb = _scan(ret_log_decay[0].reshape(-1), proj, gates, gate_bias, B, S, L)
    wr, br = router_w(0)
    h, v, *routing = _post_even(hf, hb, proj, ml_norm_g[0].reshape(1, -1), ret_norm_g[0].reshape(1, -1),
                              x2d, ctx2d, mod[0], norm2_g[0].reshape(1, D), w_out_even[0].astype(BF16), wr, br, dims)
    y0, y1, gk = _moe(v, routing, w1, w3, w2, 0)

    lam_init = 0.8 - 0.6 * math.exp(-0.3 * 1)
    nqk = 4 * DA_HEADS * DA_DHEAD
    w_qk = w_in_odd[0][:, :nqk].astype(BF16)
    w_vt = w_in_odd[0][:, nqk:].T.astype(BF16)
    c2, s1, s2 = _rotary_tables_odd(S, L, B)
    h, qk, vt = _proj_odd(h, y0, y1, gk, mod[0], mod[1], norm1_g[1].reshape(1, D), w_qk, w_vt, c2, s1, s2, dims)
    att = _attention(qk, vt, da_lambda[0], da_norm_g[0], lam_init, B, S, L)
    wr, br = router_w(1)
    h_lat, v, *routing = _post_odd(att, h, mod[1], norm2_g[1].reshape(1, D), w_out_odd[0].astype(BF16), wr, br, dims)
    y0, y1, gk = _moe(v, routing, w1, w3, w2, 1)
    out = _final(h_lat, y0, y1, gk, mod[1], final_norm_g.reshape(1, D), dims)
    return out.reshape(B, S, D)
```

```python
import functools
import math

import numpy as np
import jax
import jax.numpy as jnp
from jax import lax
from jax.experimental import pallas as pl
from jax.experimental.pallas import tpu as pltpu

F32 = jnp.float32
BF16 = jnp.bfloat16

D_MODEL = 1024
GRID_W = 64
CHUNK = 128
EPS = 1e-6
ROPE_BASE = 10000.0
ML_HEADS = 4
RET_HEADS = 4
HEAD_DIM = 128
DA_HEADS = 8
DA_DHEAD = 64
N_GROUPS = 4
EXPERTS_PER_GROUP = 8
N_EXPERTS = N_GROUPS * EXPERTS_PER_GROUP
TOP_K = 2
D_EXPERT = D_MODEL // 2
MOE_BLOCK = 256
ROW_TILE_SUBLANES = D_MODEL // 128
ROUTER_PAD = 128
GATE_PAD = 128
EVEN_MAIN = 4096
EVEN_SCAN_COLS = 3072
NEG_BIG = -1e30

V7X_VMEM_LIMIT = 56 * 1024 * 1024


def _cparams(sem):
    return pltpu.CompilerParams(dimension_semantics=sem, vmem_limit_bytes=V7X_VMEM_LIMIT)


def _row_tile(n_lat_per_batch, n_ctx_total):
    for t in (512, 256, 128):
        if n_lat_per_batch % t == 0 and n_ctx_total % t == 0:
            return t
    raise ValueError("row counts must be multiples of 128")


def _split3(a):
    hi = a.astype(BF16)
    r1 = a - hi.astype(F32)
    mid = r1.astype(BF16)
    lo = (r1 - mid.astype(F32)).astype(BF16)
    return hi, mid, lo


def _dot_hi(a, b):
    ah, al, _ = _split3(a)
    bh, bl, _ = _split3(b)
    d = functools.partial(jnp.dot, preferred_element_type=F32)
    return d(ah, bh) + (d(ah, bl) + d(al, bh))


def _sigmoid(x):
    return 1.0 / (1.0 + jnp.exp(-x))


def _rms_rows(x):
    return x * lax.rsqrt(jnp.mean(x * x, axis=-1, keepdims=True) + EPS)


def _mod_kernel(c_ref, w_ref, b_ref, o_ref):
    c = c_ref[...]
    o_ref[...] = _dot_hi(c * _sigmoid(c), w_ref[...]) + b_ref[...]


def _modulation(cond8, w_mod, b_mod):
    depth = w_mod.shape[0]
    ncol = w_mod.shape[2]
    tn = 1024
    return pl.pallas_call(
        _mod_kernel,
        grid=(depth, ncol // tn),
        in_specs=[pl.BlockSpec((8, D_MODEL), lambda l, j: (0, 0)),
                  pl.BlockSpec((None, D_MODEL, tn), lambda l, j: (l, 0, j)),
                  pl.BlockSpec((None, 1, tn), lambda l, j: (l, 0, j))],
        out_specs=pl.BlockSpec((None, 8, tn), lambda l, j: (l, 0, j)),
        out_shape=jax.ShapeDtypeStruct((depth, 8, ncol), F32),
        compiler_params=_cparams(("arbitrary", "arbitrary")),
        name="modulation",
    )(cond8, w_mod, b_mod.reshape(depth, 1, ncol))


def _adaln_bf16(x, g, shift, scale):
    return ((_rms_rows(x) * g) * (1.0 + scale) + shift).astype(BF16)


def _lat_or_ctx(x_ref, ctx_ref, lat_tiles):
    return jnp.where(pl.program_id(0) < lat_tiles, x_ref[...], ctx_ref[...])


def _proj_even_kernel(x_ref, ctx_ref, mod_ref, g_ref, w_ref, wg_ref, cs_ref, sn_ref, proj_ref, gates_ref,
                      *, lat_tiles):
    u = _adaln_bf16(_lat_or_ctx(x_ref, ctx_ref, lat_tiles), g_ref[...], mod_ref[0:1, :], mod_ref[1:2, :])
    cs = cs_ref[...]
    sn = sn_ref[...]
    kscale = HEAD_DIM ** -0.5
    for gi in range(EVEN_MAIN // 512):
        acc = jnp.dot(u, w_ref[:, gi * 512:(gi + 1) * 512], preferred_element_type=F32)
        if gi in (3, 4):
            parts = []
            for hh in range(RET_HEADS):
                blk = acc[:, hh * HEAD_DIM:(hh + 1) * HEAD_DIM]
                parts.append(blk * cs + pltpu.roll(blk, HEAD_DIM // 2, 1) * sn)
            acc = jnp.concatenate(parts, axis=1)
        if gi in (1, 4):
            acc = acc * kscale
        proj_ref[:, gi * 512:(gi + 1) * 512] = acc.astype(BF16)
    gates_ref[...] = jnp.dot(u, wg_ref[...], preferred_element_type=F32)


def _combine(h, y0, y1, gk, gate):
    moe = gk[:, 0:1] * y0.astype(F32) + gk[:, 1:2] * y1.astype(F32)
    return h + gate * moe


def _proj_odd_kernel(h_ref, y0_ref, y1_ref, gk_ref, modp_ref, mod_ref, g_ref, w_ref, wvt_ref,
                     c_ref, s1_ref, s2_ref, hout_ref, qk_ref, vt_ref):
    h = _combine(h_ref[...], y0_ref[...], y1_ref[...], gk_ref[...], modp_ref[5:6, :])
    hout_ref[...] = h
    u = _adaln_bf16(h, g_ref[...], mod_ref[0:1, :], mod_ref[1:2, :])
    c = c_ref[...]
    s1 = s1_ref[...]
    s2 = s2_ref[...]
    qscale = DA_DHEAD ** -0.5 * math.log2(math.e)
    nq = DA_HEADS * 2 * DA_DHEAD
    for gi in range(2 * nq // 512):
        acc = jnp.dot(u, w_ref[:, gi * 512:(gi + 1) * 512], preferred_element_type=F32)
        parts = []
        for hh in range(512 // 128):
            blk = acc[:, hh * 128:(hh + 1) * 128]
            parts.append(blk * c + pltpu.roll(blk, 128 - 16, 1) * s1 + pltpu.roll(blk, 16, 1) * s2)
        acc = jnp.concatenate(parts, axis=1)
        if gi * 512 < nq:
            acc = acc * qscale
        qk_ref[:, gi * 512:(gi + 1) * 512] = acc.astype(BF16)
    vt = lax.dot_general(wvt_ref[...], u, (((1,), (1,)), ((), ())), preferred_element_type=F32)
    vt_ref[...] = vt.astype(BF16)


def _tile_maps(B, S, L, tr):
    lat_tiles = B * S // tr
    per_b = S // tr

    def mod_idx(i):
        return jnp.where(i < lat_tiles, i // per_b, B)

    def rot_idx(i):
        return jnp.where(i < lat_tiles, i % per_b, per_b + (i - lat_tiles))

    return mod_idx, rot_idx


def _lat_ctx_specs(tr, lat_tiles):
    return [pl.BlockSpec((tr, D_MODEL), lambda i: (jnp.minimum(i, lat_tiles - 1), 0)),
            pl.BlockSpec((tr, D_MODEL), lambda i: (jnp.maximum(i - lat_tiles, 0), 0))]


def _proj_even(x2d, ctx2d, mod, g, w_main, w_gates, cs, sn, dims):
    B, S, L, tr = dims
    n = x2d.shape[0] + ctx2d.shape[0]
    lat_tiles = x2d.shape[0] // tr
    mod_idx, rot_idx = _tile_maps(B, S, L, tr)
    const = lambda i: (0, 0)
    return pl.pallas_call(
        functools.partial(_proj_even_kernel, lat_tiles=lat_tiles),
        grid=(n // tr,),
        in_specs=_lat_ctx_specs(tr, lat_tiles) + [
                  pl.BlockSpec((None, 8, D_MODEL), lambda i: (mod_idx(i), 0, 0)),
                  pl.BlockSpec((1, D_MODEL), const),
                  pl.BlockSpec((D_MODEL, EVEN_MAIN), const),
                  pl.BlockSpec((D_MODEL, GATE_PAD), const),
                  pl.BlockSpec((tr, 128), lambda i: (rot_idx(i), 0)),
                  pl.BlockSpec((tr, 128), lambda i: (rot_idx(i), 0))],
        out_specs=[pl.BlockSpec((tr, EVEN_MAIN), lambda i: (i, 0)),
                   pl.BlockSpec((tr, GATE_PAD), lambda i: (i, 0))],
        out_shape=[jax.ShapeDtypeStruct((n, EVEN_MAIN), BF16),
                   jax.ShapeDtypeStruct((n, GATE_PAD), F32)],
        compiler_params=_cparams(("arbitrary",)),
        name="proj_even",
    )(x2d, ctx2d, mod, g, w_main, w_gates, cs, sn)


def _proj_odd(h, y0, y1, gk, modp, mod, g, w_qk, w_vt, c, s1, s2, dims):
    B, S, L, tr = dims
    n = h.shape[0]
    mod_idx, rot_idx = _tile_maps(B, S, L, tr)
    const = lambda i: (0, 0)
    row = lambda i: (i, 0)
    nqk = w_qk.shape[1]
    return pl.pallas_call(
        _proj_odd_kernel,
        grid=(n // tr,),
        in_specs=[pl.BlockSpec((tr, D_MODEL), row),
                  pl.BlockSpec((tr, D_MODEL), row),
                  pl.BlockSpec((tr, D_MODEL), row),
                  pl.BlockSpec((tr, ROUTER_PAD), row),
                  pl.BlockSpec((None, 8, D_MODEL), lambda i: (mod_idx(i), 0, 0)),
                  pl.BlockSpec((None, 8, D_MODEL), lambda i: (mod_idx(i), 0, 0)),
                  pl.BlockSpec((1, D_MODEL), const),
                  pl.BlockSpec((D_MODEL, nqk), const),
                  pl.BlockSpec((D_MODEL, D_MODEL), const),
                  pl.BlockSpec((tr, 128), lambda i: (rot_idx(i), 0)),
                  pl.BlockSpec((tr, 128), lambda i: (rot_idx(i), 0)),
                  pl.BlockSpec((tr, 128), lambda i: (rot_idx(i), 0))],
        out_specs=[pl.BlockSpec((tr, D_MODEL), row),
                   pl.BlockSpec((tr, nqk), row),
                   pl.BlockSpec((D_MODEL, tr), lambda i: (0, i))],
        out_shape=[jax.ShapeDtypeStruct((n, D_MODEL), F32),
                   jax.ShapeDtypeStruct((n, nqk), BF16),
                   jax.ShapeDtypeStruct((D_MODEL, n), BF16)],
        compiler_params=_cparams(("arbitrary",)),
        name="proj_odd",
    )(h, y0, y1, gk, modp, mod, g, w_qk, w_vt, c, s1, s2)


_NT = (((1,), (1,)), ((), ()))
_TN = (((0,), (0,)), ((), ()))


def _scan_kernel(ld_ref, xf_ref, xb_ref, gf_ref, gb_ref, gfn_ref, gbn_ref, bias_ref, hf_ref, hb_ref,
                 ct_ref, n_ref, m_ref, r_ref, intra_ref, qd_ref, kd_ref, prep_ref):
    t = pl.program_id(1)
    nh = ML_HEADS
    row_i = lax.broadcasted_iota(jnp.int32, (CHUNK, CHUNK), 0)
    col_i = lax.broadcasted_iota(jnp.int32, (CHUNK, CHUNK), 1)
    row_f = row_i.astype(F32)
    col_f = col_i.astype(F32)
    bias = bias_ref[...]
    dot = functools.partial(jnp.dot, preferred_element_type=F32)
    dg = functools.partial(lax.dot_general, preferred_element_type=F32)

    def seen_mask(d):
        return (col_i <= row_i) if d == 0 else (col_i >= row_i)

    def gate_prep(g_ref, d, slot):
        G = g_ref[...] + bias
        LF = jnp.minimum(G, 0.0) - jnp.log1p(jnp.exp(-jnp.abs(G)))
        tri = jnp.where(seen_mask(d), 1.0, 0.0).astype(BF16)
        l_hi, l_mid, l_lo = _split3(LF)
        Bc = dot(tri, l_hi) + (dot(tri, l_mid) + dot(tri, l_lo))
        prep_ref[slot, d, 0] = G
        prep_ref[slot, d, 1] = Bc
        prep_ref[slot, d, 2] = G.T
        prep_ref[slot, d, 3] = Bc.T

    @pl.when(t == 0)
    def _init():
        ct_ref[...] = jnp.zeros_like(ct_ref)
        n_ref[...] = jnp.zeros_like(n_ref)
        m_ref[...] = jnp.zeros_like(m_ref)
        r_ref[...] = jnp.zeros_like(r_ref)
        gate_prep(gf_ref, 0, 0)
        gate_prep(gb_ref, 1, 0)
        for d in range(2):
            for hh in range(nh):
                hd = d * nh + hh
                lg = ld_ref[hd]
                if d == 0:
                    intra = jnp.where(col_i <= row_i, jnp.exp(lg * (row_f - col_f)), 0.0)
                    qd = jnp.exp(lg * (row_f + 1.0))
                    kd = jnp.exp(lg * (CHUNK - 1.0 - row_f))
                else:
                    intra = jnp.where(col_i >= row_i, jnp.exp(lg * (col_f - row_f)), 0.0)
                    qd = jnp.exp(lg * (CHUNK - row_f))
                    kd = jnp.exp(lg * row_f)
                intra_ref[hd] = intra
                qd_ref[hd] = qd
                kd_ref[hd] = kd

    cur = t % 2
    gate_prep(gfn_ref, 0, 1 - cur)
    gate_prep(gbn_ref, 1, 1 - cur)

    for d, (x_ref, o_ref) in enumerate(((xf_ref, hf_ref), (xb_ref, hb_ref))):
        seen = seen_mask(d)
        G = prep_ref[cur, d, 0]
        Bc = prep_ref[cur, d, 1]
        GT = prep_ref[cur, d, 2]
        BT = prep_ref[cur, d, 3]
        last = CHUNK - 1 if d == 0 else 0
        for hh in range(nh):
            hd = d * nh + hh
            c0 = hh * HEAD_DIM
            q = x_ref[:, c0:c0 + HEAD_DIM]
            k = x_ref[:, 512 + c0:512 + c0 + HEAD_DIM]
            v = x_ref[:, 1024 + c0:1024 + c0 + HEAD_DIM]
            ic_col = G[:, hd:hd + 1]
            ic_row = GT[hd:hd + 1, :]
            b_col = Bc[:, 8 + hd:9 + hd]
            b_row = BT[8 + hd:9 + hd, :]
            m0 = m_ref[hd][:, 0:1]
            dlog = jnp.where(seen, b_col - b_row + ic_row, NEG_BIG)
            inter = b_col + m0
            m_t = jnp.maximum(inter, jnp.max(dlog, axis=1, keepdims=True))
            s = dg(q, k, _NT) * jnp.exp(dlog - m_t)
            w_inter = jnp.exp(inter - m_t)
            ct = ct_ref[hd]
            n_row = n_ref[hd]
            numer = dot(s.astype(BF16), v) + w_inter * dot(q, ct.astype(BF16))
            denom = (jnp.sum(s, axis=1, keepdims=True)
                     + w_inter * jnp.sum(q.astype(F32) * n_row, axis=1, keepdims=True))
            hval = numer / jnp.maximum(jnp.abs(denom), jnp.exp(-m_t))
            o_ref[:, c0:c0 + HEAD_DIM] = hval.astype(BF16)
            b_last = b_col[last:last + 1, :]
            w_log_row = b_last - b_row + ic_row
            m_new = jnp.maximum(b_last + m0, jnp.max(w_log_row, axis=1, keepdims=True))
            decay = jnp.exp(b_last + m0 - m_new)
            w_col = jnp.exp(b_last - b_col + ic_col - m_new)
            w_row = jnp.exp(w_log_row - m_new)
            vw = (v.astype(F32) * w_col).astype(BF16)
            ct_ref[hd] = decay * ct + dg(k, vw, _TN)
            w8 = jnp.broadcast_to(w_row, (8, CHUNK)).astype(BF16)
            n_ref[hd] = decay * n_row + dot(w8, k)[0:1, :]
            m_ref[hd] = jnp.broadcast_to(m_new, (1, HEAD_DIM))
            rq = x_ref[:, 1536 + c0:1536 + c0 + HEAD_DIM]
            rk = x_ref[:, 2048 + c0:2048 + c0 + HEAD_DIM]
            rv = x_ref[:, 2560 + c0:2560 + c0 + HEAD_DIM]
            rs = dg(rq, rk, _NT) * intra_ref[hd]
            R = r_ref[hd]
            o = dot(rs.astype(BF16), rv) + qd_ref[hd] * dot(rq, R.astype(BF16))
            o_ref[:, 512 + c0:512 + c0 + HEAD_DIM] = o.astype(BF16)
            kdk = (rk.astype(F32) * kd_ref[hd]).astype(BF16)
            cdec = jnp.exp(ld_ref[hd] * jnp.full((1, HEAD_DIM), float(CHUNK), F32))
            r_ref[hd] = cdec * R + dg(kdk, rv, _TN)


def _scan(ret_ld8, proj, gates, bias, B, S, L):
    n = proj.shape[0]
    nlb, ncb = S // CHUNK, L // CHUNK
    nc = nlb + ncb

    def fwd(b, t):
        return jnp.where(t < ncb, B * nlb + b * ncb + t, b * nlb + t - ncb)

    def bwd(b, t):
        return jnp.where(t < ncb, B * nlb + b * ncb + (ncb - 1 - t), b * nlb + (nlb - 1 - (t - ncb)))

    state = pltpu.VMEM((2 * ML_HEADS, HEAD_DIM, HEAD_DIM), F32)
    vec = pltpu.VMEM((2 * ML_HEADS, 1, HEAD_DIM), F32)
    grid_spec = pltpu.PrefetchScalarGridSpec(
        num_scalar_prefetch=1,
        grid=(B, nc),
        in_specs=[pl.BlockSpec((CHUNK, EVEN_SCAN_COLS), lambda b, t, ld: (fwd(b, t), 0)),
                  pl.BlockSpec((CHUNK, EVEN_SCAN_COLS), lambda b, t, ld: (bwd(b, t), 0)),
                  pl.BlockSpec((CHUNK, GATE_PAD), lambda b, t, ld: (fwd(b, t), 0)),
                  pl.BlockSpec((CHUNK, GATE_PAD), lambda b, t, ld: (bwd(b, t), 0)),
                  pl.BlockSpec((CHUNK, GATE_PAD), lambda b, t, ld: (fwd(b, jnp.minimum(t + 1, nc - 1)), 0)),
                  pl.BlockSpec((CHUNK, GATE_PAD), lambda b, t, ld: (bwd(b, jnp.minimum(t + 1, nc - 1)), 0)),
                  pl.BlockSpec((1, GATE_PAD), lambda b, t, ld: (0, 0))],
        out_specs=[pl.BlockSpec((CHUNK, D_MODEL), lambda b, t, ld: (fwd(b, t), 0)),
                   pl.BlockSpec((CHUNK, D_MODEL), lambda b, t, ld: (bwd(b, t), 0))],
        scratch_shapes=[state, vec, vec, state, state, state, state,
                        pltpu.VMEM((2, 2, 4, CHUNK, CHUNK), F32)],
    )
    return pl.pallas_call(
        _scan_kernel,
        grid_spec=grid_spec,
        out_shape=[jax.ShapeDtypeStruct((n, D_MODEL), BF16),
                   jax.ShapeDtypeStruct((n, D_MODEL), BF16)],
        compiler_params=_cparams(("arbitrary", "arbitrary")),
        name="scan",
    )(ret_ld8, proj, proj, gates, gates, gates, gates, bias)


def _post_tail(merged, h_in, mod_ref, g2_ref, wout_ref, wr_ref, br_ref, hout_ref, v_ref, idx_ref, gate_ref,
               cnt_ref, base_ref):
    y = jnp.dot(merged, wout_ref[...], preferred_element_type=F32)
    h = h_in + mod_ref[2:3, :] * y
    hout_ref[...] = h
    v = (_rms_rows(h) * g2_ref[...]) * (1.0 + mod_ref[4:5, :]) + mod_ref[3:4, :]
    for s in range(ROW_TILE_SUBLANES):
        v_ref[pl.ds(s, v.shape[0], stride=ROW_TILE_SUBLANES), :] = v[:, s * 128:(s + 1) * 128]
    _route_tile(_dot_hi(v, wr_ref[...]) + br_ref[...], idx_ref, gate_ref, cnt_ref, base_ref)


def _head_ln(x, g):
    mu = jnp.mean(x, axis=-1, keepdims=True)
    xc = x - mu
    var = jnp.mean(xc * xc, axis=-1, keepdims=True)
    return xc * lax.rsqrt(var + EPS) * g


def _post_even_kernel(hf_ref, hb_ref, og_ref, mlg_ref, retg_ref, x_ref, ctx_ref, mod_ref, g2_ref, wout_ref,
                      wr_ref, br_ref, hout_ref, v_ref, idx_ref, gate_ref, cnt_ref, base_ref, *, lat_tiles):
    parts = []
    for hh in range(ML_HEADS):
        sl = slice(hh * HEAD_DIM, (hh + 1) * HEAD_DIM)
        ml = hf_ref[:, sl].astype(F32) + hb_ref[:, sl].astype(F32)
        parts.append(_head_ln(_sigmoid(og_ref[:, sl].astype(F32)) * ml, mlg_ref[:, sl]))
    for hh in range(RET_HEADS):
        sl = slice(512 + hh * HEAD_DIM, 512 + (hh + 1) * HEAD_DIM)
        ret = hf_ref[:, sl].astype(F32) + hb_ref[:, sl].astype(F32)
        rg = og_ref[:, sl].astype(F32)
        parts.append((rg * _sigmoid(rg)) * _head_ln(ret, retg_ref[:, hh * HEAD_DIM:(hh + 1) * HEAD_DIM]))
    merged = jnp.concatenate(parts, axis=1).astype(BF16)
    h_in = _lat_or_ctx(x_ref, ctx_ref, lat_tiles)
    _post_tail(merged, h_in, mod_ref, g2_ref, wout_ref, wr_ref, br_ref, hout_ref, v_ref, idx_ref, gate_ref,
               cnt_ref, base_ref)


def _post_odd_kernel(att_ref, h_ref, mod_ref, g2_ref, wout_ref, wr_ref, br_ref, hout_ref, v_ref, idx_ref, gate_ref,
                     cnt_ref, base_ref):
    _post_tail(att_ref[...], h_ref[...], mod_ref, g2_ref, wout_ref, wr_ref, br_ref, hout_ref, v_ref, idx_ref,
               gate_ref, cnt_ref, base_ref)


def _post_specs(n, tr, mod_idx):
    const = lambda i: (0, 0)
    row = lambda i: (i, 0)
    tail_in = [pl.BlockSpec((tr, D_MODEL), row),
               pl.BlockSpec((None, 8, D_MODEL), lambda i: (mod_idx(i), 0, 0)),
               pl.BlockSpec((1, D_MODEL), const),
               pl.BlockSpec((D_MODEL, D_MODEL), const),
               pl.BlockSpec((D_MODEL, ROUTER_PAD), const),
               pl.BlockSpec((1, ROUTER_PAD), const)]
    out_specs = [pl.BlockSpec((tr, D_MODEL), row),
                 pl.BlockSpec((tr * ROW_TILE_SUBLANES, 128), row),
                 pl.BlockSpec((8, tr), lambda i: (0, i)),
                 pl.BlockSpec((tr, ROUTER_PAD), row),
                 pl.BlockSpec((ROUTE_ROWS, ROUTER_PAD), const)]
    out_shape = [jax.ShapeDtypeStruct((n, D_MODEL), F32),
                 jax.ShapeDtypeStruct((n * ROW_TILE_SUBLANES, 128), F32),
                 jax.ShapeDtypeStruct((8, n), jnp.int32),
                 jax.ShapeDtypeStruct((n, ROUTER_PAD), F32),
                 jax.ShapeDtypeStruct((ROUTE_ROWS, ROUTER_PAD), F32)]
    return tail_in, out_specs, out_shape


def _post_even(hf, hb, proj, mlg, retg, x2d, ctx2d, mod, g2, wout, wr, br, dims):
    B, S, L, tr = dims
    n = hf.shape[0]
    mod_idx, _ = _tile_maps(B, S, L, tr)
    tail_in, out_specs, out_shape = _post_specs(n, tr, mod_idx)
    lat_tiles = x2d.shape[0] // tr
    row = lambda i: (i, 0)
    const = lambda i: (0, 0)
    return pl.pallas_call(
        functools.partial(_post_even_kernel, lat_tiles=lat_tiles),
        grid=(n // tr,),
        in_specs=[pl.BlockSpec((tr, D_MODEL), row),
                  pl.BlockSpec((tr, D_MODEL), row),
                  pl.BlockSpec((tr, D_MODEL), lambda i: (i, EVEN_SCAN_COLS // D_MODEL)),
                  pl.BlockSpec((1, 512), const),
                  pl.BlockSpec((1, 512), const)] + _lat_ctx_specs(tr, lat_tiles) + tail_in[1:],
        out_specs=out_specs,
        out_shape=out_shape,
        scratch_shapes=[pltpu.VMEM((ROUTE_ROWS, ROUTER_PAD), F32)],
        compiler_params=_cparams(("arbitrary",)),
        name="post_even",
    )(hf, hb, proj, mlg, retg, x2d, ctx2d, mod, g2, wout, wr, br)


def _post_odd(att, h, mod, g2, wout, wr, br, dims):
    B, S, L, tr = dims
    n = att.shape[0]
    mod_idx, _ = _tile_maps(B, S, L, tr)
    tail_in, out_specs, out_shape = _post_specs(n, tr, mod_idx)
    return pl.pallas_call(
        _post_odd_kernel,
        grid=(n // tr,),
        in_specs=[pl.BlockSpec((tr, D_MODEL), lambda i: (i, 0))] + tail_in,
        out_specs=out_specs,
        out_shape=out_shape,
        scratch_shapes=[pltpu.VMEM((ROUTE_ROWS, ROUTER_PAD), F32)],
        compiler_params=_cparams(("arbitrary",)),
        name="post_odd",
    )(att, h, mod, g2, wout, wr, br)


SUM_ROWS = 16
ATTN_TILE = 1024


def _attn_kernel(q_ref, kl_ref, kc_ref, vl_ref, vc_ref, lam_ref, g_ref, o_ref,
                 acc_ref, m_ref, cmax_ref, qm_ref, sa_ref, sb_ref, sc_ref, *, tk, lam_init):
    n_lat = kl_ref.shape[0] // tk
    q = q_ref[...]
    lane = lax.broadcasted_iota(jnp.int32, q.shape, 1)
    zero = jnp.zeros_like(q)
    qm_ref[0] = jnp.where(lane < DA_DHEAD, q, zero)
    qm_ref[1] = jnp.where(lane >= DA_DHEAD, q, zero)
    acc_ref[...] = jnp.zeros_like(acc_ref)
    m_ref[...] = jnp.full(m_ref.shape, NEG_BIG, F32)

    def lat_k(c):
        return kl_ref[pl.ds(pl.multiple_of(c * tk, tk), tk), :]

    def lat_v(c):
        return vl_ref[:, pl.ds(pl.multiple_of(c * tk, tk), tk)]

    def scores(dst_ref, slot, kc):
        for mi in range(2):
            st = lax.dot_general(kc, qm_ref[mi], _NT, preferred_element_type=F32)
            dst_ref[mi] = st
            cmax_ref[slot, mi] = jnp.max(st, axis=0, keepdims=True)

    def absorb(src_ref, slot, vtc):
        vext = jnp.concatenate([vtc, jnp.ones((SUM_ROWS, vtc.shape[1]), BF16)], axis=0)
        for mi in range(2):
            m_old = m_ref[mi]
            m_new = jnp.maximum(m_old, cmax_ref[slot, mi])
            p = jnp.exp2(src_ref[mi] - m_new).astype(BF16)
            acc_ref[mi] = jnp.exp2(m_old - m_new) * acc_ref[mi] + jnp.dot(vext, p, preferred_element_type=F32)
            m_ref[mi] = m_new

    scores(sa_ref, 0, lat_k(0))

    def body(j, carry):
        c = 2 * j
        scores(sb_ref, 1, lat_k(c + 1))
        absorb(sa_ref, 0, lat_v(c))
        scores(sa_ref, 0, lat_k(c + 2))
        absorb(sb_ref, 1, lat_v(c + 1))
        return carry

    lax.fori_loop(0, n_lat // 2 - 1, body, 0)
    scores(sb_ref, 1, lat_k(n_lat - 1))
    absorb(sa_ref, 0, lat_v(n_lat - 2))
    scores(sc_ref, 2, kc_ref[...])
    absorb(sb_ref, 1, lat_v(n_lat - 1))
    absorb(sc_ref, 2, vc_ref[...])

    lp = lam_ref[...]
    lam = (jnp.exp(jnp.sum(lp[0:1, :] * lp[1:2, :], axis=1, keepdims=True))
           - jnp.exp(jnp.sum(lp[2:3, :] * lp[3:4, :], axis=1, keepdims=True)) + lam_init)
    o0 = acc_ref[0, 0:128, :] / acc_ref[0, 128:129, :]
    o1 = acc_ref[1, 0:128, :] / acc_ref[1, 128:129, :]
    ot = o0 - lam * o1
    ot = ot * lax.rsqrt(jnp.mean(ot * ot, axis=0, keepdims=True) + EPS)
    o = ot.T * g_ref[...] * (1.0 - lam_init)
    o_ref[...] = o.astype(BF16)


def _attention(qk, vt, lam_p, norm_g, lam_init, B, S, L):
    tq = ATTN_TILE
    tk = ATTN_TILE
    assert S % tq == 0 and S % (2 * tk) == 0
    nq = S // tq
    nh = DA_HEADS
    kcol = nh
    ctx0 = B * S // L
    kern = functools.partial(_attn_kernel, tk=tk, lam_init=lam_init)
    return pl.pallas_call(
        kern,
        grid=(B, nh, nq),
        in_specs=[pl.BlockSpec((tq, 128), lambda b, h, i: (b * nq + i, h)),
                  pl.BlockSpec((S, 128), lambda b, h, i: (b, kcol + h)),
                  pl.BlockSpec((L, 128), lambda b, h, i: (ctx0 + b, kcol + h)),
                  pl.BlockSpec((128, S), lambda b, h, i: (h, b)),
                  pl.BlockSpec((128, L), lambda b, h, i: (h, ctx0 + b)),
                  pl.BlockSpec((4, DA_DHEAD), lambda b, h, i: (0, 0)),
                  pl.BlockSpec((None, 1, 128), lambda b, h, i: (h, 0, 0))],
        out_specs=pl.BlockSpec((tq, 128), lambda b, h, i: (b * nq + i, h)),
        out_shape=jax.ShapeDtypeStruct((B * S, D_MODEL), BF16),
        scratch_shapes=[pltpu.VMEM((2, 128 + SUM_ROWS, tq), F32),
                        pltpu.VMEM((2, 1, tq), F32),
                        pltpu.VMEM((3, 2, 1, tq), F32),
                        pltpu.VMEM((2, tq, 128), BF16),
                        pltpu.VMEM((2, tk, tq), F32),
                        pltpu.VMEM((2, tk, tq), F32),
                        pltpu.VMEM((2, L, tq), F32)],
        compiler_params=_cparams(("arbitrary", "arbitrary", "arbitrary")),
        name="diff_attention",
    )(qk, qk, qk, vt, vt, lam_p, norm_g.reshape(nh, 1, 128))


def _expert_kernel(be_ref, nu_ref, tok_ref, tokn_ref, v_hbm, w1_ref, w3_ref, w2_ref, y_ref,
                   w1b_ref, w3b_ref, w2b_ref, xbuf_ref, sem_ref):
    i = pl.program_id(0)
    n_used = nu_ref[0]
    tile = ROW_TILE_SUBLANES

    def row_copy(idx_ref, r, slot):
        src = v_hbm.at[pl.ds(pl.multiple_of(idx_ref[0, r], tile), tile), :]
        return pltpu.make_async_copy(src, xbuf_ref.at[slot, pl.ds(r * tile, tile), :], sem_ref.at[slot])

    def start_gather(idx_ref, slot):
        for r in range(MOE_BLOCK):
            row_copy(idx_ref, r, slot).start(priority=r % 2)

    def wait_gather(slot):
        pltpu.make_async_copy(v_hbm.at[pl.ds(0, MOE_BLOCK * tile), :], xbuf_ref.at[slot], sem_ref.at[slot]).wait()

    @pl.when(i == 0)
    def _first_block():
        start_gather(tok_ref, 0)

    @pl.when((i == 0) | (be_ref[i] != be_ref[jnp.maximum(i - 1, 0)]))
    def _new_expert():
        w1b_ref[...] = w1_ref[...].astype(BF16)
        w3b_ref[...] = w3_ref[...].astype(BF16)
        w2b_ref[...] = w2_ref[...].astype(BF16)

    for slot in range(2):
        @pl.when((i % 2 == slot) & (i < n_used))
        def _compute():
            @pl.when(i + 1 < n_used)
            def _next_block():
                start_gather(tokn_ref, 1 - slot)

            wait_gather(slot)
            x = jnp.concatenate([xbuf_ref[slot, pl.ds(s, MOE_BLOCK, stride=tile), :] for s in range(tile)],
                                axis=1).astype(BF16)
            a = jnp.dot(x, w1b_ref[...], preferred_element_type=F32)
            b = jnp.dot(x, w3b_ref[...], preferred_element_type=F32)
            hm = ((a * _sigmoid(a)) * b).astype(BF16)
            y_ref[...] = jnp.dot(hm, w2b_ref[...], preferred_element_type=F32).astype(BF16)

    @pl.when(i >= n_used)
    def _skip():
        y_ref[...] = jnp.zeros_like(y_ref)


def _experts(blk_expert, n_used, buf_row, v_tiles, w1, w3, w2, layer):
    nblk = blk_expert.shape[0]
    tok3 = buf_row.reshape(nblk, 1, MOE_BLOCK)
    wmap = lambda i, be, nu: (layer, be[i], 0, 0)
    grid_spec = pltpu.PrefetchScalarGridSpec(
        num_scalar_prefetch=2,
        grid=(nblk,),
        in_specs=[pl.BlockSpec((None, 1, MOE_BLOCK), lambda i, be, nu: (i, 0, 0), memory_space=pltpu.SMEM),
                  pl.BlockSpec((None, 1, MOE_BLOCK), lambda i, be, nu: (jnp.minimum(i + 1, nblk - 1), 0, 0),
                               memory_space=pltpu.SMEM),
                  pl.BlockSpec(memory_space=pl.ANY),
                  pl.BlockSpec((None, None, D_MODEL, D_EXPERT), wmap),
                  pl.BlockSpec((None, None, D_MODEL, D_EXPERT), wmap),
                  pl.BlockSpec((None, None, D_EXPERT, D_MODEL), wmap)],
        out_specs=pl.BlockSpec((MOE_BLOCK, D_MODEL), lambda i, be, nu: (i, 0)),
        scratch_shapes=[pltpu.VMEM((D_MODEL, D_EXPERT), BF16),
                        pltpu.VMEM((D_MODEL, D_EXPERT), BF16),
                        pltpu.VMEM((D_EXPERT, D_MODEL), BF16),
                        pltpu.VMEM((2, MOE_BLOCK * ROW_TILE_SUBLANES, 128), F32),
                        pltpu.SemaphoreType.DMA((2,))],
    )
    return pl.pallas_call(
        _expert_kernel,
        grid_spec=grid_spec,
        out_shape=jax.ShapeDtypeStruct((nblk * MOE_BLOCK, D_MODEL), BF16),
        compiler_params=_cparams(("arbitrary",)),
        name="experts",
    )(blk_expert, n_used, tok3, tok3, v_tiles, w1, w3, w2)


ROUTE_ROWS = 48


def _route_tile(lg, idx_ref, gate_ref, cnt_ref, base_ref):
    tr = lg.shape[0]

    @pl.when(pl.program_id(0) == 0)
    def _init():
        base_ref[...] = jnp.zeros_like(base_ref)

    lt = lg.T[0:ROUTE_ROWS, :]
    row = lax.broadcasted_iota(jnp.int32, lt.shape, 0)
    first_arg = lambda x, mx: jnp.min(jnp.where(x == mx, row, 128), axis=0, keepdims=True)
    is_g = row < N_GROUPS
    gl = jnp.where(is_g, lt, NEG_BIG)
    gmax = jnp.max(gl, axis=0, keepdims=True)
    pg_top = 1.0 / jnp.sum(jnp.where(is_g, jnp.exp(gl - gmax), 0.0), axis=0, keepdims=True)
    grp = first_arg(gl, gmax)
    e_lo = N_GROUPS + grp * EXPERTS_PER_GROUP
    el = jnp.where((row >= e_lo) & (row < e_lo + EXPERTS_PER_GROUP), lt, NEG_BIG)
    m1 = jnp.max(el, axis=0, keepdims=True)
    i1 = first_arg(el, m1)
    el2 = jnp.where(row == i1, NEG_BIG, el)
    m2 = jnp.max(el2, axis=0, keepdims=True)
    i2 = first_arg(el2, m2)
    p2 = jnp.exp(m2 - m1)
    g1 = pg_top / (1.0 + p2)
    g2 = pg_top * p2 / (1.0 + p2)

    hit1 = row == i1
    hit2 = row == i2
    onehot = jnp.where(hit1 | hit2, 1.0, 0.0).astype(BF16)
    s_i = lax.broadcasted_iota(jnp.int32, (tr, tr), 0)
    t_i = lax.broadcasted_iota(jnp.int32, (tr, tr), 1)
    earlier = jnp.where(s_i < t_i, 1.0, 0.0).astype(BF16)
    before = jnp.dot(onehot, earlier, preferred_element_type=F32) + base_ref[:, 0:1]
    rank1 = jnp.sum(jnp.where(hit1, before, 0.0), axis=0, keepdims=True).astype(jnp.int32)
    rank2 = jnp.sum(jnp.where(hit2, before, 0.0), axis=0, keepdims=True).astype(jnp.int32)
    base = base_ref[...] + jnp.dot(onehot, jnp.ones((tr, ROUTER_PAD), BF16), preferred_element_type=F32)
    base_ref[...] = base
    cnt_ref[...] = base

    r8 = lax.broadcasted_iota(jnp.int32, (8, tr), 0)
    idx_ref[...] = jnp.where(r8 == 0, i1 - N_GROUPS,
                             jnp.where(r8 == 1, i2 - N_GROUPS,
                                       jnp.where(r8 == 2, rank1, jnp.where(r8 == 3, rank2, 0))))
    r128 = lax.broadcasted_iota(jnp.int32, (ROUTER_PAD, tr), 0)
    gate_ref[...] = jnp.where(r128 == 0, g1, jnp.where(r128 == 1, g2, 0.0)).T


def _block_layout(idx, gate, cnt):
    n = idx.shape[1]
    expert = idx[0:TOP_K]
    rank = idx[TOP_K:2 * TOP_K]
    counts = cnt[N_GROUPS:N_GROUPS + N_EXPERTS, 0].astype(jnp.int32)
    a = n * TOP_K
    nblk = -(-a // MOE_BLOCK) + N_EXPERTS
    padded = (counts + MOE_BLOCK - 1) // MOE_BLOCK * MOE_BLOCK
    pend = jnp.cumsum(padded)
    start = pend - padded
    cstart = jnp.cumsum(counts) - counts
    hit = expert[:, None, :] == jnp.arange(N_EXPERTS, dtype=jnp.int32)[None, :, None]
    dest = jnp.sum(jnp.where(hit, start[None, :, None], 0), axis=1) + rank
    tok = jnp.broadcast_to(jnp.arange(n, dtype=jnp.int32)[None, :], (TOP_K, n))
    _, by_row = lax.sort_key_val(dest.reshape(-1), tok.reshape(-1))
    n_used = (pend[-1] // MOE_BLOCK).astype(jnp.int32)
    blk_row = jnp.minimum(jnp.arange(nblk, dtype=jnp.int32), n_used - 1) * MOE_BLOCK
    blk_expert = jnp.clip(jnp.sum((pend[None, :] <= blk_row[:, None]).astype(jnp.int32), axis=1), 0, N_EXPERTS - 1)
    row = jnp.arange(nblk * MOE_BLOCK, dtype=jnp.int32).reshape(nblk, MOE_BLOCK)
    in_expert = row - start[blk_expert][:, None]
    src = jnp.clip(in_expert + cstart[blk_expert][:, None], 0, a - 1)
    live = (in_expert < counts[blk_expert][:, None]) & (jnp.arange(nblk)[:, None] < n_used)
    buf_tok = jnp.where(live, by_row.at[src].get(mode="promise_in_bounds"), 0).reshape(-1)
    return buf_tok, blk_expert.astype(jnp.int32), n_used.reshape(1), dest, gate


def _moe(v, routing, w1, w3, w2, layer):
    buf_tok, blk_expert, n_used, pos, gate = _block_layout(*routing)
    yb = _experts(blk_expert, n_used, buf_tok * ROW_TILE_SUBLANES, v, w1, w3, w2, layer)
    y0 = yb.at[pos[0]].get(mode="promise_in_bounds")
    y1 = yb.at[pos[1]].get(mode="promise_in_bounds")
    return y0, y1, gate


def _final_kernel(h_ref, y0_ref, y1_ref, gk_ref, modp_ref, g_ref, o_ref):
    h = _combine(h_ref[...], y0_ref[...], y1_ref[...], gk_ref[...], modp_ref[5:6, :])
    o_ref[...] = _rms_rows(h) * g_ref[...]


def _final(h, y0, y1, gk, modp, g, dims):
    B, S, L, tr = dims
    n = h.shape[0]
    mod_idx, _ = _tile_maps(B, S, L, tr)
    row = lambda i: (i, 0)
    return pl.pallas_call(
        _final_kernel,
        grid=(n // tr,),
        in_specs=[pl.BlockSpec((tr, D_MODEL), row),
                  pl.BlockSpec((tr, D_MODEL), row),
                  pl.BlockSpec((tr, D_MODEL), row),
                  pl.BlockSpec((tr, ROUTER_PAD), row),
                  pl.BlockSpec((None, 8, D_MODEL), lambda i: (mod_idx(i), 0, 0)),
                  pl.BlockSpec((1, D_MODEL), lambda i: (0, 0))],
        out_specs=pl.BlockSpec((tr, D_MODEL), row),
        out_shape=jax.ShapeDtypeStruct((n, D_MODEL), F32),
        compiler_params=_cparams(("arbitrary",)),
        name="final_norm",
    )(h, y0, y1, gk, modp, g)


def _rotary_tables_even(S, L, B):
    nf = HEAD_DIM // 2
    inv = ROPE_BASE ** (-jnp.arange(nf, dtype=F32) / nf)
    pos = jnp.concatenate([L + jnp.arange(S), jnp.tile(jnp.arange(L), B)]).astype(F32)
    ang = pos[:, None] * inv[None, :]
    cos, sin = jnp.cos(ang), jnp.sin(ang)
    return jnp.concatenate([cos, cos], 1), jnp.concatenate([-sin, sin], 1)


def _rotary_tables_odd(S, L, B):
    nf = DA_DHEAD // 4
    inv = ROPE_BASE ** (-jnp.arange(nf, dtype=F32) / nf)
    t = jnp.arange(S)
    rows, cols = (t // GRID_W).astype(F32), (t % GRID_W).astype(F32)
    j = np.arange(128)
    f_idx = j % nf
    use_col = (j % DA_DHEAD) >= DA_DHEAD // 2
    first = (j % (2 * nf)) < nf
    ang = jnp.where(use_col[None, :], cols[:, None], rows[:, None]) * inv[f_idx][None, :]
    cos, sin = jnp.cos(ang), jnp.sin(ang)
    c_lat = cos
    s1_lat = jnp.where(first[None, :], -sin, 0.0)
    s2_lat = jnp.where(first[None, :], 0.0, sin)
    nctx = B * L
    c = jnp.concatenate([c_lat, jnp.ones((nctx, 128), F32)])
    s1 = jnp.concatenate([s1_lat, jnp.zeros((nctx, 128), F32)])
    s2 = jnp.concatenate([s2_lat, jnp.zeros((nctx, 128), F32)])
    return c, s1, s2


def kernel(x, c, ctx, c_ctx, w_mod, b_mod, norm1_g, norm2_g, w_in_even, ml_gate_b, ml_norm_g, ret_log_decay, ret_norm_g, w_out_even, w_in_odd, da_lambda, da_norm_g, w_out_odd, router_g_w, router_g_b, router_e_w, router_e_b, w1, w3, w2, final_norm_g):
    B, S, D = x.shape
    L = ctx.shape[1]
    depth = w_mod.shape[0]
    assert D == D_MODEL and depth == 2 and S % CHUNK == 0 and L % CHUNK == 0 and S % L == 0
    n_lat, n_ctx = B * S, B * L
    tr = _row_tile(S, n_ctx)
    dims = (B, S, L, tr)

    x2d, ctx2d = x.reshape(n_lat, D), ctx.reshape(n_ctx, D)

    cond8 = jnp.zeros((8, D), F32).at[:B].set(c).at[B].set(c_ctx)
    mod = jnp.pad(_modulation(cond8, w_mod, b_mod).reshape(depth, 8, 6, D), ((0, 0), (0, 0), (0, 2), (0, 0)))

    def router_w(l):
        wr = jnp.zeros((D, ROUTER_PAD), F32)
        wr = wr.at[:, :N_GROUPS].set(router_g_w[l]).at[:, N_GROUPS:N_GROUPS + N_EXPERTS].set(router_e_w[l])
        br = jnp.zeros((1, ROUTER_PAD), F32)
        br = br.at[0, :N_GROUPS].set(router_g_b[l]).at[0, N_GROUPS:N_GROUPS + N_EXPERTS].set(router_e_b[l])
        return wr, br

    wi = w_in_even[0]
    mq, mk, mv, mo, mi, mf, rq, rk, rv, rg = jnp.split(wi, np.cumsum(
        [512, 512, 512, 512, 8, 8, 512, 512, 512])[:9].tolist(), axis=1)
    w_main = jnp.concatenate([mq, mk, mv, rq, rk, rv, mo, rg], 1).astype(BF16)
    w_gates = jnp.pad(jnp.concatenate([mi, mf], 1), ((0, 0), (0, GATE_PAD - 16))).astype(BF16)
    gate_bias = jnp.pad(jnp.concatenate([ml_gate_b[0][:, 0].reshape(-1), ml_gate_b[0][:, 1].reshape(-1)]),
                        (0, GATE_PAD - 16)).reshape(1, GATE_PAD)
    cs, sn = _rotary_tables_even(S, L, B)
    proj, gates = _proj_even(x2d, ctx2d, mod[0], norm1_g[0].reshape(1, D), w_main, w_gates, cs, sn, dims)
    hf, hb = _scan(ret_log_decay[0].reshape(-1), proj, gates, gate_bias, B, S, L)
    wr, br = router_w(0)
    h, v, *routing = _post_even(hf, hb, proj, ml_norm_g[0].reshape(1, -1), ret_norm_g[0].reshape(1, -1),
                              x2d, ctx2d, mod[0], norm2_g[0].reshape(1, D), w_out_even[0].astype(BF16), wr, br, dims)
    y0, y1, gk = _moe(v, routing, w1, w3, w2, 0)

    lam_init = 0.8 - 0.6 * math.exp(-0.3 * 1)
    nqk = 4 * DA_HEADS * DA_DHEAD
    w_qk = w_in_odd[0][:, :nqk].astype(BF16)
    w_vt = w_in_odd[0][:, nqk:].T.astype(BF16)
    c2, s1, s2 = _rotary_tables_odd(S, L, B)
    h, qk, vt = _proj_odd(h, y0, y1, gk, mod[0], mod[1], norm1_g[1].reshape(1, D), w_qk, w_vt, c2, s1, s2, dims)
    att = _attention(qk, vt, da_lambda[0], da_norm_g[0], lam_init, B, S, L)
    wr, br = router_w(1)
    h_lat, v, *routing = _post_odd(att, h, mod[1], norm2_g[1].reshape(1, D), w_out_odd[0].astype(BF16), wr, br, dims)
    y0, y1, gk = _moe(v, routing, w1, w3, w2, 1)
    out = _final(h_lat, y0, y1, gk, mod[1], final_norm_g.reshape(1, D), dims)
    return out.reshape(B, S, D)
```

```python
import functools
import math

import numpy as np
import jax
import jax.numpy as jnp
from jax import lax
from jax.experimental import pallas as pl
from jax.experimental.pallas import tpu as pltpu

F32 = jnp.float32
BF16 = jnp.bfloat16

D_MODEL = 1024
GRID_W = 64
CHUNK = 128
EPS = 1e-6
ROPE_BASE = 10000.0
ML_HEADS = 4
RET_HEADS = 4
HEAD_DIM = 128
DA_HEADS = 8
DA_DHEAD = 64
N_GROUPS = 4
EXPERTS_PER_GROUP = 8
N_EXPERTS = N_GROUPS * EXPERTS_PER_GROUP
TOP_K = 2
D_EXPERT = D_MODEL // 2
MOE_BLOCK = 256
ROW_TILE_SUBLANES = D_MODEL // 128
ROUTER_PAD = 128
GATE_PAD = 128
EVEN_MAIN = 4096
EVEN_SCAN_COLS = 3072
NEG_BIG = -1e30

V7X_VMEM_LIMIT = 56 * 1024 * 1024


def _cparams(sem):
    return pltpu.CompilerParams(dimension_semantics=sem, vmem_limit_bytes=V7X_VMEM_LIMIT)


def _row_tile(n_lat_per_batch, n_ctx_total):
    for t in (512, 256, 128):
        if n_lat_per_batch % t == 0 and n_ctx_total % t == 0:
            return t
    raise ValueError("row counts must be multiples of 128")


def _split3(a):
    hi = a.astype(BF16)
    r1 = a - hi.astype(F32)
    mid = r1.astype(BF16)
    lo = (r1 - mid.astype(F32)).astype(BF16)
    return hi, mid, lo


def _dot_hi(a, b):
    ah, al, _ = _split3(a)
    bh, bl, _ = _split3(b)
    d = functools.partial(jnp.dot, preferred_element_type=F32)
    return d(ah, bh) + (d(ah, bl) + d(al, bh))


def _sigmoid(x):
    return 1.0 / (1.0 + jnp.exp(-x))


def _rms_rows(x):
    return x * lax.rsqrt(jnp.mean(x * x, axis=-1, keepdims=True) + EPS)


def _mod_kernel(c_ref, w_ref, b_ref, o_ref):
    c = c_ref[...]
    o_ref[...] = _dot_hi(c * _sigmoid(c), w_ref[...]) + b_ref[...]


def _modulation(cond8, w_mod, b_mod):
    depth = w_mod.shape[0]
    ncol = w_mod.shape[2]
    tn = 1024
    return pl.pallas_call(
        _mod_kernel,
        grid=(depth, ncol // tn),
        in_specs=[pl.BlockSpec((8, D_MODEL), lambda l, j: (0, 0)),
                  pl.BlockSpec((None, D_MODEL, tn), lambda l, j: (l, 0, j)),
                  pl.BlockSpec((None, 1, tn), lambda l, j: (l, 0, j))],
        out_specs=pl.BlockSpec((None, 8, tn), lambda l, j: (l, 0, j)),
        out_shape=jax.ShapeDtypeStruct((depth, 8, ncol), F32),
        compiler_params=_cparams(("arbitrary", "arbitrary")),
        name="modulation",
    )(cond8, w_mod, b_mod.reshape(depth, 1, ncol))


def _adaln_bf16(x, g, shift, scale):
    return ((_rms_rows(x) * g) * (1.0 + scale) + shift).astype(BF16)


def _lat_or_ctx(x_ref, ctx_ref, lat_tiles):
    return jnp.where(pl.program_id(0) < lat_tiles, x_ref[...], ctx_ref[...])


def _proj_even_kernel(x_ref, ctx_ref, mod_ref, g_ref, w_ref, wg_ref, cs_ref, sn_ref, proj_ref, gates_ref,
                      *, lat_tiles):
    u = _adaln_bf16(_lat_or_ctx(x_ref, ctx_ref, lat_tiles), g_ref[...], mod_ref[0:1, :], mod_ref[1:2, :])
    cs = cs_ref[...]
    sn = sn_ref[...]
    kscale = HEAD_DIM ** -0.5
    for gi in range(EVEN_MAIN // 512):
        acc = jnp.dot(u, w_ref[:, gi * 512:(gi + 1) * 512], preferred_element_type=F32)
        if gi in (3, 4):
            parts = []
            for hh in range(RET_HEADS):
                blk = acc[:, hh * HEAD_DIM:(hh + 1) * HEAD_DIM]
                parts.append(blk * cs + pltpu.roll(blk, HEAD_DIM // 2, 1) * sn)
            acc = jnp.concatenate(parts, axis=1)
        if gi in (1, 4):
            acc = acc * kscale
        proj_ref[:, gi * 512:(gi + 1) * 512] = acc.astype(BF16)
    gates_ref[...] = jnp.dot(u, wg_ref[...], preferred_element_type=F32)


def _combine(h, y0, y1, gk, gate):
    moe = gk[:, 0:1] * y0.astype(F32) + gk[:, 1:2] * y1.astype(F32)
    return h + gate * moe


def _proj_odd_kernel(h_ref, y0_ref, y1_ref, gk_ref, modp_ref, mod_ref, g_ref, w_ref, wvt_ref,
                     c_ref, s1_ref, s2_ref, hout_ref, qk_ref, vt_ref):
    h = _combine(h_ref[...], y0_ref[...], y1_ref[...], gk_ref[...], modp_ref[5:6, :])
    hout_ref[...] = h
    u = _adaln_bf16(h, g_ref[...], mod_ref[0:1, :], mod_ref[1:2, :])
    c = c_ref[...]
    s1 = s1_ref[...]
    s2 = s2_ref[...]
    qscale = DA_DHEAD ** -0.5 * math.log2(math.e)
    nq = DA_HEADS * 2 * DA_DHEAD
    for gi in range(2 * nq // 512):
        acc = jnp.dot(u, w_ref[:, gi * 512:(gi + 1) * 512], preferred_element_type=F32)
        parts = []
        for hh in range(512 // 128):
            blk = acc[:, hh * 128:(hh + 1) * 128]
            parts.append(blk * c + pltpu.roll(blk, 128 - 16, 1) * s1 + pltpu.roll(blk, 16, 1) * s2)
        acc = jnp.concatenate(parts, axis=1)
        if gi * 512 < nq:
            acc = acc * qscale
        qk_ref[:, gi * 512:(gi + 1) * 512] = acc.astype(BF16)
    vt = lax.dot_general(wvt_ref[...], u, (((1,), (1,)), ((), ())), preferred_element_type=F32)
    vt_ref[...] = vt.astype(BF16)


def _tile_maps(B, S, L, tr):
    lat_tiles = B * S // tr
    per_b = S // tr

    def mod_idx(i):
        return jnp.where(i < lat_tiles, i // per_b, B)

    def rot_idx(i):
        return jnp.where(i < lat_tiles, i % per_b, per_b + (i - lat_tiles))

    return mod_idx, rot_idx


def _lat_ctx_specs(tr, lat_tiles):
    return [pl.BlockSpec((tr, D_MODEL), lambda i: (jnp.minimum(i, lat_tiles - 1), 0)),
            pl.BlockSpec((tr, D_MODEL), lambda i: (jnp.maximum(i - lat_tiles, 0), 0))]


def _proj_even(x2d, ctx2d, mod, g, w_main, w_gates, cs, sn, dims):
    B, S, L, tr = dims
    n = x2d.shape[0] + ctx2d.shape[0]
    lat_tiles = x2d.shape[0] // tr
    mod_idx, rot_idx = _tile_maps(B, S, L, tr)
    const = lambda i: (0, 0)
    return pl.pallas_call(
        functools.partial(_proj_even_kernel, lat_tiles=lat_tiles),
        grid=(n // tr,),
        in_specs=_lat_ctx_specs(tr, lat_tiles) + [
                  pl.BlockSpec((None, 8, D_MODEL), lambda i: (mod_idx(i), 0, 0)),
                  pl.BlockSpec((1, D_MODEL), const),
                  pl.BlockSpec((D_MODEL, EVEN_MAIN), const),
                  pl.BlockSpec((D_MODEL, GATE_PAD), const),
                  pl.BlockSpec((tr, 128), lambda i: (rot_idx(i), 0)),
                  pl.BlockSpec((tr, 128), lambda i: (rot_idx(i), 0))],
        out_specs=[pl.BlockSpec((tr, EVEN_MAIN), lambda i: (i, 0)),
                   pl.BlockSpec((tr, GATE_PAD), lambda i: (i, 0))],
        out_shape=[jax.ShapeDtypeStruct((n, EVEN_MAIN), BF16),
                   jax.ShapeDtypeStruct((n, GATE_PAD), F32)],
        compiler_params=_cparams(("arbitrary",)),
        name="proj_even",
    )(x2d, ctx2d, mod, g, w_main, w_gates, cs, sn)


def _proj_odd(h, y0, y1, gk, modp, mod, g, w_qk, w_vt, c, s1, s2, dims):
    B, S, L, tr = dims
    n = h.shape[0]
    mod_idx, rot_idx = _tile_maps(B, S, L, tr)
    const = lambda i: (0, 0)
    row = lambda i: (i, 0)
    nqk = w_qk.shape[1]
    return pl.pallas_call(
        _proj_odd_kernel,
        grid=(n // tr,),
        in_specs=[pl.BlockSpec((tr, D_MODEL), row),
                  pl.BlockSpec((tr, D_MODEL), row),
                  pl.BlockSpec((tr, D_MODEL), row),
                  pl.BlockSpec((tr, ROUTER_PAD), row),
                  pl.BlockSpec((None, 8, D_MODEL), lambda i: (mod_idx(i), 0, 0)),
                  pl.BlockSpec((None, 8, D_MODEL), lambda i: (mod_idx(i), 0, 0)),
                  pl.BlockSpec((1, D_MODEL), const),
                  pl.BlockSpec((D_MODEL, nqk), const),
                  pl.BlockSpec((D_MODEL, D_MODEL), const),
                  pl.BlockSpec((tr, 128), lambda i: (rot_idx(i), 0)),
                  pl.BlockSpec((tr, 128), lambda i: (rot_idx(i), 0)),
                  pl.BlockSpec((tr, 128), lambda i: (rot_idx(i), 0))],
        out_specs=[pl.BlockSpec((tr, D_MODEL), row),
                   pl.BlockSpec((tr, nqk), row),
                   pl.BlockSpec((D_MODEL, tr), lambda i: (0, i))],
        out_shape=[jax.ShapeDtypeStruct((n, D_MODEL), F32),
                   jax.ShapeDtypeStruct((n, nqk), BF16),
                   jax.ShapeDtypeStruct((D_MODEL, n), BF16)],
        compiler_params=_cparams(("arbitrary",)),
        name="proj_odd",
    )(h, y0, y1, gk, modp, mod, g, w_qk, w_vt, c, s1, s2)


_NT = (((1,), (1,)), ((), ()))
_TN = (((0,), (0,)), ((), ()))


def _scan_kernel(ld_ref, xf_ref, xb_ref, gf_ref, gb_ref, gfn_ref, gbn_ref, bias_ref, hf_ref, hb_ref,
                 ct_ref, n_ref, m_ref, r_ref, intra_ref, qd_ref, kd_ref, prep_ref):
    t = pl.program_id(1)
    nh = ML_HEADS
    row_i = lax.broadcasted_iota(jnp.int32, (CHUNK, CHUNK), 0)
    col_i = lax.broadcasted_iota(jnp.int32, (CHUNK, CHUNK), 1)
    row_f = row_i.astype(F32)
    col_f = col_i.astype(F32)
    bias = bias_ref[...]
    dot = functools.partial(jnp.dot, preferred_element_type=F32)
    dg = functools.partial(lax.dot_general, preferred_element_type=F32)

    def seen_mask(d):
        return (col_i <= row_i) if d == 0 else (col_i >= row_i)

    def gate_prep(g_ref, d, slot):
        G = g_ref[...] + bias
        LF = jnp.minimum(G, 0.0) - jnp.log1p(jnp.exp(-jnp.abs(G)))
        tri = jnp.where(seen_mask(d), 1.0, 0.0).astype(BF16)
        l_hi, l_mid, l_lo = _split3(LF)
        Bc = dot(tri, l_hi) + (dot(tri, l_mid) + dot(tri, l_lo))
        prep_ref[slot, d, 0] = G
        prep_ref[slot, d, 1] = Bc
        prep_ref[slot, d, 2] = G.T
        prep_ref[slot, d, 3] = Bc.T

    @pl.when(t == 0)
    def _init():
        ct_ref[...] = jnp.zeros_like(ct_ref)
        n_ref[...] = jnp.zeros_like(n_ref)
        m_ref[...] = jnp.zeros_like(m_ref)
        r_ref[...] = jnp.zeros_like(r_ref)
        gate_prep(gf_ref, 0, 0)
        gate_prep(gb_ref, 1, 0)
        for d in range(2):
            for hh in range(nh):
                hd = d * nh + hh
                lg = ld_ref[hd]
                if d == 0:
                    intra = jnp.where(col_i <= row_i, jnp.exp(lg * (row_f - col_f)), 0.0)
                    qd = jnp.exp(lg * (row_f + 1.0))
                    kd = jnp.exp(lg * (CHUNK - 1.0 - row_f))
                else:
                    intra = jnp.where(col_i >= row_i, jnp.exp(lg * (col_f - row_f)), 0.0)
                    qd = jnp.exp(lg * (CHUNK - row_f))
                    kd = jnp.exp(lg * row_f)
                intra_ref[hd] = intra
                qd_ref[hd] = qd
                kd_ref[hd] = kd

    cur = t % 2
    gate_prep(gfn_ref, 0, 1 - cur)
    gate_prep(gbn_ref, 1, 1 - cur)

    for d, (x_ref, o_ref) in enumerate(((xf_ref, hf_ref), (xb_ref, hb_ref))):
        seen = seen_mask(d)
        G = prep_ref[cur, d, 0]
        Bc = prep_ref[cur, d, 1]
        GT = prep_ref[cur, d, 2]
        BT = prep_ref[cur, d, 3]
        last = CHUNK - 1 if d == 0 else 0
        for hh in range(nh):
            hd = d * nh + hh
            c0 = hh * HEAD_DIM
            q = x_ref[:, c0:c0 + HEAD_DIM]
            k = x_ref[:, 512 + c0:512 + c0 + HEAD_DIM]
            v = x_ref[:, 1024 + c0:1024 + c0 + HEAD_DIM]
            ic_col = G[:, hd:hd + 1]
            ic_row = GT[hd:hd + 1, :]
            b_col = Bc[:, 8 + hd:9 + hd]
            b_row = BT[8 + hd:9 + hd, :]
            m0 = m_ref[hd][:, 0:1]
            dlog = jnp.where(seen, b_col - b_row + ic_row, NEG_BIG)
            inter = b_col + m0
            m_t = jnp.maximum(inter, jnp.max(dlog, axis=1, keepdims=True))
            s = dg(q, k, _NT) * jnp.exp(dlog - m_t)
            w_inter = jnp.exp(inter - m_t)
            ct = ct_ref[hd]
            n_row = n_ref[hd]
            numer = dot(s.astype(BF16), v) + w_inter * dot(q, ct.astype(BF16))
            denom = (jnp.sum(s, axis=1, keepdims=True)
                     + w_inter * jnp.sum(q.astype(F32) * n_row, axis=1, keepdims=True))
            hval = numer / jnp.maximum(jnp.abs(denom), jnp.exp(-m_t))
            o_ref[:, c0:c0 + HEAD_DIM] = hval.astype(BF16)
            b_last = b_col[last:last + 1, :]
            w_log_row = b_last - b_row + ic_row
            m_new = jnp.maximum(b_last + m0, jnp.max(w_log_row, axis=1, keepdims=True))
            decay = jnp.exp(b_last + m0 - m_new)
            w_col = jnp.exp(b_last - b_col + ic_col - m_new)
            w_row = jnp.exp(w_log_row - m_new)
            vw = (v.astype(F32) * w_col).astype(BF16)
            ct_ref[hd] = decay * ct + dg(k, vw, _TN)
            w8 = jnp.broadcast_to(w_row, (8, CHUNK)).astype(BF16)
            n_ref[hd] = decay * n_row + dot(w8, k)[0:1, :]
            m_ref[hd] = jnp.broadcast_to(m_new, (1, HEAD_DIM))
            rq = x_ref[:, 1536 + c0:1536 + c0 + HEAD_DIM]
            rk = x_ref[:, 2048 + c0:2048 + c0 + HEAD_DIM]
            rv = x_ref[:, 2560 + c0:2560 + c0 + HEAD_DIM]
            rs = dg(rq, rk, _NT) * intra_ref[hd]
            R = r_ref[hd]
            o = dot(rs.astype(BF16), rv) + qd_ref[hd] * dot(rq, R.astype(BF16))
            o_ref[:, 512 + c0:512 + c0 + HEAD_DIM] = o.astype(BF16)
            kdk = (rk.astype(F32) * kd_ref[hd]).astype(BF16)
            cdec = jnp.exp(ld_ref[hd] * jnp.full((1, HEAD_DIM), float(CHUNK), F32))
            r_ref[hd] = cdec * R + dg(kdk, rv, _TN)


def _scan(ret_ld8, proj, gates, bias, B, S, L):
    n = proj.shape[0]
    nlb, ncb = S // CHUNK, L // CHUNK
    nc = nlb + ncb

    def fwd(b, t):
        return jnp.where(t < ncb, B * nlb + b * ncb + t, b * nlb + t - ncb)

    def bwd(b, t):
        return jnp.where(t < ncb, B * nlb + b * ncb + (ncb - 1 - t), b * nlb + (nlb - 1 - (t - ncb)))

    state = pltpu.VMEM((2 * ML_HEADS, HEAD_DIM, HEAD_DIM), F32)
    vec = pltpu.VMEM((2 * ML_HEADS, 1, HEAD_DIM), F32)
    grid_spec = pltpu.PrefetchScalarGridSpec(
        num_scalar_prefetch=1,
        grid=(B, nc),
        in_specs=[pl.BlockSpec((CHUNK, EVEN_SCAN_COLS), lambda b, t, ld: (fwd(b, t), 0)),
                  pl.BlockSpec((CHUNK, EVEN_SCAN_COLS), lambda b, t, ld: (bwd(b, t), 0)),
                  pl.BlockSpec((CHUNK, GATE_PAD), lambda b, t, ld: (fwd(b, t), 0)),
                  pl.BlockSpec((CHUNK, GATE_PAD), lambda b, t, ld: (bwd(b, t), 0)),
                  pl.BlockSpec((CHUNK, GATE_PAD), lambda b, t, ld: (fwd(b, jnp.minimum(t + 1, nc - 1)), 0)),
                  pl.BlockSpec((CHUNK, GATE_PAD), lambda b, t, ld: (bwd(b, jnp.minimum(t + 1, nc - 1)), 0)),
                  pl.BlockSpec((1, GATE_PAD), lambda b, t, ld: (0, 0))],
        out_specs=[pl.BlockSpec((CHUNK, D_MODEL), lambda b, t, ld: (fwd(b, t), 0)),
                   pl.BlockSpec((CHUNK, D_MODEL), lambda b, t, ld: (bwd(b, t), 0))],
        scratch_shapes=[state, vec, vec, state, state, state, state,
                        pltpu.VMEM((2, 2, 4, CHUNK, CHUNK), F32)],
    )
    return pl.pallas_call(
        _scan_kernel,
        grid_spec=grid_spec,
        out_shape=[jax.ShapeDtypeStruct((n, D_MODEL), BF16),
                   jax.ShapeDtypeStruct((n, D_MODEL), BF16)],
        compiler_params=_cparams(("arbitrary", "arbitrary")),
        name="scan",
    )(ret_ld8, proj, proj, gates, gates, gates, gates, bias)


def _post_tail(merged, h_in, mod_ref, g2_ref, wout_ref, wr_ref, br_ref, hout_ref, v_ref, idx_ref, gate_ref,
               cnt_ref, base_ref):
    y = jnp.dot(merged, wout_ref[...], preferred_element_type=F32)
    h = h_in + mod_ref[2:3, :] * y
    hout_ref[...] = h
    v = (_rms_rows(h) * g2_ref[...]) * (1.0 + mod_ref[4:5, :]) + mod_ref[3:4, :]
    for s in range(ROW_TILE_SUBLANES):
        v_ref[pl.ds(s, v.shape[0], stride=ROW_TILE_SUBLANES), :] = v[:, s * 128:(s + 1) * 128]
    _route_tile(_dot_hi(v, wr_ref[...]) + br_ref[...], idx_ref, gate_ref, cnt_ref, base_ref)


def _head_ln(x, g):
    mu = jnp.mean(x, axis=-1, keepdims=True)
    xc = x - mu
    var = jnp.mean(xc * xc, axis=-1, keepdims=True)
    return xc * lax.rsqrt(var + EPS) * g


def _post_even_kernel(hf_ref, hb_ref, og_ref, mlg_ref, retg_ref, x_ref, ctx_ref, mod_ref, g2_ref, wout_ref,
                      wr_ref, br_ref, hout_ref, v_ref, idx_ref, gate_ref, cnt_ref, base_ref, *, lat_tiles):
    parts = []
    for hh in range(ML_HEADS):
        sl = slice(hh * HEAD_DIM, (hh + 1) * HEAD_DIM)
        ml = hf_ref[:, sl].astype(F32) + hb_ref[:, sl].astype(F32)
        parts.append(_head_ln(_sigmoid(og_ref[:, sl].astype(F32)) * ml, mlg_ref[:, sl]))
    for hh in range(RET_HEADS):
        sl = slice(512 + hh * HEAD_DIM, 512 + (hh + 1) * HEAD_DIM)
        ret = hf_ref[:, sl].astype(F32) + hb_ref[:, sl].astype(F32)
        rg = og_ref[:, sl].astype(F32)
        parts.append((rg * _sigmoid(rg)) * _head_ln(ret, retg_ref[:, hh * HEAD_DIM:(hh + 1) * HEAD_DIM]))
    merged = jnp.concatenate(parts, axis=1).astype(BF16)
    h_in = _lat_or_ctx(x_ref, ctx_ref, lat_tiles)
    _post_tail(merged, h_in, mod_ref, g2_ref, wout_ref, wr_ref, br_ref, hout_ref, v_ref, idx_ref, gate_ref,
               cnt_ref, base_ref)


def _post_odd_kernel(att_ref, h_ref, mod_ref, g2_ref, wout_ref, wr_ref, br_ref, hout_ref, v_ref, idx_ref, gate_ref,
                     cnt_ref, base_ref):
    _post_tail(att_ref[...], h_ref[...], mod_ref, g2_ref, wout_ref, wr_ref, br_ref, hout_ref, v_ref, idx_ref,
               gate_ref, cnt_ref, base_ref)


def _post_specs(n, tr, mod_idx):
    const = lambda i: (0, 0)
    row = lambda i: (i, 0)
    tail_in = [pl.BlockSpec((tr, D_MODEL), row),
               pl.BlockSpec((None, 8, D_MODEL), lambda i: (mod_idx(i), 0, 0)),
               pl.BlockSpec((1, D_MODEL), const),
               pl.BlockSpec((D_MODEL, D_MODEL), const),
               pl.BlockSpec((D_MODEL, ROUTER_PAD), const),
               pl.BlockSpec((1, ROUTER_PAD), const)]
    out_specs = [pl.BlockSpec((tr, D_MODEL), row),
                 pl.BlockSpec((tr * ROW_TILE_SUBLANES, 128), row),
                 pl.BlockSpec((8, tr), lambda i: (0, i)),
                 pl.BlockSpec((tr, ROUTER_PAD), row),
                 pl.BlockSpec((ROUTE_ROWS, ROUTER_PAD), const)]
    out_shape = [jax.ShapeDtypeStruct((n, D_MODEL), F32),
                 jax.ShapeDtypeStruct((n * ROW_TILE_SUBLANES, 128), F32),
                 jax.ShapeDtypeStruct((8, n), jnp.int32),
                 jax.ShapeDtypeStruct((n, ROUTER_PAD), F32),
                 jax.ShapeDtypeStruct((ROUTE_ROWS, ROUTER_PAD), F32)]
    return tail_in, out_specs, out_shape


def _post_even(hf, hb, proj, mlg, retg, x2d, ctx2d, mod, g2, wout, wr, br, dims):
    B, S, L, tr = dims
    n = hf.shape[0]
    mod_idx, _ = _tile_maps(B, S, L, tr)
    tail_in, out_specs, out_shape = _post_specs(n, tr, mod_idx)
    lat_tiles = x2d.shape[0] // tr
    row = lambda i: (i, 0)
    const = lambda i: (0, 0)
    return pl.pallas_call(
        functools.partial(_post_even_kernel, lat_tiles=lat_tiles),
        grid=(n // tr,),
        in_specs=[pl.BlockSpec((tr, D_MODEL), row),
                  pl.BlockSpec((tr, D_MODEL), row),
                  pl.BlockSpec((tr, D_MODEL), lambda i: (i, EVEN_SCAN_COLS // D_MODEL)),
                  pl.BlockSpec((1, 512), const),
                  pl.BlockSpec((1, 512), const)] + _lat_ctx_specs(tr, lat_tiles) + tail_in[1:],
        out_specs=out_specs,
        out_shape=out_shape,
        scratch_shapes=[pltpu.VMEM((ROUTE_ROWS, ROUTER_PAD), F32)],
        compiler_params=_cparams(("arbitrary",)),
        name="post_even",
    )(hf, hb, proj, mlg, retg, x2d, ctx2d, mod, g2, wout, wr, br)


def _post_odd(att, h, mod, g2, wout, wr, br, dims):
    B, S, L, tr = dims
    n = att.shape[0]
    mod_idx, _ = _tile_maps(B, S, L, tr)
    tail_in, out_specs, out_shape = _post_specs(n, tr, mod_idx)
    return pl.pallas_call(
        _post_odd_kernel,
        grid=(n // tr,),
        in_specs=[pl.BlockSpec((tr, D_MODEL), lambda i: (i, 0))] + tail_in,
        out_specs=out_specs,
        out_shape=out_shape,
        scratch_shapes=[pltpu.VMEM((ROUTE_ROWS, ROUTER_PAD), F32)],
        compiler_params=_cparams(("arbitrary",)),
        name="post_odd",
    )(att, h, mod, g2, wout, wr, br)


SUM_ROWS = 16
ATTN_TILE = 1024


def _attn_kernel(q_ref, kl_ref, kc_ref, vl_ref, vc_ref, lam_ref, g_ref, o_ref,
                 acc_ref, m_ref, cmax_ref, qm_ref, sa_ref, sb_ref, sc_ref, *, tk, lam_init):
    n_lat = kl_ref.shape[0] // tk
    q = q_ref[...]
    lane = lax.broadcasted_iota(jnp.int32, q.shape, 1)
    zero = jnp.zeros_like(q)
    qm_ref[0] = jnp.where(lane < DA_DHEAD, q, zero)
    qm_ref[1] = jnp.where(lane >= DA_DHEAD, q, zero)
    acc_ref[...] = jnp.zeros_like(acc_ref)
    m_ref[...] = jnp.full(m_ref.shape, NEG_BIG, F32)

    def lat_k(c):
        return kl_ref[pl.ds(pl.multiple_of(c * tk, tk), tk), :]

    def lat_v(c):
        return vl_ref[:, pl.ds(pl.multiple_of(c * tk, tk), tk)]

    def scores(dst_ref, slot, kc):
        for mi in range(2):
            st = lax.dot_general(kc, qm_ref[mi], _NT, preferred_element_type=F32)
            dst_ref[mi] = st
            cmax_ref[slot, mi] = jnp.max(st, axis=0, keepdims=True)

    def absorb(src_ref, slot, vtc):
        vext = jnp.concatenate([vtc, jnp.ones((SUM_ROWS, vtc.shape[1]), BF16)], axis=0)
        for mi in range(2):
            m_old = m_ref[mi]
            m_new = jnp.maximum(m_old, cmax_ref[slot, mi])
            p = jnp.exp2(src_ref[mi] - m_new).astype(BF16)
            acc_ref[mi] = jnp.exp2(m_old - m_new) * acc_ref[mi] + jnp.dot(vext, p, preferred_element_type=F32)
            m_ref[mi] = m_new

    scores(sa_ref, 0, lat_k(0))

    def body(j, carry):
        c = 2 * j
        scores(sb_ref, 1, lat_k(c + 1))
        absorb(sa_ref, 0, lat_v(c))
        scores(sa_ref, 0, lat_k(c + 2))
        absorb(sb_ref, 1, lat_v(c + 1))
        return carry

    lax.fori_loop(0, n_lat // 2 - 1, body, 0)
    scores(sb_ref, 1, lat_k(n_lat - 1))
    absorb(sa_ref, 0, lat_v(n_lat - 2))
    scores(sc_ref, 2, kc_ref[...])
    absorb(sb_ref, 1, lat_v(n_lat - 1))
    absorb(sc_ref, 2, vc_ref[...])

    lp = lam_ref[...]
    lam = (jnp.exp(jnp.sum(lp[0:1, :] * lp[1:2, :], axis=1, keepdims=True))
           - jnp.exp(jnp.sum(lp[2:3, :] * lp[3:4, :], axis=1, keepdims=True)) + lam_init)
    o0 = acc_ref[0, 0:128, :] / acc_ref[0, 128:129, :]
    o1 = acc_ref[1, 0:128, :] / acc_ref[1, 128:129, :]
    ot = o0 - lam * o1
    ot = ot * lax.rsqrt(jnp.mean(ot * ot, axis=0, keepdims=True) + EPS)
    o = ot.T * g_ref[...] * (1.0 - lam_init)
    o_ref[...] = o.astype(BF16)


def _attention(qk, vt, lam_p, norm_g, lam_init, B, S, L):
    tq = ATTN_TILE
    tk = ATTN_TILE
    assert S % tq == 0 and S % (2 * tk) == 0
    nq = S // tq
    nh = DA_HEADS
    kcol = nh
    ctx0 = B * S // L
    kern = functools.partial(_attn_kernel, tk=tk, lam_init=lam_init)
    return pl.pallas_call(
        kern,
        grid=(B, nh, nq),
        in_specs=[pl.BlockSpec((tq, 128), lambda b, h, i: (b * nq + i, h)),
                  pl.BlockSpec((S, 128), lambda b, h, i: (b, kcol + h)),
                  pl.BlockSpec((L, 128), lambda b, h, i: (ctx0 + b, kcol + h)),
                  pl.BlockSpec((128, S), lambda b, h, i: (h, b)),
                  pl.BlockSpec((128, L), lambda b, h, i: (h, ctx0 + b)),
                  pl.BlockSpec((4, DA_DHEAD), lambda b, h, i: (0, 0)),
                  pl.BlockSpec((None, 1, 128), lambda b, h, i: (h, 0, 0))],
        out_specs=pl.BlockSpec((tq, 128), lambda b, h, i: (b * nq + i, h)),
        out_shape=jax.ShapeDtypeStruct((B * S, D_MODEL), BF16),
        scratch_shapes=[pltpu.VMEM((2, 128 + SUM_ROWS, tq), F32),
                        pltpu.VMEM((2, 1, tq), F32),
                        pltpu.VMEM((3, 2, 1, tq), F32),
                        pltpu.VMEM((2, tq, 128), BF16),
                        pltpu.VMEM((2, tk, tq), F32),
                        pltpu.VMEM((2, tk, tq), F32),
                        pltpu.VMEM((2, L, tq), F32)],
        compiler_params=_cparams(("arbitrary", "arbitrary", "arbitrary")),
        name="diff_attention",
    )(qk, qk, qk, vt, vt, lam_p, norm_g.reshape(nh, 1, 128))


GATHER_SLOTS = 3


def _expert_kernel(be_ref, nu_ref, tok_ref, tok1_ref, tok2_ref, v_hbm, w1_ref, w3_ref, w2_ref, y_ref,
                   w1b_ref, w3b_ref, w2b_ref, xbuf_ref, sem_ref):
    i = pl.program_id(0)
    n_used = nu_ref[0]
    tile = ROW_TILE_SUBLANES

    def row_copy(idx_ref, r, slot):
        src = v_hbm.at[pl.ds(pl.multiple_of(idx_ref[0, r], tile), tile), :]
        return pltpu.make_async_copy(src, xbuf_ref.at[slot, pl.ds(r * tile, tile), :], sem_ref.at[slot])

    def start_gather(idx_ref, slot):
        for r in range(MOE_BLOCK):
            row_copy(idx_ref, r, slot).start(priority=r % 2)

    def wait_gather(slot):
        pltpu.make_async_copy(v_hbm.at[pl.ds(0, MOE_BLOCK * tile), :], xbuf_ref.at[slot], sem_ref.at[slot]).wait()

    @pl.when(i == 0)
    def _first_blocks():
        start_gather(tok_ref, 0)

        @pl.when(1 < n_used)
        def _second_block():
            start_gather(tok1_ref, 1)

    @pl.when((i == 0) | (be_ref[i] != be_ref[jnp.maximum(i - 1, 0)]))
    def _new_expert():
        w1b_ref[...] = w1_ref[...].astype(BF16)
        w3b_ref[...] = w3_ref[...].astype(BF16)
        w2b_ref[...] = w2_ref[...].astype(BF16)

    for slot in range(GATHER_SLOTS):
        @pl.when((i % GATHER_SLOTS == slot) & (i < n_used))
        def _compute():
            @pl.when(i + 2 < n_used)
            def _ahead_block():
                start_gather(tok2_ref, (slot + 2) % GATHER_SLOTS)

            wait_gather(slot)
            x = jnp.concatenate([xbuf_ref[slot, pl.ds(s, MOE_BLOCK, stride=tile), :] for s in range(tile)],
                                axis=1).astype(BF16)
            a = jnp.dot(x, w1b_ref[...], preferred_element_type=F32)
            b = jnp.dot(x, w3b_ref[...], preferred_element_type=F32)
            hm = ((a * _sigmoid(a)) * b).astype(BF16)
            y_ref[...] = jnp.dot(hm, w2b_ref[...], preferred_element_type=F32).astype(BF16)

    @pl.when(i >= n_used)
    def _skip():
        y_ref[...] = jnp.zeros_like(y_ref)


def _experts(blk_expert, n_used, buf_row, v_tiles, w1, w3, w2, layer):
    nblk = blk_expert.shape[0]
    tok3 = buf_row.reshape(nblk, 1, MOE_BLOCK)
    wmap = lambda i, be, nu: (layer, be[i], 0, 0)
    grid_spec = pltpu.PrefetchScalarGridSpec(
        num_scalar_prefetch=2,
        grid=(nblk,),
        in_specs=[pl.BlockSpec((None, 1, MOE_BLOCK), lambda i, be, nu: (i, 0, 0), memory_space=pltpu.SMEM),
                  pl.BlockSpec((None, 1, MOE_BLOCK), lambda i, be, nu: (jnp.minimum(i + 1, nblk - 1), 0, 0),
                               memory_space=pltpu.SMEM),
                  pl.BlockSpec((None, 1, MOE_BLOCK), lambda i, be, nu: (jnp.minimum(i + 2, nblk - 1), 0, 0),
                               memory_space=pltpu.SMEM),
                  pl.BlockSpec(memory_space=pl.ANY),
                  pl.BlockSpec((None, None, D_MODEL, D_EXPERT), wmap),
                  pl.BlockSpec((None, None, D_MODEL, D_EXPERT), wmap),
                  pl.BlockSpec((None, None, D_EXPERT, D_MODEL), wmap)],
        out_specs=pl.BlockSpec((MOE_BLOCK, D_MODEL), lambda i, be, nu: (i, 0)),
        scratch_shapes=[pltpu.VMEM((D_MODEL, D_EXPERT), BF16),
                        pltpu.VMEM((D_MODEL, D_EXPERT), BF16),
                        pltpu.VMEM((D_EXPERT, D_MODEL), BF16),
                        pltpu.VMEM((GATHER_SLOTS, MOE_BLOCK * ROW_TILE_SUBLANES, 128), F32),
                        pltpu.SemaphoreType.DMA((GATHER_SLOTS,))],
    )
    return pl.pallas_call(
        _expert_kernel,
        grid_spec=grid_spec,
        out_shape=jax.ShapeDtypeStruct((nblk * MOE_BLOCK, D_MODEL), BF16),
        compiler_params=_cparams(("arbitrary",)),
        name="experts",
    )(blk_expert, n_used, tok3, tok3, tok3, v_tiles, w1, w3, w2)


ROUTE_ROWS = 48


def _route_tile(lg, idx_ref, gate_ref, cnt_ref, base_ref):
    tr = lg.shape[0]

    @pl.when(pl.program_id(0) == 0)
    def _init():
        base_ref[...] = jnp.zeros_like(base_ref)

    lt = lg.T[0:ROUTE_ROWS, :]
    row = lax.broadcasted_iota(jnp.int32, lt.shape, 0)
    first_arg = lambda x, mx: jnp.min(jnp.where(x == mx, row, 128), axis=0, keepdims=True)
    is_g = row < N_GROUPS
    gl = jnp.where(is_g, lt, NEG_BIG)
    gmax = jnp.max(gl, axis=0, keepdims=True)
    pg_top = 1.0 / jnp.sum(jnp.where(is_g, jnp.exp(gl - gmax), 0.0), axis=0, keepdims=True)
    grp = first_arg(gl, gmax)
    e_lo = N_GROUPS + grp * EXPERTS_PER_GROUP
    el = jnp.where((row >= e_lo) & (row < e_lo + EXPERTS_PER_GROUP), lt, NEG_BIG)
    m1 = jnp.max(el, axis=0, keepdims=True)
    i1 = first_arg(el, m1)
    el2 = jnp.where(row == i1, NEG_BIG, el)
    m2 = jnp.max(el2, axis=0, keepdims=True)
    i2 = first_arg(el2, m2)
    p2 = jnp.exp(m2 - m1)
    g1 = pg_top / (1.0 + p2)
    g2 = pg_top * p2 / (1.0 + p2)

    hit1 = row == i1
    hit2 = row == i2
    onehot = jnp.where(hit1 | hit2, 1.0, 0.0).astype(BF16)
    s_i = lax.broadcasted_iota(jnp.int32, (tr, tr), 0)
    t_i = lax.broadcasted_iota(jnp.int32, (tr, tr), 1)
    earlier = jnp.where(s_i < t_i, 1.0, 0.0).astype(BF16)
    before = jnp.dot(onehot, earlier, preferred_element_type=F32) + base_ref[:, 0:1]
    rank1 = jnp.sum(jnp.where(hit1, before, 0.0), axis=0, keepdims=True).astype(jnp.int32)
    rank2 = jnp.sum(jnp.where(hit2, before, 0.0), axis=0, keepdims=True).astype(jnp.int32)
    base = base_ref[...] + jnp.dot(onehot, jnp.ones((tr, ROUTER_PAD), BF16), preferred_element_type=F32)
    base_ref[...] = base
    cnt_ref[...] = base

    r8 = lax.broadcasted_iota(jnp.int32, (8, tr), 0)
    idx_ref[...] = jnp.where(r8 == 0, i1 - N_GROUPS,
                             jnp.where(r8 == 1, i2 - N_GROUPS,
                                       jnp.where(r8 == 2, rank1, jnp.where(r8 == 3, rank2, 0))))
    r128 = lax.broadcasted_iota(jnp.int32, (ROUTER_PAD, tr), 0)
    gate_ref[...] = jnp.where(r128 == 0, g1, jnp.where(r128 == 1, g2, 0.0)).T


def _block_layout(idx, gate, cnt):
    n = idx.shape[1]
    expert = idx[0:TOP_K]
    rank = idx[TOP_K:2 * TOP_K]
    counts = cnt[N_GROUPS:N_GROUPS + N_EXPERTS, 0].astype(jnp.int32)
    a = n * TOP_K
    nblk = -(-a // MOE_BLOCK) + N_EXPERTS
    padded = (counts + MOE_BLOCK - 1) // MOE_BLOCK * MOE_BLOCK
    pend = jnp.cumsum(padded)
    start = pend - padded
    cstart = jnp.cumsum(counts) - counts
    hit = expert[:, None, :] == jnp.arange(N_EXPERTS, dtype=jnp.int32)[None, :, None]
    dest = jnp.sum(jnp.where(hit, start[None, :, None], 0), axis=1) + rank
    tok = jnp.broadcast_to(jnp.arange(n, dtype=jnp.int32)[None, :], (TOP_K, n))
    _, by_row = lax.sort_key_val(dest.reshape(-1), tok.reshape(-1))
    n_used = (pend[-1] // MOE_BLOCK).astype(jnp.int32)
    blk_row = jnp.minimum(jnp.arange(nblk, dtype=jnp.int32), n_used - 1) * MOE_BLOCK
    blk_expert = jnp.clip(jnp.sum((pend[None, :] <= blk_row[:, None]).astype(jnp.int32), axis=1), 0, N_EXPERTS - 1)
    row = jnp.arange(nblk * MOE_BLOCK, dtype=jnp.int32).reshape(nblk, MOE_BLOCK)
    in_expert = row - start[blk_expert][:, None]
    src = jnp.clip(in_expert + cstart[blk_expert][:, None], 0, a - 1)
    live = (in_expert < counts[blk_expert][:, None]) & (jnp.arange(nblk)[:, None] < n_used)
    buf_tok = jnp.where(live, by_row.at[src].get(mode="promise_in_bounds"), 0).reshape(-1)
    return buf_tok, blk_expert.astype(jnp.int32), n_used.reshape(1), dest, gate


def _moe(v, routing, w1, w3, w2, layer):
    buf_tok, blk_expert, n_used, pos, gate = _block_layout(*routing)
    yb = _experts(blk_expert, n_used, buf_tok * ROW_TILE_SUBLANES, v, w1, w3, w2, layer)
    y0 = yb.at[pos[0]].get(mode="promise_in_bounds")
    y1 = yb.at[pos[1]].get(mode="promise_in_bounds")
    return y0, y1, gate


def _final_kernel(h_ref, y0_ref, y1_ref, gk_ref, modp_ref, g_ref, o_ref):
    h = _combine(h_ref[...], y0_ref[...], y1_ref[...], gk_ref[...], modp_ref[5:6, :])
    o_ref[...] = _rms_rows(h) * g_ref[...]


def _final(h, y0, y1, gk, modp, g, dims):
    B, S, L, tr = dims
    n = h.shape[0]
    mod_idx, _ = _tile_maps(B, S, L, tr)
    row = lambda i: (i, 0)
    return pl.pallas_call(
        _final_kernel,
        grid=(n // tr,),
        in_specs=[pl.BlockSpec((tr, D_MODEL), row),
                  pl.BlockSpec((tr, D_MODEL), row),
                  pl.BlockSpec((tr, D_MODEL), row),
                  pl.BlockSpec((tr, ROUTER_PAD), row),
                  pl.BlockSpec((None, 8, D_MODEL), lambda i: (mod_idx(i), 0, 0)),
                  pl.BlockSpec((1, D_MODEL), lambda i: (0, 0))],
        out_specs=pl.BlockSpec((tr, D_MODEL), row),
        out_shape=jax.ShapeDtypeStruct((n, D_MODEL), F32),
        compiler_params=_cparams(("arbitrary",)),
        name="final_norm",
    )(h, y0, y1, gk, modp, g)


def _rotary_tables_even(S, L, B):
    nf = HEAD_DIM // 2
    inv = ROPE_BASE ** (-jnp.arange(nf, dtype=F32) / nf)
    pos = jnp.concatenate([L + jnp.arange(S), jnp.tile(jnp.arange(L), B)]).astype(F32)
    ang = pos[:, None] * inv[None, :]
    cos, sin = jnp.cos(ang), jnp.sin(ang)
    return jnp.concatenate([cos, cos], 1), jnp.concatenate([-sin, sin], 1)


def _rotary_tables_odd(S, L, B):
    nf = DA_DHEAD // 4
    inv = ROPE_BASE ** (-jnp.arange(nf, dtype=F32) / nf)
    t = jnp.arange(S)
    rows, cols = (t // GRID_W).astype(F32), (t % GRID_W).astype(F32)
    j = np.arange(128)
    f_idx = j % nf
    use_col = (j % DA_DHEAD) >= DA_DHEAD // 2
    first = (j % (2 * nf)) < nf
    ang = jnp.where(use_col[None, :], cols[:, None], rows[:, None]) * inv[f_idx][None, :]
    cos, sin = jnp.cos(ang), jnp.sin(ang)
    c_lat = cos
    s1_lat = jnp.where(first[None, :], -sin, 0.0)
    s2_lat = jnp.where(first[None, :], 0.0, sin)
    nctx = B * L
    c = jnp.concatenate([c_lat, jnp.ones((nctx, 128), F32)])
    s1 = jnp.concatenate([s1_lat, jnp.zeros((nctx, 128), F32)])
    s2 = jnp.concatenate([s2_lat, jnp.zeros((nctx, 128), F32)])
    return c, s1, s2


def kernel(x, c, ctx, c_ctx, w_mod, b_mod, norm1_g, norm2_g, w_in_even, ml_gate_b, ml_norm_g, ret_log_decay, ret_norm_g, w_out_even, w_in_odd, da_lambda, da_norm_g, w_out_odd, router_g_w, router_g_b, router_e_w, router_e_b, w1, w3, w2, final_norm_g):
    B, S, D = x.shape
    L = ctx.shape[1]
    depth = w_mod.shape[0]
    assert D == D_MODEL and depth == 2 and S % CHUNK == 0 and L % CHUNK == 0 and S % L == 0
    n_lat, n_ctx = B * S, B * L
    tr = _row_tile(S, n_ctx)
    dims = (B, S, L, tr)

    x2d, ctx2d = x.reshape(n_lat, D), ctx.reshape(n_ctx, D)

    cond8 = jnp.zeros((8, D), F32).at[:B].set(c).at[B].set(c_ctx)
    mod = jnp.pad(_modulation(cond8, w_mod, b_mod).reshape(depth, 8, 6, D), ((0, 0), (0, 0), (0, 2), (0, 0)))

    def router_w(l):
        wr = jnp.zeros((D, ROUTER_PAD), F32)
        wr = wr.at[:, :N_GROUPS].set(router_g_w[l]).at[:, N_GROUPS:N_GROUPS + N_EXPERTS].set(router_e_w[l])
        br = jnp.zeros((1, ROUTER_PAD), F32)
        br = br.at[0, :N_GROUPS].set(router_g_b[l]).at[0, N_GROUPS:N_GROUPS + N_EXPERTS].set(router_e_b[l])
        return wr, br

    wi = w_in_even[0]
    mq, mk, mv, mo, mi, mf, rq, rk, rv, rg = jnp.split(wi, np.cumsum(
        [512, 512, 512, 512, 8, 8, 512, 512, 512])[:9].tolist(), axis=1)
    w_main = jnp.concatenate([mq, mk, mv, rq, rk, rv, mo, rg], 1).astype(BF16)
    w_gates = jnp.pad(jnp.concatenate([mi, mf], 1), ((0, 0), (0, GATE_PAD - 16))).astype(BF16)
    gate_bias = jnp.pad(jnp.concatenate([ml_gate_b[0][:, 0].reshape(-1), ml_gate_b[0][:, 1].reshape(-1)]),
                        (0, GATE_PAD - 16)).reshape(1, GATE_PAD)
    cs, sn = _rotary_tables_even(S, L, B)
    proj, gates = _proj_even(x2d, ctx2d, mod[0], norm1_g[0].reshape(1, D), w_main, w_gates, cs, sn, dims)
    hf, hb = _scan(ret_log_decay[0].reshape(-1), proj, gates, gate_bias, B, S, L)
    wr, br = router_w(0)
    h, v, *routing = _post_even(hf, hb, proj, ml_norm_g[0].reshape(1, -1), ret_norm_g[0].reshape(1, -1),
                              x2d, ctx2d, mod[0], norm2_g[0].reshape(1, D), w_out_even[0].astype(BF16), wr, br, dims)
    y0, y1, gk = _moe(v, routing, w1, w3, w2, 0)

    lam_init = 0.8 - 0.6 * math.exp(-0.3 * 1)
    nqk = 4 * DA_HEADS * DA_DHEAD
    w_qk = w_in_odd[0][:, :nqk].astype(BF16)
    w_vt = w_in_odd[0][:, nqk:].T.astype(BF16)
    c2, s1, s2 = _rotary_tables_odd(S, L, B)
    h, qk, vt = _proj_odd(h, y0, y1, gk, mod[0], mod[1], norm1_g[1].reshape(1, D), w_qk, w_vt, c2, s1, s2, dims)
    att = _attention(qk, vt, da_lambda[0], da_norm_g[0], lam_init, B, S, L)
    wr, br = router_w(1)
    h_lat, v, *routing = _post_odd(att, h, mod[1], norm2_g[1].reshape(1, D), w_out_odd[0].astype(BF16), wr, br, dims)
    y0, y1, gk = _moe(v, routing, w1, w3, w2, 1)
    out = _final(h_lat, y0, y1, gk, mod[1], final_norm_g.reshape(1, D), dims)
    return out.reshape(B, S, D)
```
